```python
import jax
import jax.numpy as jnp
from jax import lax
import numpy as np

D_MODEL = 2048
BATCH = 8
SEQ = 4096
DEPTH = 1

PLE_DIM = 256
HEAD_DIM = 128
CONV_WIDTH = D_MODEL // 2
CONV_GROUPS = CONV_WIDTH // HEAD_DIM
CONV_KERNEL = 3
SB_HEADS = (D_MODEL // 2) // HEAD_DIM
SB_WIDTH = SB_HEADS * HEAD_DIM
MIX_WIDTH = CONV_WIDTH + SB_WIDTH
IN_PROJ_WIDTH = 3 * CONV_WIDTH + 3 * SB_WIDTH
SPLIT_POINTS = (CONV_WIDTH, 2 * CONV_WIDTH, 3 * CONV_WIDTH,
                3 * CONV_WIDTH + SB_WIDTH, 3 * CONV_WIDTH + 2 * SB_WIDTH)
Q_BLOCK = 128
ATTN_SCALE = HEAD_DIM ** -0.5
N_EXPERTS = 64
N_GROUPS = 8
EXPERTS_PER_GROUP = N_EXPERTS // N_GROUPS
TOPK_GROUPS = 4
TOP_K = 8
EXPERT_DIM = D_MODEL // 4
SHARED_DIM = D_MODEL // 4
ROUTED_SCALE = 2.5
NORM_EPS = 1e-6

kernel_name = 'hybrid_conv_stickbreaking_moe_ple'


def rms_norm(x, g):
    xf = x.astype(jnp.float32)
    y = xf * lax.rsqrt(jnp.mean(xf * xf, axis=-1, keepdims=True) + NORM_EPS)
    return (y * g.astype(jnp.float32)).astype(x.dtype)


def short_conv_mixer(h, b_gate, c_gate, conv_w):
    seq = h.shape[1]
    u = c_gate * h
    u_pad = jnp.pad(u, ((0, 0), (CONV_KERNEL - 1, 0), (0, 0)))
    y = conv_w[0] * u_pad[:, 0:seq]
    for j in range(1, CONV_KERNEL):
        y = y + conv_w[j] * u_pad[:, j:j + seq]
    return b_gate * y


def stick_breaking_attention(q, k, v):
    seq = q.shape[2]
    outs = []
    for i in range(seq // Q_BLOCK):
        q0 = i * Q_BLOCK
        kv_len = q0 + Q_BLOCK
        qb = q[:, :, q0:kv_len].astype(jnp.float32)
        kb = k[:, :, :kv_len].astype(jnp.float32)
        vb = v[:, :, :kv_len].astype(jnp.float32)
        z = jnp.einsum('bhqd,bhkd->bhqk', qb, kb) * ATTN_SCALE
        t_pos = q0 + jnp.arange(Q_BLOCK)[:, None]
        s_pos = jnp.arange(kv_len)[None, :]
        causal = s_pos < t_pos
        log_keep = jnp.where(causal, jax.nn.log_sigmoid(-z), 0.0)
        log_between = lax.cumsum(log_keep, axis=3, reverse=True) - log_keep
        a = jnp.where(causal, jnp.exp(jax.nn.log_sigmoid(z) + log_between), 0.0)
        outs.append(jnp.einsum('bhqk,bhkd->bhqd', a, vb))
    return jnp.concatenate(outs, axis=2).astype(q.dtype)


def hybrid_mixer(h, norm_g, w_in, conv_w, gnorm_conv_g, gnorm_sb_g, w_out):
    bsz, seq, _ = h.shape
    hn = rms_norm(h, norm_g)
    proj = jnp.einsum('bsd,de->bse', hn, w_in)
    c_h, c_b, c_c, q, k, v = jnp.split(proj, SPLIT_POINTS, axis=-1)
    conv_out = short_conv_mixer(c_h, c_b, c_c, conv_w)

    def heads(t):
        return t.reshape(bsz, seq, SB_HEADS, HEAD_DIM).transpose(0, 2, 1, 3)

    sb = stick_breaking_attention(heads(q), heads(k), heads(v))
    sb = sb.transpose(0, 2, 1, 3).reshape(bsz, seq, SB_WIDTH)
    y = jnp.concatenate([rms_norm(conv_out, gnorm_conv_g), rms_norm(sb, gnorm_sb_g)], axis=-1)
    return jnp.einsum('bse,ed->bsd', y, w_out)


def swiglu(x, w_gate, w_up, w_down):
    return (jax.nn.silu(x @ w_gate) * (x @ w_up)) @ w_down


def moe_ffn(h, norm_g, w_router, router_bias, w_exp_gate, w_exp_up, w_exp_down,
            w_sh_gate, w_sh_up, w_sh_down):
    bsz, seq, d = h.shape
    xt = rms_norm(h, norm_g).reshape(bsz * seq, d)
    scores = jax.nn.sigmoid((xt @ w_router).astype(jnp.float32))
    biased = scores + router_bias.astype(jnp.float32)
    grouped = biased.reshape(-1, N_GROUPS, EXPERTS_PER_GROUP)
    group_score = jnp.sum(lax.top_k(grouped, 2)[0], axis=-1)
    _, top_groups = lax.top_k(group_score, TOPK_GROUPS)
    group_ok = jnp.any(top_groups[..., None] == jnp.arange(N_GROUPS), axis=1)
    expert_ok = jnp.repeat(group_ok, EXPERTS_PER_GROUP, axis=1)
    _, top_idx = lax.top_k(jnp.where(expert_ok, biased, -jnp.inf), TOP_K)
    w = jnp.take_along_axis(scores, top_idx, axis=1)
    w = w / jnp.sum(w, axis=-1, keepdims=True) * ROUTED_SCALE
    gates = jnp.sum(jnp.where(top_idx[..., None] == jnp.arange(N_EXPERTS), w[..., None], 0.0),
                    axis=1).astype(xt.dtype)

    def expert_step(acc, ew):
        wg, wu, wd, g = ew
        return acc + g[:, None] * swiglu(xt, wg, wu, wd), None

    routed, _ = lax.scan(expert_step, jnp.zeros_like(xt), (w_exp_gate, w_exp_up, w_exp_down, gates.T))
    out = routed + swiglu(xt, w_sh_gate, w_sh_up, w_sh_down)
    return out.reshape(bsz, seq, d)


def per_layer_embedding(h, p_i, norm_g, w_gate, w_proj):
    gate = jax.nn.sigmoid(rms_norm(h, norm_g) @ w_gate)
    return gate * (p_i @ w_proj)


def setup_inputs(seed: int = 0) -> dict:
    key = jax.random.key(seed)
    ks = jax.random.split(key, 21)

    def normal(k, shape, scale):
        return jax.random.normal(k, shape, jnp.float32) * scale

    def gain(k, shape):
        return 1.0 + 0.1 * jax.random.normal(k, shape, jnp.float32)

    return {
        'x': normal(ks[0], (BATCH, SEQ, D_MODEL), 1.0),
        'p': normal(ks[1], (DEPTH, BATCH, SEQ, PLE_DIM), 1.0),
        'norm_mix_g': gain(ks[2], (DEPTH, D_MODEL)),
        'w_in': normal(ks[3], (DEPTH, D_MODEL, IN_PROJ_WIDTH), D_MODEL ** -0.5),
        'conv_w': normal(ks[4], (DEPTH, CONV_KERNEL, CONV_WIDTH), CONV_KERNEL ** -0.5),
        'gnorm_conv_g': gain(ks[5], (DEPTH, CONV_WIDTH)),
        'gnorm_sb_g': gain(ks[6], (DEPTH, SB_WIDTH)),
        'w_out': normal(ks[7], (DEPTH, MIX_WIDTH, D_MODEL), MIX_WIDTH ** -0.5),
        'norm_ffn_g': gain(ks[8], (DEPTH, D_MODEL)),
        'w_router': normal(ks[9], (DEPTH, D_MODEL, N_EXPERTS), D_MODEL ** -0.5),
        'router_bias': normal(ks[10], (DEPTH, N_EXPERTS), 0.01),
        'w_exp_gate': normal(ks[11], (DEPTH, N_EXPERTS, D_MODEL, EXPERT_DIM), D_MODEL ** -0.5),
        'w_exp_up': normal(ks[12], (DEPTH, N_EXPERTS, D_MODEL, EXPERT_DIM), D_MODEL ** -0.5),
        'w_exp_down': normal(ks[13], (DEPTH, N_EXPERTS, EXPERT_DIM, D_MODEL), EXPERT_DIM ** -0.5),
        'w_sh_gate': normal(ks[14], (DEPTH, D_MODEL, SHARED_DIM), D_MODEL ** -0.5),
        'w_sh_up': normal(ks[15], (DEPTH, D_MODEL, SHARED_DIM), D_MODEL ** -0.5),
        'w_sh_down': normal(ks[16], (DEPTH, SHARED_DIM, D_MODEL), SHARED_DIM ** -0.5),
        'norm_ple_g': gain(ks[17], (DEPTH, D_MODEL)),
        'w_ple_gate': normal(ks[18], (DEPTH, D_MODEL, D_MODEL), D_MODEL ** -0.5),
        'w_ple_proj': normal(ks[19], (DEPTH, PLE_DIM, D_MODEL), PLE_DIM ** -0.5),
        'norm_final_g': gain(ks[20], (D_MODEL,)),
    }


def reference(x, p, norm_mix_g, w_in, conv_w, gnorm_conv_g, gnorm_sb_g, w_out,
              norm_ffn_g, w_router, router_bias, w_exp_gate, w_exp_up, w_exp_down,
              w_sh_gate, w_sh_up, w_sh_down, norm_ple_g, w_ple_gate, w_ple_proj,
              norm_final_g):
    h = x
    for i in range(DEPTH):
        h = h + hybrid_mixer(h, norm_mix_g[i], w_in[i], conv_w[i], gnorm_conv_g[i],
                             gnorm_sb_g[i], w_out[i])
        h = h + moe_ffn(h, norm_ffn_g[i], w_router[i], router_bias[i], w_exp_gate[i],
                        w_exp_up[i], w_exp_down[i], w_sh_gate[i], w_sh_up[i], w_sh_down[i])
        h = h + per_layer_embedding(h, p[i], norm_ple_g[i], w_ple_gate[i], w_ple_proj[i])
    return rms_norm(h, norm_final_g)
```

```python
import functools

import jax
import jax.numpy as jnp
from jax import lax
from jax.experimental import pallas as pl
from jax.experimental.pallas import tpu as pltpu

NORM_EPS = 1e-6
HEAD_DIM = 128
CONV_KERNEL = 3
N_GROUPS = 8
TOPK_GROUPS = 4
TOP_K = 8
ROUTED_SCALE = 2.5

V7X_VMEM_BYTES = 64 * 1024 * 1024
V7X_SUBLANES = 8
V7X_LANES = 128

F32 = jnp.float32
BF16 = jnp.bfloat16


def _vmem_limit(pipelined_bytes, resident_bytes):
    want = 2 * pipelined_bytes + resident_bytes
    return int(min(want, V7X_VMEM_BYTES - 4 * 1024 * 1024))


def _nbytes(shape, dtype):
    n = 1
    for s in shape:
        n *= s
    return n * jnp.dtype(dtype).itemsize


def _rms(y, g):
    return y * lax.rsqrt(jnp.mean(y * y, axis=-1, keepdims=True) + NORM_EPS) * g


def _sigmoid(a):
    return 1.0 / (1.0 + jnp.exp(-a))


def _dot(a, b):
    return jnp.dot(a, b, preferred_element_type=F32)


def _norm_inproj_kernel(x_ref, g_ref, w_ref, cs_ref, o_ref, hn_ref):
    @pl.when(pl.program_id(1) == 0)
    def _():
        hn_ref[...] = _rms(x_ref[...], g_ref[...]).astype(hn_ref.dtype)

    o_ref[...] = (_dot(hn_ref[...], w_ref[...]) * cs_ref[...]).astype(o_ref.dtype)


def _norm_inproj(h, g, w, col_scale, *, tm=1024, tn=1024):
    t, d = h.shape
    n = w.shape[1]
    blocks = _nbytes((tm, d), F32) + _nbytes((d, tn), BF16) + _nbytes((tm, tn), BF16)
    temps = _nbytes((tm, d), BF16) + 2 * _nbytes((tm, d), F32) + _nbytes((tm, tn), F32)
    return pl.pallas_call(
        _norm_inproj_kernel,
        grid=(t // tm, n // tn),
        in_specs=[
            pl.BlockSpec((tm, d), lambda i, j: (i, 0)),
            pl.BlockSpec((1, d), lambda i, j: (0, 0)),
            pl.BlockSpec((d, tn), lambda i, j: (0, j)),
            pl.BlockSpec((1, tn), lambda i, j: (0, j)),
        ],
        out_specs=pl.BlockSpec((tm, tn), lambda i, j: (i, j)),
        out_shape=jax.ShapeDtypeStruct((t, n), BF16),
        scratch_shapes=[pltpu.VMEM((tm, d), BF16)],
        compiler_params=pltpu.CompilerParams(
            dimension_semantics=("arbitrary", "arbitrary"),
            vmem_limit_bytes=_vmem_limit(blocks, temps),
        ),
        name="norm_inproj",
    )(h, g, w, col_scale)


def _sb_attn_kernel(q_ref, k_ref, v_ref, uo_ref, o_ref, acc_ref, run_ref, *, blk):
    qi = pl.program_id(2)
    q = q_ref[0]
    uo = uo_ref[...]
    acc_ref[...] = jnp.zeros_like(acc_ref)
    run_ref[...] = jnp.zeros_like(run_ref)

    def block(j, masked):
        start = pl.multiple_of(j * blk, blk)
        kb = k_ref[0, pl.ds(start, blk), :]
        vb = v_ref[0, pl.ds(start, blk), :]
        z = lax.dot_general(q, kb, (((1,), (1,)), ((), ())), preferred_element_type=F32)
        sp = jnp.maximum(z, 0.0) + jnp.log1p(jnp.exp(-jnp.abs(z)))
        log_keep = -sp
        if masked:
            row = lax.broadcasted_iota(jnp.int32, (blk, blk), 0)
            col = lax.broadcasted_iota(jnp.int32, (blk, blk), 1)
            causal = col < row
            log_keep = jnp.where(causal, log_keep, 0.0)
        cs = _dot(log_keep.astype(BF16), uo)
        a = jnp.exp(z - sp + cs[:, :blk] + run_ref[...])
        if masked:
            a = jnp.where(causal, a, 0.0)
        acc_ref[...] += _dot(a.astype(BF16), vb)
        run_ref[...] += cs[:, blk:]

    block(qi, True)

    def body(n, carry):
        block(qi - 1 - n, False)
        return carry

    lax.fori_loop(0, qi, body, 0)
    o_ref[0] = acc_ref[...].astype(o_ref.dtype)


def _sb_attention(proj3, *, conv_width, sb_width, blk=256):
    b, s, _ = proj3.shape
    heads = sb_width // HEAD_DIM
    q_off = 3 * conv_width // HEAD_DIM
    k_off = q_off + heads
    v_off = k_off + heads
    later = (lax.broadcasted_iota(jnp.int32, (blk, blk), 0) > lax.broadcasted_iota(jnp.int32, (blk, blk), 1))
    uo = jnp.concatenate([later.astype(BF16), jnp.ones((blk, blk), BF16)], axis=1)
    blocks = 2 * _nbytes((blk, HEAD_DIM), BF16) + 2 * _nbytes((s, HEAD_DIM), BF16) + _nbytes((blk, 2 * blk), BF16)
    temps = 8 * _nbytes((blk, 2 * blk), F32)
    return pl.pallas_call(
        functools.partial(_sb_attn_kernel, blk=blk),
        grid=(b, heads, s // blk),
        in_specs=[
            pl.BlockSpec((1, blk, HEAD_DIM), lambda bi, hi, qi: (bi, qi, q_off + hi)),
            pl.BlockSpec((1, s, HEAD_DIM), lambda bi, hi, qi: (bi, 0, k_off + hi)),
            pl.BlockSpec((1, s, HEAD_DIM), lambda bi, hi, qi: (bi, 0, v_off + hi)),
            pl.BlockSpec((blk, 2 * blk), lambda bi, hi, qi: (0, 0)),
        ],
        out_specs=pl.BlockSpec((1, blk, HEAD_DIM), lambda bi, hi, qi: (bi, qi, hi)),
        out_shape=jax.ShapeDtypeStruct((b, s, sb_width), BF16),
        scratch_shapes=[pltpu.VMEM((blk, HEAD_DIM), F32), pltpu.VMEM((blk, blk), F32)],
        compiler_params=pltpu.CompilerParams(
            dimension_semantics=("arbitrary", "arbitrary", "arbitrary"),
            vmem_limit_bytes=_vmem_limit(blocks, temps),
        ),
        name="sb_attn",
    )(proj3, proj3, proj3, uo)


def _router_gates(logits_t, bias):
    n_exp, tm = logits_t.shape
    per_group = n_exp // N_GROUPS
    scores = _sigmoid(logits_t)
    biased = scores + bias

    grouped = biased.reshape(N_GROUPS, per_group, tm)
    top1 = jnp.max(grouped, axis=1, keepdims=True)
    n_top = jnp.sum(jnp.where(grouped == top1, 1.0, 0.0), axis=1, keepdims=True)
    below = jnp.max(jnp.where(grouped < top1, grouped, -jnp.inf), axis=1, keepdims=True)
    group_score = (top1 + jnp.where(n_top >= 2.0, top1, below)).reshape(N_GROUPS, tm)

    def rank_of(vals):
        n = vals.shape[0]
        idx = lax.broadcasted_iota(jnp.int32, vals.shape, 0)
        rank = jnp.zeros(vals.shape, F32)
        for other in range(n):
            o = vals[other:other + 1, :]
            rank = rank + jnp.where(o > vals, 1.0, 0.0) + jnp.where(o == vals, jnp.where(idx > other, 1.0, 0.0), 0.0)
        return rank

    group_ok = rank_of(group_score) < float(TOPK_GROUPS)
    expert_ok = jnp.broadcast_to(group_ok.reshape(N_GROUPS, 1, tm), (N_GROUPS, per_group, tm)).reshape(n_exp, tm)
    masked = jnp.where(expert_ok, biased, -jnp.inf)
    chosen = rank_of(masked) < float(TOP_K)
    w = jnp.where(chosen, scores, 0.0)
    return w / jnp.sum(w, axis=0, keepdims=True) * ROUTED_SCALE


def _mixer_out_kernel(ch_ref, cb_ref, cc_ref, hh_ref, hc_ref, sb_ref, x_ref, cw_ref, gc_ref, gs_ref,
                      wo_ref, gf_ref, wr_ref, rb_ref, h1_ref, xt_ref, gates_ref, *, tiles_per_seq):
    i = pl.program_id(0)
    u = cc_ref[...].astype(F32) * ch_ref[...].astype(F32)
    halo = hc_ref[...].astype(F32) * hh_ref[...].astype(F32)
    halo = jnp.where(i % tiles_per_seq == 0, 0.0, halo)
    prev1 = halo[V7X_SUBLANES - 1:V7X_SUBLANES, :]
    prev2 = halo[V7X_SUBLANES - 2:V7X_SUBLANES - 1, :]
    row = lax.broadcasted_iota(jnp.int32, u.shape, 0)
    u1 = jnp.where(row == 0, prev1, pltpu.roll(u, 1, 0))
    u2 = jnp.where(row == 0, prev2, jnp.where(row == 1, prev1, pltpu.roll(u, 2, 0)))
    cw = cw_ref[...]
    conv = cb_ref[...].astype(F32) * (cw[0:1, :] * u2 + cw[1:2, :] * u1 + cw[2:3, :] * u)

    y = jnp.concatenate([_rms(conv, gc_ref[...]), _rms(sb_ref[...].astype(F32), gs_ref[...])], axis=-1)
    h1 = x_ref[...] + _dot(y.astype(BF16), wo_ref[...])
    h1_ref[...] = h1
    xt = _rms(h1, gf_ref[...])
    xt_ref[...] = xt.astype(xt_ref.dtype)

    logits = jnp.dot(xt, wr_ref[...], preferred_element_type=F32, precision=lax.Precision.HIGHEST)
    gates_ref[...] = _router_gates(logits.T, rb_ref[...]).T


def _mixer_out(proj, sb, h, conv_w, gc, gs, w_out, gf, w_router, router_bias, *, seq, conv_width, tm=256):
    t, d = h.shape
    sb_width = sb.shape[1]
    n_exp = w_router.shape[1]
    halo_blocks = tm // V7X_SUBLANES
    conv_spec = lambda c: pl.BlockSpec((tm, conv_width), lambda i: (i, c))
    halo_spec = lambda c: pl.BlockSpec((V7X_SUBLANES, conv_width), lambda i: (jnp.maximum(i * halo_blocks - 1, 0), c))
    full = lambda shape: pl.BlockSpec(shape, lambda i: (0,) * len(shape))
    blocks = (3 * _nbytes((tm, conv_width), BF16) + _nbytes((tm, sb_width), BF16) + 2 * _nbytes((tm, d), F32)
              + _nbytes((tm, d), BF16) + _nbytes(w_out.shape, BF16))
    temps = 10 * _nbytes((tm, d), F32)
    return pl.pallas_call(
        functools.partial(_mixer_out_kernel, tiles_per_seq=seq // tm),
        grid=(t // tm,),
        in_specs=[
            conv_spec(0), conv_spec(1), conv_spec(2), halo_spec(0), halo_spec(2),
            pl.BlockSpec((tm, sb_width), lambda i: (i, 0)),
            pl.BlockSpec((tm, d), lambda i: (i, 0)),
            full(conv_w.shape), full(gc.shape), full(gs.shape), full(w_out.shape), full(gf.shape),
            full(w_router.shape), full(router_bias.shape),
        ],
        out_specs=[
            pl.BlockSpec((tm, d), lambda i: (i, 0)),
            pl.BlockSpec((tm, d), lambda i: (i, 0)),
            pl.BlockSpec((tm, n_exp), lambda i: (i, 0)),
        ],
        out_shape=[
            jax.ShapeDtypeStruct((t, d), F32),
            jax.ShapeDtypeStruct((t, d), BF16),
            jax.ShapeDtypeStruct((t, n_exp), F32),
        ],
        compiler_params=pltpu.CompilerParams(
            dimension_semantics=("arbitrary",),
            vmem_limit_bytes=_vmem_limit(blocks, temps),
        ),
        name="mixer_out",
    )(proj, proj, proj, proj, proj, sb, h, conv_w, gc, gs, w_out, gf, w_router, router_bias)


def _swiglu_hidden(xt, wg, wu):
    a = _dot(xt, wg)
    return a * _sigmoid(a) * _dot(xt, wu)


def _moe_dense_kernel(xt_ref, gates_ref, h1_ref, wg_ref, wu_ref, wd_ref, sg_ref, su_ref, sd_ref, o_ref):
    e = pl.program_id(1)
    xt = xt_ref[...]

    @pl.when(e == 0)
    def _():
        hs = _swiglu_hidden(xt, sg_ref[...], su_ref[...])
        o_ref[...] = h1_ref[...] + _dot(hs.astype(BF16), sd_ref[...])

    gates = gates_ref[...]
    lane = lax.broadcasted_iota(jnp.int32, gates.shape, 1)
    gate = jnp.sum(jnp.where(lane == e, gates, 0.0), axis=1, keepdims=True)
    hid = _swiglu_hidden(xt, wg_ref[0], wu_ref[0]) * gate
    o_ref[...] += _dot(hid.astype(BF16), wd_ref[0])


def _moe_dense(xt, gates, h1, wg, wu, wd, sg, su, sd, *, tm=512):
    t, d = xt.shape
    n_exp, _, f = wg.shape
    fs = sg.shape[1]
    full = lambda shape: pl.BlockSpec(shape, lambda i, e: (0,) * len(shape))
    blocks = (_nbytes((tm, d), BF16) + 2 * _nbytes((tm, d), F32) + 3 * _nbytes((d, f), BF16)
              + 3 * _nbytes((d, fs), BF16) + _nbytes((tm, V7X_LANES), F32))
    temps = 4 * _nbytes((tm, max(f, fs)), F32) + _nbytes((tm, d), F32)
    return pl.pallas_call(
        _moe_dense_kernel,
        grid=(t // tm, n_exp),
        in_specs=[
            pl.BlockSpec((tm, d), lambda i, e: (i, 0)),
            pl.BlockSpec((tm, n_exp), lambda i, e: (i, 0)),
            pl.BlockSpec((tm, d), lambda i, e: (i, 0)),
            pl.BlockSpec((1, d, f), lambda i, e: (e, 0, 0)),
            pl.BlockSpec((1, d, f), lambda i, e: (e, 0, 0)),
            pl.BlockSpec((1, f, d), lambda i, e: (e, 0, 0)),
            full(sg.shape), full(su.shape), full(sd.shape),
        ],
        out_specs=pl.BlockSpec((tm, d), lambda i, e: (i, 0)),
        out_shape=jax.ShapeDtypeStruct((t, d), F32),
        compiler_params=pltpu.CompilerParams(
            dimension_semantics=("arbitrary", "arbitrary"),
            vmem_limit_bytes=_vmem_limit(blocks, temps),
        ),
        name="moe_dense",
    )(xt, gates, h1, wg, wu, wd, sg, su, sd)


def _ple_kernel(h2_ref, p_ref, gp_ref, wg_ref, wp_ref, gf_ref, o_ref, *, final_norm):
    h2 = h2_ref[...]
    gate = _sigmoid(_dot(_rms(h2, gp_ref[...]).astype(BF16), wg_ref[...]))
    h3 = h2 + gate * _dot(p_ref[...].astype(BF16), wp_ref[...])
    o_ref[...] = _rms(h3, gf_ref[...]) if final_norm else h3


def _ple(h2, p, gp, wg, wp, gf, *, final_norm, tm=512):
    t, d = h2.shape
    pd = p.shape[1]
    full = lambda shape: pl.BlockSpec(shape, lambda i: (0,) * len(shape))
    blocks = 2 * _nbytes((tm, d), F32) + _nbytes((tm, pd), F32) + _nbytes(wg.shape, BF16) + _nbytes(wp.shape, BF16)
    temps = 6 * _nbytes((tm, d), F32)
    return pl.pallas_call(
        functools.partial(_ple_kernel, final_norm=final_norm),
        grid=(t // tm,),
        in_specs=[
            pl.BlockSpec((tm, d), lambda i: (i, 0)),
            pl.BlockSpec((tm, pd), lambda i: (i, 0)),
            full(gp.shape), full(wg.shape), full(wp.shape), full(gf.shape),
        ],
        out_specs=pl.BlockSpec((tm, d), lambda i: (i, 0)),
        out_shape=jax.ShapeDtypeStruct((t, d), F32),
        compiler_params=pltpu.CompilerParams(
            dimension_semantics=("arbitrary",),
            vmem_limit_bytes=_vmem_limit(blocks, temps),
        ),
        name="ple_final",
    )(h2, p, gp, wg, wp, gf)


def kernel(x, p, norm_mix_g, w_in, conv_w, gnorm_conv_g, gnorm_sb_g, w_out, norm_ffn_g, w_router, router_bias,
           w_exp_gate, w_exp_up, w_exp_down, w_sh_gate, w_sh_up, w_sh_down, norm_ple_g, w_ple_gate, w_ple_proj,
           norm_final_g):
    bsz, seq, d = x.shape
    depth = p.shape[0]
    t = bsz * seq
    conv_width = conv_w.shape[-1]
    sb_width = gnorm_sb_g.shape[-1]
    n_proj = w_in.shape[-1]
    assert n_proj == 3 * conv_width + 3 * sb_width

    q_lo = 3 * conv_width
    col = jnp.arange(n_proj)
    col_scale = jnp.where((col >= q_lo) & (col < q_lo + sb_width), HEAD_DIM ** -0.5, 1.0).astype(F32)[None, :]
    row = lambda v: v.astype(F32)[None, :]

    h = x.reshape(t, d)
    for i in range(depth):
        proj = _norm_inproj(h, row(norm_mix_g[i]), w_in[i].astype(BF16), col_scale)
        sb = _sb_attention(proj.reshape(bsz, seq, n_proj), conv_width=conv_width, sb_width=sb_width)
        h1, xt, gates = _mixer_out(
            proj, sb.reshape(t, sb_width), h, conv_w[i], row(gnorm_conv_g[i]), row(gnorm_sb_g[i]),
            w_out[i].astype(BF16), row(norm_ffn_g[i]), w_router[i], router_bias[i].astype(F32)[:, None],
            seq=seq, conv_width=conv_width)
        h2 = _moe_dense(xt, gates, h1, w_exp_gate[i].astype(BF16), w_exp_up[i].astype(BF16),
                        w_exp_down[i].astype(BF16), w_sh_gate[i].astype(BF16), w_sh_up[i].astype(BF16),
                        w_sh_down[i].astype(BF16))
        h = _ple(h2, p[i].reshape(t, -1), row(norm_ple_g[i]), w_ple_gate[i].astype(BF16),
                 w_ple_proj[i].astype(BF16), row(norm_final_g), final_norm=(i == depth - 1))
    return h.reshape(bsz, seq, d)
```

```python
import functools

import jax
import jax.numpy as jnp
from jax import lax
from jax.experimental import pallas as pl
from jax.experimental.pallas import tpu as pltpu

NORM_EPS = 1e-6
HEAD_DIM = 128
CONV_KERNEL = 3
N_GROUPS = 8
TOPK_GROUPS = 4
TOP_K = 8
ROUTED_SCALE = 2.5
LOG2_E = 1.4426950408889634

V7X_VMEM_BYTES = 64 * 1024 * 1024
V7X_SUBLANES = 8
V7X_LANES = 128

F32 = jnp.float32
BF16 = jnp.bfloat16
I32 = jnp.int32

ATTN_BLOCK = 256
EXPERT_TILE = 512
EXPERT_TILE_LOG2 = 9
assert 1 << EXPERT_TILE_LOG2 == EXPERT_TILE


def _vmem_limit(pipelined_bytes, resident_bytes):
    want = 2 * pipelined_bytes + resident_bytes
    return int(min(want, V7X_VMEM_BYTES - 4 * 1024 * 1024))


def _nbytes(shape, dtype):
    n = 1
    for s in shape:
        n *= s
    return n * jnp.dtype(dtype).itemsize


def _resident(shape, n_grid_axes, n_prefetch=0):
    zeros = (0,) * len(shape)
    return pl.BlockSpec(shape, lambda *_: zeros, pipeline_mode=pl.Buffered(1))


def _rms(y, g):
    return y * lax.rsqrt(jnp.mean(y * y, axis=-1, keepdims=True) + NORM_EPS) * g


def _sigmoid(a):
    return 1.0 / (1.0 + jnp.exp(-a))


def _softplus2(z):
    return jnp.maximum(z, 0.0) + jnp.log2(1.0 + jnp.exp2(-jnp.abs(z)))


def _dot(a, b):
    return jnp.dot(a, b, preferred_element_type=F32)


def _dot_nt(a, b):
    return lax.dot_general(a, b, (((1,), (1,)), ((), ())), preferred_element_type=F32)


def _swiglu_hidden(x, wg, wu):
    a = _dot(x, wg)
    return a * _sigmoid(a) * _dot(x, wu)


def _norm_inproj_kernel(x_ref, g_ref, w_ref, cs_ref, wvt_ref, o_ref, vt_ref, hn_ref, *, n_col_tiles):
    j = pl.program_id(1)

    @pl.when(j == 0)
    def _():
        hn_ref[...] = _rms(x_ref[...], g_ref[...]).astype(hn_ref.dtype)

    @pl.when(j < n_col_tiles)
    def _():
        o_ref[...] = (_dot(hn_ref[...], w_ref[...]) * cs_ref[...]).astype(o_ref.dtype)

    @pl.when(j == n_col_tiles)
    def _():
        vt = _dot_nt(wvt_ref[...], hn_ref[...]).astype(vt_ref.dtype)
        kb = vt_ref.shape[2]
        for c in range(vt_ref.shape[0]):
            vt_ref[c] = vt[:, c * kb:(c + 1) * kb]


def _norm_inproj(h, g, w, col_scale, wvt, *, key_block, tm=1024, tn=1024):
    t, d = h.shape
    n = w.shape[1]
    vw = wvt.shape[0]
    n_col_tiles = n // tn
    last = n_col_tiles - 1
    blocks = _nbytes((tm, d), F32) + _nbytes((d, tn), BF16) + _nbytes((tm, tn), BF16) + _nbytes((vw, tm), BF16)
    temps = _nbytes((tm, d), BF16) + 2 * _nbytes((tm, d), F32) + _nbytes((tm, tn), F32) + _nbytes((vw, d), BF16)
    return pl.pallas_call(
        functools.partial(_norm_inproj_kernel, n_col_tiles=n_col_tiles),
        grid=(t // tm, n_col_tiles + 1),
        in_specs=[
            pl.BlockSpec((tm, d), lambda i, j: (i, 0)),
            _resident((1, d), 2),
            pl.BlockSpec((d, tn), lambda i, j: (0, jnp.minimum(j, last))),
            pl.BlockSpec((1, tn), lambda i, j: (0, jnp.minimum(j, last))),
            _resident((vw, d), 2),
        ],
        out_specs=[
            pl.BlockSpec((tm, tn), lambda i, j: (i, jnp.minimum(j, last))),
            pl.BlockSpec((tm // key_block, vw, key_block), lambda i, j: (i, 0, 0)),
        ],
        out_shape=[jax.ShapeDtypeStruct((t, n), BF16), jax.ShapeDtypeStruct((t // key_block, vw, key_block), BF16)],
        scratch_shapes=[pltpu.VMEM((tm, d), BF16)],
        compiler_params=pltpu.CompilerParams(
            dimension_semantics=("arbitrary", "arbitrary"),
            vmem_limit_bytes=_vmem_limit(blocks, temps),
        ),
        name="norm_inproj",
    )(h, g, w, col_scale, wvt)


def _sb_attn_kernel(q_ref, k_ref, vt_ref, lo_ref, o_ref, acc_ref, run_ref, *, blk, heads):
    qi = pl.program_id(2)
    lo = lo_ref[...]
    acc_ref[...] = jnp.zeros_like(acc_ref)
    run_ref[...] = jnp.zeros_like(run_ref)

    def block(j, masked):
        start = pl.multiple_of(j * blk, blk)
        cols = [slice(h * HEAD_DIM, (h + 1) * HEAD_DIM) for h in range(heads)]
        if masked:
            causal = lax.broadcasted_iota(I32, (blk, blk), 0) < lax.broadcasted_iota(I32, (blk, blk), 1)
        zs = [_dot_nt(k_ref[0, pl.ds(start, blk), c], q_ref[0, :, c]) for c in cols]
        sps = [_softplus2(z) for z in zs]
        keeps = [jnp.where(causal, sp, 0.0) if masked else sp for sp in sps]
        css = [_dot(lo, keep.astype(BF16)) for keep in keeps]
        for h in range(heads):
            run = run_ref[h]
            run_all = jnp.concatenate([run] * (blk // V7X_SUBLANES), axis=0)
            a = jnp.exp2(zs[h] - sps[h] + css[h][:blk] + run_all)
            if masked:
                a = jnp.where(causal, a, 0.0)
            acc_ref[h] += _dot(vt_ref[j, cols[h], :], a.astype(BF16))
            run_ref[h] = run + css[h][blk:]

    block(qi, True)

    def body(n, carry):
        block(qi - 1 - n, False)
        return carry

    lax.fori_loop(0, qi, body, 0)
    for h in range(heads):
        o_ref[0, :, h * HEAD_DIM:(h + 1) * HEAD_DIM] = acc_ref[h].T.astype(o_ref.dtype)


def _sb_attention(proj3, vt, *, conv_width, sb_width, blk, heads=4):
    b, s, _ = proj3.shape
    gw = heads * HEAD_DIM
    groups = sb_width // gw
    q_off = 3 * conv_width // gw
    k_off = q_off + groups
    later = lax.broadcasted_iota(I32, (blk, blk), 1) > lax.broadcasted_iota(I32, (blk, blk), 0)
    lo = -jnp.concatenate([later.astype(BF16), jnp.ones((V7X_SUBLANES, blk), BF16)], axis=0)
    blocks = 2 * _nbytes((blk, gw), BF16) + 2 * _nbytes((s, gw), BF16)
    temps = heads * 8 * _nbytes((blk, blk), F32)
    return pl.pallas_call(
        functools.partial(_sb_attn_kernel, blk=blk, heads=heads),
        grid=(b, groups, s // blk),
        in_specs=[
            pl.BlockSpec((1, blk, gw), lambda bi, gi, qi: (bi, qi, q_off + gi)),
            pl.BlockSpec((1, s, gw), lambda bi, gi, qi: (bi, 0, k_off + gi)),
            pl.BlockSpec((s // blk, gw, blk), lambda bi, gi, qi: (bi, gi, 0)),
            _resident((blk + V7X_SUBLANES, blk), 3),
        ],
        out_specs=pl.BlockSpec((1, blk, gw), lambda bi, gi, qi: (bi, qi, gi)),
        out_shape=jax.ShapeDtypeStruct((b, s, sb_width), BF16),
        scratch_shapes=[pltpu.VMEM((heads, HEAD_DIM, blk), F32), pltpu.VMEM((heads, V7X_SUBLANES, blk), F32)],
        compiler_params=pltpu.CompilerParams(
            dimension_semantics=("arbitrary", "arbitrary", "arbitrary"),
            vmem_limit_bytes=_vmem_limit(blocks, temps),
        ),
        name="sb_attn",
    )(proj3, proj3, vt, lo)


def _rank_of(vals):
    n = vals.shape[0]
    idx = lax.broadcasted_iota(I32, vals.shape, 0)
    rank = jnp.zeros(vals.shape, F32)
    for other in range(n):
        o = vals[other:other + 1, :]
        rank = rank + jnp.where(o > vals, 1.0, 0.0) + jnp.where(o == vals, jnp.where(idx > other, 1.0, 0.0), 0.0)
    return rank


def _router(logits_t, bias):
    n_exp, tm = logits_t.shape
    per_group = n_exp // N_GROUPS
    scores = _sigmoid(logits_t)
    biased = scores + bias

    grouped = biased.reshape(N_GROUPS, per_group, tm)
    top1 = jnp.max(grouped, axis=1, keepdims=True)
    n_top = jnp.sum(jnp.where(grouped == top1, 1.0, 0.0), axis=1, keepdims=True)
    below = jnp.max(jnp.where(grouped < top1, grouped, -jnp.inf), axis=1, keepdims=True)
    group_score = (top1 + jnp.where(n_top >= 2.0, top1, below)).reshape(N_GROUPS, tm)

    group_ok = _rank_of(group_score) < float(TOPK_GROUPS)
    expert_ok = jnp.broadcast_to(group_ok.reshape(N_GROUPS, 1, tm), (N_GROUPS, per_group, tm)).reshape(n_exp, tm)
    chosen = jnp.where(_rank_of(jnp.where(expert_ok, biased, -jnp.inf)) < float(TOP_K), 1.0, 0.0)
    w = chosen * scores
    return chosen, w / jnp.sum(w, axis=0, keepdims=True) * ROUTED_SCALE


def _mixer_out_kernel(ch_ref, cb_ref, cc_ref, hh_ref, hc_ref, sb_ref, x_ref, cw_ref, gc_ref, gs_ref,
                      wo_ref, gf_ref, wr_ref, rb_ref, tri_ref, low_ref,
                      h1_ref, xt_ref, eid_ref, rank_ref, gate_ref, cnt_ref, *, tiles_per_seq):
    i = pl.program_id(0)
    u = cc_ref[...].astype(F32) * ch_ref[...].astype(F32)
    halo = hc_ref[...].astype(F32) * hh_ref[...].astype(F32)
    halo = jnp.where(i % tiles_per_seq == 0, 0.0, halo)
    prev1 = halo[V7X_SUBLANES - 1:V7X_SUBLANES, :]
    prev2 = halo[V7X_SUBLANES - 2:V7X_SUBLANES - 1, :]
    row = lax.broadcasted_iota(I32, u.shape, 0)
    u1 = jnp.where(row == 0, prev1, pltpu.roll(u, 1, 0))
    u2 = jnp.where(row == 0, prev2, jnp.where(row == 1, prev1, pltpu.roll(u, 2, 0)))
    cw = cw_ref[...]
    conv = cb_ref[...].astype(F32) * (cw[0:1, :] * u2 + cw[1:2, :] * u1 + cw[2:3, :] * u)

    y = jnp.concatenate([_rms(conv, gc_ref[...]), _rms(sb_ref[...].astype(F32), gs_ref[...])], axis=-1)
    h1 = x_ref[...] + _dot(y.astype(BF16), wo_ref[...])
    h1_ref[...] = h1
    xt = _rms(h1, gf_ref[...])
    xt_ref[...] = xt

    logits = jnp.dot(xt, wr_ref[...], preferred_element_type=F32, precision=lax.Precision.HIGHEST)
    chosen, gates = _router(logits.T, rb_ref[...])
    n_exp, tm = chosen.shape

    @pl.when(i == 0)
    def _():
        cnt_ref[...] = jnp.zeros_like(cnt_ref)

    counts = _dot(chosen.astype(BF16), tri_ref[...])
    seen = cnt_ref[...]
    rank = seen + counts[:, :tm]
    cnt_ref[...] = seen + counts[:, tm:]

    slot = _dot(low_ref[...], chosen.astype(BF16))
    expert = lax.broadcasted_iota(I32, chosen.shape, 0).astype(F32)
    eids, ranks, gsel = [], [], []
    for k in range(TOP_K):
        pick = chosen * jnp.where(slot == float(k), 1.0, 0.0)
        eids.append(jnp.sum(pick * expert, axis=0, keepdims=True))
        ranks.append(jnp.sum(pick * rank, axis=0, keepdims=True))
        gsel.append(jnp.sum(pick * gates, axis=0, keepdims=True))
    eid_ref[...] = jnp.concatenate(eids, axis=0).astype(I32)
    rank_ref[...] = jnp.concatenate(ranks, axis=0).astype(I32)
    gate_ref[...] = jnp.concatenate(gsel, axis=0).T


def _mixer_out(proj, sb, h, conv_w, gc, gs, w_out, gf, w_router, router_bias, *, seq, conv_width, tm=256):
    t, d = h.shape
    sb_width = sb.shape[1]
    n_exp = w_router.shape[1]
    halo_blocks = tm // V7X_SUBLANES
    conv_spec = lambda c: pl.BlockSpec((tm, conv_width), lambda i: (i, c))
    halo_spec = lambda c: pl.BlockSpec((V7X_SUBLANES, conv_width), lambda i: (jnp.maximum(i * halo_blocks - 1, 0), c))
    earlier = lax.broadcasted_iota(I32, (tm, tm), 0) < lax.broadcasted_iota(I32, (tm, tm), 1)
    tri = jnp.concatenate([earlier.astype(BF16), jnp.ones((tm, tm), BF16)], axis=1)
    lower = (lax.broadcasted_iota(I32, (n_exp, n_exp), 1) < lax.broadcasted_iota(I32, (n_exp, n_exp), 0)).astype(BF16)
    blocks = (3 * _nbytes((tm, conv_width), BF16) + _nbytes((tm, sb_width), BF16) + 3 * _nbytes((tm, d), F32))
    temps = 10 * _nbytes((tm, d), F32) + _nbytes(w_out.shape, BF16) + _nbytes(w_router.shape, F32)
    row_block = pl.BlockSpec((tm, d), lambda i: (i, 0))
    slot_block = pl.BlockSpec((TOP_K, tm), lambda i: (0, i))
    return pl.pallas_call(
        functools.partial(_mixer_out_kernel, tiles_per_seq=seq // tm),
        grid=(t // tm,),
        in_specs=[
            conv_spec(0), conv_spec(1), conv_spec(2), halo_spec(0), halo_spec(2),
            pl.BlockSpec((tm, sb_width), lambda i: (i, 0)),
            row_block,
            _resident(conv_w.shape, 1), _resident(gc.shape, 1), _resident(gs.shape, 1), _resident(w_out.shape, 1),
            _resident(gf.shape, 1), _resident(w_router.shape, 1), _resident(router_bias.shape, 1),
            _resident(tri.shape, 1), _resident(lower.shape, 1),
        ],
        out_specs=[
            row_block, row_block, slot_block, slot_block,
            pl.BlockSpec((tm, TOP_K), lambda i: (i, 0)),
            pl.BlockSpec((n_exp, tm), lambda i: (0, 0)),
        ],
        out_shape=[
            jax.ShapeDtypeStruct((t, d), F32),
            jax.ShapeDtypeStruct((t, d), F32),
            jax.ShapeDtypeStruct((TOP_K, t), I32),
            jax.ShapeDtypeStruct((TOP_K, t), I32),
            jax.ShapeDtypeStruct((t, TOP_K), F32),
            jax.ShapeDtypeStruct((n_exp, tm), F32),
        ],
        compiler_params=pltpu.CompilerParams(
            dimension_semantics=("arbitrary",),
            vmem_limit_bytes=_vmem_limit(blocks, temps),
        ),
        name="mixer_out",
    )(proj, proj, proj, proj, proj, sb, h, conv_w, gc, gs, w_out, gf, w_router, router_bias, tri, lower)


def _moe_dispatch_kernel(cnt_ref, eid_ref, rank_ref, xt_ref, xs_hbm, pos_ref, texp_ref, nused_ref,
                         off_ref, zero_ref, row_sem, pad_sem, *, n_exp, n_tiles, n_assigned):
    ts = xt_ref.shape[0]

    @pl.when(pl.program_id(0) == 0)
    def _():
        zero_ref[...] = jnp.zeros_like(zero_ref)

        def layout(e, off):
            off_ref[e] = off
            tiles = (cnt_ref[e] + EXPERT_TILE - 1) >> EXPERT_TILE_LOG2
            first = off >> EXPERT_TILE_LOG2
            end = off + (tiles << EXPERT_TILE_LOG2)

            def mark(j, c):
                texp_ref[first + j] = e
                return c

            def zero_row(r, c):
                pltpu.make_async_copy(zero_ref.at[0], xs_hbm.at[r], pad_sem).start()
                return c

            lax.fori_loop(0, tiles, mark, 0)
            lax.fori_loop(off + cnt_ref[e], end, zero_row, 0)
            return end

        total = lax.fori_loop(0, n_exp, layout, 0)
        used_tiles = total >> EXPERT_TILE_LOG2
        nused_ref[0] = used_tiles

        def tail(j, c):
            texp_ref[j] = n_exp - 1
            return c

        def drain_row(r, c):
            pltpu.make_async_copy(zero_ref.at[0], xs_hbm.at[0], pad_sem).wait()
            return c

        lax.fori_loop(used_tiles, n_tiles, tail, 0)
        lax.fori_loop(n_assigned, total, drain_row, 0)

    def token(t, c):
        for k in range(TOP_K):
            dst = off_ref[eid_ref[k, t]] + rank_ref[k, t]
            pos_ref[k, t] = dst
            pltpu.make_async_copy(xt_ref.at[t], xs_hbm.at[dst], row_sem).start()
        return c

    lax.fori_loop(0, ts, token, 0)
    for k in range(TOP_K):
        pltpu.make_async_copy(xt_ref, xs_hbm.at[pl.ds(0, ts), :], row_sem).wait()


def _moe_dispatch(counts, eid, rank, xt, *, ts=256):
    t, d = xt.shape
    n_exp = counts.shape[0]
    n_tiles = t * TOP_K // EXPERT_TILE + n_exp
    slot_block = pl.BlockSpec((TOP_K, ts), lambda i, cnt: (0, i), memory_space=pltpu.SMEM)
    whole_smem = lambda n: pl.BlockSpec((n,), lambda i, cnt: (0,), memory_space=pltpu.SMEM)
    blocks = _nbytes((ts, d), F32)
    temps = _nbytes((V7X_SUBLANES, d), F32)
    return pl.pallas_call(
        functools.partial(_moe_dispatch_kernel, n_exp=n_exp, n_tiles=n_tiles, n_assigned=t * TOP_K),
        grid_spec=pltpu.PrefetchScalarGridSpec(
            num_scalar_prefetch=1,
            grid=(t // ts,),
            in_specs=[slot_block, slot_block, pl.BlockSpec((ts, d), lambda i, cnt: (i, 0))],
            out_specs=[pl.BlockSpec(memory_space=pl.ANY), slot_block, whole_smem(n_tiles), whole_smem(1)],
            scratch_shapes=[
                pltpu.SMEM((n_exp,), I32),
                pltpu.VMEM((V7X_SUBLANES, d), F32),
                pltpu.SemaphoreType.DMA(()),
                pltpu.SemaphoreType.DMA(()),
            ],
        ),
        out_shape=[
            jax.ShapeDtypeStruct((n_tiles * EXPERT_TILE, d), F32),
            jax.ShapeDtypeStruct((TOP_K, t), I32),
            jax.ShapeDtypeStruct((n_tiles,), I32),
            jax.ShapeDtypeStruct((1,), I32),
        ],
        compiler_params=pltpu.CompilerParams(
            dimension_semantics=("arbitrary",),
            vmem_limit_bytes=_vmem_limit(blocks, temps),
        ),
        name="moe_dispatch",
    )(counts, eid, rank, xt)


def _moe_experts_kernel(texp_ref, nused_ref, xs_ref, wg_ref, wu_ref, wd_ref, ys_ref):
    i = pl.program_id(0)

    @pl.when(i < nused_ref[0])
    def _():
        hid = _swiglu_hidden(xs_ref[...].astype(BF16), wg_ref[0], wu_ref[0])
        ys_ref[...] = _dot(hid.astype(BF16), wd_ref[0])

    @pl.when(i >= nused_ref[0])
    def _():
        ys_ref[...] = jnp.zeros_like(ys_ref)


def _moe_experts(texp, nused, xs, wg, wu, wd):
    rows, d = xs.shape
    f = wg.shape[2]
    n_tiles = rows // EXPERT_TILE
    blocks = 2 * _nbytes((EXPERT_TILE, d), F32) + 3 * _nbytes((d, f), BF16)
    temps = 4 * _nbytes((EXPERT_TILE, f), F32) + _nbytes((EXPERT_TILE, d), F32) + _nbytes((EXPERT_TILE, d), BF16)
    expert_block = lambda shape: pl.BlockSpec(shape, lambda i, texp, nused: (texp[i], 0, 0))
    return pl.pallas_call(
        _moe_experts_kernel,
        grid_spec=pltpu.PrefetchScalarGridSpec(
            num_scalar_prefetch=2,
            grid=(n_tiles,),
            in_specs=[
                pl.BlockSpec((EXPERT_TILE, d), lambda i, texp, nused: (jnp.minimum(i, nused[0] - 1), 0)),
                expert_block((1, d, f)), expert_block((1, d, f)), expert_block((1, f, d)),
            ],
            out_specs=pl.BlockSpec((EXPERT_TILE, d), lambda i, texp, nused: (i, 0)),
        ),
        out_shape=jax.ShapeDtypeStruct((rows, d), F32),
        compiler_params=pltpu.CompilerParams(
            dimension_semantics=("arbitrary",),
            vmem_limit_bytes=_vmem_limit(blocks, temps),
        ),
        name="moe_experts",
    )(texp, nused, xs, wg, wu, wd)


def _moe_combine_kernel(pos_ref, gate_ref, h1_ref, xt_ref, p_ref, sg_ref, su_ref, sd_ref, gp_ref, wg_ref,
                        wp_ref, gf_ref, ys_hbm, o_ref, buf_ref, sem, *, final_norm):
    tc = h1_ref.shape[0]

    def token(t, c):
        for k in range(TOP_K):
            pltpu.make_async_copy(ys_hbm.at[pos_ref[k, t]], buf_ref.at[k, t], sem).start()
        return c

    lax.fori_loop(0, tc, token, 0)

    hs = _swiglu_hidden(xt_ref[...].astype(BF16), sg_ref[...], su_ref[...])
    h2 = h1_ref[...] + _dot(hs.astype(BF16), sd_ref[...])
    emb = _dot(p_ref[...].astype(BF16), wp_ref[...])

    for k in range(TOP_K):
        pltpu.make_async_copy(ys_hbm.at[pl.ds(0, tc), :], buf_ref.at[k], sem).wait()
    gate = gate_ref[...]
    for k in range(TOP_K):
        h2 = h2 + gate[:, k:k + 1] * buf_ref[k]

    gate_ple = _sigmoid(_dot(_rms(h2, gp_ref[...]).astype(BF16), wg_ref[...]))
    h3 = h2 + gate_ple * emb
    o_ref[...] = _rms(h3, gf_ref[...]) if final_norm else h3


def _moe_combine(pos, gate, h1, xt, p, sg, su, sd, gp, wg, wp, gf, ys, *, final_norm, tc=128):
    t, d = h1.shape
    pd = p.shape[1]
    row_block = pl.BlockSpec((tc, d), lambda i: (i, 0))
    blocks = 3 * _nbytes((tc, d), F32) + _nbytes((tc, pd), F32) + _nbytes((tc, V7X_LANES), F32)
    temps = (_nbytes((TOP_K, tc, d), F32) + 6 * _nbytes((tc, d), F32) + _nbytes(wg.shape, BF16)
             + _nbytes(wp.shape, BF16) + 3 * _nbytes(sg.shape, BF16))
    return pl.pallas_call(
        functools.partial(_moe_combine_kernel, final_norm=final_norm),
        grid=(t // tc,),
        in_specs=[
            pl.BlockSpec((TOP_K, tc), lambda i: (0, i), memory_space=pltpu.SMEM),
            pl.BlockSpec((tc, TOP_K), lambda i: (i, 0)),
            row_block, row_block,
            pl.BlockSpec((tc, pd), lambda i: (i, 0)),
            _resident(sg.shape, 1), _resident(su.shape, 1), _resident(sd.shape, 1), _resident(gp.shape, 1),
            _resident(wg.shape, 1), _resident(wp.shape, 1), _resident(gf.shape, 1),
            pl.BlockSpec(memory_space=pl.ANY),
        ],
        out_specs=row_block,
        out_shape=jax.ShapeDtypeStruct((t, d), F32),
        scratch_shapes=[pltpu.VMEM((TOP_K, tc, d), F32), pltpu.SemaphoreType.DMA(())],
        compiler_params=pltpu.CompilerParams(
            dimension_semantics=("arbitrary",),
            vmem_limit_bytes=_vmem_limit(blocks, temps),
        ),
        name="moe_combine",
    )(pos, gate, h1, xt, p, sg, su, sd, gp, wg, wp, gf, ys)


def kernel(x, p, norm_mix_g, w_in, conv_w, gnorm_conv_g, gnorm_sb_g, w_out, norm_ffn_g, w_router, router_bias,
           w_exp_gate, w_exp_up, w_exp_down, w_sh_gate, w_sh_up, w_sh_down, norm_ple_g, w_ple_gate, w_ple_proj,
           norm_final_g):
    bsz, seq, d = x.shape
    depth = p.shape[0]
    t = bsz * seq
    conv_width = conv_w.shape[-1]
    sb_width = gnorm_sb_g.shape[-1]
    n_qk = 3 * conv_width + 2 * sb_width
    assert w_in.shape[-1] == n_qk + sb_width
    assert w_router.shape[-1] % N_GROUPS == 0

    q_lo = 3 * conv_width
    col = jnp.arange(n_qk)
    q_scale = HEAD_DIM ** -0.5 * LOG2_E
    col_scale = jnp.where((col >= q_lo) & (col < q_lo + sb_width), q_scale, 1.0).astype(F32)[None, :]
    row = lambda v: v.astype(F32)[None, :]

    h = x.reshape(t, d)
    for i in range(depth):
        w_in_bf = w_in[i].astype(BF16)
        proj, vt = _norm_inproj(h, row(norm_mix_g[i]), w_in_bf[:, :n_qk], col_scale, w_in_bf[:, n_qk:].T,
                                key_block=ATTN_BLOCK)
        sb = _sb_attention(proj.reshape(bsz, seq, n_qk), vt, conv_width=conv_width, sb_width=sb_width,
                           blk=ATTN_BLOCK)
        h1, xt, eid, rank, gate, cnt = _mixer_out(
            proj, sb.reshape(t, sb_width), h, conv_w[i], row(gnorm_conv_g[i]), row(gnorm_sb_g[i]),
            w_out[i].astype(BF16), row(norm_ffn_g[i]), w_router[i], router_bias[i].astype(F32)[:, None],
            seq=seq, conv_width=conv_width)
        xs, pos, texp, nused = _moe_dispatch(cnt[:, 0].astype(I32), eid, rank, xt)
        ys = _moe_experts(texp, nused, xs, w_exp_gate[i].astype(BF16), w_exp_up[i].astype(BF16),
                          w_exp_down[i].astype(BF16))
        h = _moe_combine(pos, gate, h1, xt, p[i].reshape(t, -1), w_sh_gate[i].astype(BF16),
                         w_sh_up[i].astype(BF16), w_sh_down[i].astype(BF16), row(norm_ple_g[i]),
                         w_ple_gate[i].astype(BF16), w_ple_proj[i].astype(BF16), row(norm_final_g), ys,
                         final_norm=(i == depth - 1))
    return h.reshape(bsz, seq, d)
```

```python
import functools

import jax
import jax.numpy as jnp
from jax import lax
from jax.experimental import pallas as pl
from jax.experimental.pallas import tpu as pltpu

NORM_EPS = 1e-6
HEAD_DIM = 128
CONV_KERNEL = 3
N_GROUPS = 8
TOPK_GROUPS = 4
TOP_K = 8
ROUTED_SCALE = 2.5
LOG2_E = 1.4426950408889634

V7X_VMEM_BYTES = 64 * 1024 * 1024
V7X_SUBLANES = 8
V7X_LANES = 128

F32 = jnp.float32
BF16 = jnp.bfloat16
I32 = jnp.int32

ATTN_BLOCK = 256
EXPERT_TILE = 512
EXPERT_TILE_LOG2 = 9
assert 1 << EXPERT_TILE_LOG2 == EXPERT_TILE


def _vmem_limit(pipelined_bytes, resident_bytes):
    want = 2 * pipelined_bytes + resident_bytes
    return int(min(want, V7X_VMEM_BYTES - 4 * 1024 * 1024))


def _nbytes(shape, dtype):
    n = 1
    for s in shape:
        n *= s
    return n * jnp.dtype(dtype).itemsize


def _resident(shape, n_grid_axes, n_prefetch=0):
    zeros = (0,) * len(shape)
    return pl.BlockSpec(shape, lambda *_: zeros, pipeline_mode=pl.Buffered(1))


def _rms(y, g):
    return y * lax.rsqrt(jnp.mean(y * y, axis=-1, keepdims=True) + NORM_EPS) * g


def _sigmoid(a):
    return 1.0 / (1.0 + jnp.exp(-a))


def _softplus2(z):
    return jnp.maximum(z, 0.0) + jnp.log2(1.0 + jnp.exp2(-jnp.abs(z)))


def _dot(a, b):
    return jnp.dot(a, b, preferred_element_type=F32)


def _dot_nt(a, b):
    return lax.dot_general(a, b, (((1,), (1,)), ((), ())), preferred_element_type=F32)


def _swiglu_hidden(x, wg, wu):
    a = _dot(x, wg)
    return a * _sigmoid(a) * _dot(x, wu)


def _norm_inproj_kernel(x_ref, g_ref, w_ref, cs_ref, wvt_ref, o_ref, vt_ref, hn_ref, *, n_col_tiles):
    j = pl.program_id(1)

    @pl.when(j == 0)
    def _():
        hn_ref[...] = _rms(x_ref[...], g_ref[...]).astype(hn_ref.dtype)

    @pl.when(j < n_col_tiles)
    def _():
        o_ref[...] = (_dot(hn_ref[...], w_ref[...]) * cs_ref[...]).astype(o_ref.dtype)

    @pl.when(j == n_col_tiles)
    def _():
        vt = _dot_nt(wvt_ref[...], hn_ref[...]).astype(vt_ref.dtype)
        kb = vt_ref.shape[2]
        for c in range(vt_ref.shape[0]):
            vt_ref[c] = vt[:, c * kb:(c + 1) * kb]


def _norm_inproj(h, g, w, col_scale, wvt, *, key_block, tm=1024, tn=1024):
    t, d = h.shape
    n = w.shape[1]
    vw = wvt.shape[0]
    n_col_tiles = n // tn
    last = n_col_tiles - 1
    blocks = _nbytes((tm, d), F32) + _nbytes((d, tn), BF16) + _nbytes((tm, tn), BF16) + _nbytes((vw, tm), BF16)
    temps = _nbytes((tm, d), BF16) + 2 * _nbytes((tm, d), F32) + _nbytes((tm, tn), F32) + _nbytes((vw, d), BF16)
    return pl.pallas_call(
        functools.partial(_norm_inproj_kernel, n_col_tiles=n_col_tiles),
        grid=(t // tm, n_col_tiles + 1),
        in_specs=[
            pl.BlockSpec((tm, d), lambda i, j: (i, 0)),
            _resident((1, d), 2),
            pl.BlockSpec((d, tn), lambda i, j: (0, jnp.minimum(j, last))),
            pl.BlockSpec((1, tn), lambda i, j: (0, jnp.minimum(j, last))),
            _resident((vw, d), 2),
        ],
        out_specs=[
            pl.BlockSpec((tm, tn), lambda i, j: (i, jnp.minimum(j, last))),
            pl.BlockSpec((tm // key_block, vw, key_block), lambda i, j: (i, 0, 0)),
        ],
        out_shape=[jax.ShapeDtypeStruct((t, n), BF16), jax.ShapeDtypeStruct((t // key_block, vw, key_block), BF16)],
        scratch_shapes=[pltpu.VMEM((tm, d), BF16)],
        compiler_params=pltpu.CompilerParams(
            dimension_semantics=("arbitrary", "arbitrary"),
            vmem_limit_bytes=_vmem_limit(blocks, temps),
        ),
        name="norm_inproj",
    )(h, g, w, col_scale, wvt)


def _sb_attn_kernel(q_ref, k_ref, vt_ref, lo_ref, o_ref, acc_ref, run_ref, *, blk, heads):
    qi = pl.program_id(2)
    lo = lo_ref[...]
    acc_ref[...] = jnp.zeros_like(acc_ref)
    run_ref[...] = jnp.zeros_like(run_ref)

    def block(j, masked):
        start = pl.multiple_of(j * blk, blk)
        cols = [slice(h * HEAD_DIM, (h + 1) * HEAD_DIM) for h in range(heads)]
        if masked:
            causal = lax.broadcasted_iota(I32, (blk, blk), 0) < lax.broadcasted_iota(I32, (blk, blk), 1)
        zs = [_dot_nt(k_ref[0, pl.ds(start, blk), c], q_ref[0, :, c]) for c in cols]
        sps = [_softplus2(z) for z in zs]
        keeps = [jnp.where(causal, sp, 0.0) if masked else sp for sp in sps]
        css = [_dot(lo, keep.astype(BF16)) for keep in keeps]
        for h in range(heads):
            run = run_ref[h]
            run_all = jnp.concatenate([run] * (blk // V7X_SUBLANES), axis=0)
            a = jnp.exp2(zs[h] - sps[h] + css[h][:blk] + run_all)
            if masked:
                a = jnp.where(causal, a, 0.0)
            acc_ref[h] += _dot(vt_ref[j, cols[h], :], a.astype(BF16))
            run_ref[h] = run + css[h][blk:]

    block(qi, True)

    def body(n, carry):
        block(qi - 1 - n, False)
        return carry

    lax.fori_loop(0, qi, body, 0)
    for h in range(heads):
        o_ref[0, :, h * HEAD_DIM:(h + 1) * HEAD_DIM] = acc_ref[h].T.astype(o_ref.dtype)


def _sb_attention(proj3, vt, *, conv_width, sb_width, blk, heads=8):
    b, s, _ = proj3.shape
    gw = heads * HEAD_DIM
    groups = sb_width // gw
    q_off = 3 * conv_width // gw
    k_off = q_off + groups
    later = lax.broadcasted_iota(I32, (blk, blk), 1) > lax.broadcasted_iota(I32, (blk, blk), 0)
    lo = -jnp.concatenate([later.astype(BF16), jnp.ones((V7X_SUBLANES, blk), BF16)], axis=0)
    blocks = 2 * _nbytes((blk, gw), BF16) + 2 * _nbytes((s, gw), BF16)
    temps = heads * 8 * _nbytes((blk, blk), F32)
    return pl.pallas_call(
        functools.partial(_sb_attn_kernel, blk=blk, heads=heads),
        grid=(b, groups, s // blk),
        in_specs=[
            pl.BlockSpec((1, blk, gw), lambda bi, gi, qi: (bi, qi, q_off + gi)),
            pl.BlockSpec((1, s, gw), lambda bi, gi, qi: (bi, 0, k_off + gi)),
            pl.BlockSpec((s // blk, gw, blk), lambda bi, gi, qi: (bi, gi, 0)),
            _resident((blk + V7X_SUBLANES, blk), 3),
        ],
        out_specs=pl.BlockSpec((1, blk, gw), lambda bi, gi, qi: (bi, qi, gi)),
        out_shape=jax.ShapeDtypeStruct((b, s, sb_width), BF16),
        scratch_shapes=[pltpu.VMEM((heads, HEAD_DIM, blk), F32), pltpu.VMEM((heads, V7X_SUBLANES, blk), F32)],
        compiler_params=pltpu.CompilerParams(
            dimension_semantics=("arbitrary", "arbitrary", "arbitrary"),
            vmem_limit_bytes=_vmem_limit(blocks, temps),
        ),
        name="sb_attn",
    )(proj3, proj3, vt, lo)


def _rank_of(vals):
    n = vals.shape[0]
    idx = lax.broadcasted_iota(I32, vals.shape, 0)
    rank = jnp.zeros(vals.shape, F32)
    for other in range(n):
        o = vals[other:other + 1, :]
        rank = rank + jnp.where(o > vals, 1.0, 0.0) + jnp.where(o == vals, jnp.where(idx > other, 1.0, 0.0), 0.0)
    return rank


def _router(logits_t, bias):
    n_exp, tm = logits_t.shape
    per_group = n_exp // N_GROUPS
    scores = _sigmoid(logits_t)
    biased = scores + bias

    grouped = biased.reshape(N_GROUPS, per_group, tm)
    top1 = jnp.max(grouped, axis=1, keepdims=True)
    n_top = jnp.sum(jnp.where(grouped == top1, 1.0, 0.0), axis=1, keepdims=True)
    below = jnp.max(jnp.where(grouped < top1, grouped, -jnp.inf), axis=1, keepdims=True)
    group_score = (top1 + jnp.where(n_top >= 2.0, top1, below)).reshape(N_GROUPS, tm)

    group_ok = _rank_of(group_score) < float(TOPK_GROUPS)
    expert_ok = jnp.broadcast_to(group_ok.reshape(N_GROUPS, 1, tm), (N_GROUPS, per_group, tm)).reshape(n_exp, tm)
    chosen = jnp.where(_rank_of(jnp.where(expert_ok, biased, -jnp.inf)) < float(TOP_K), 1.0, 0.0)
    w = chosen * scores
    return chosen, w / jnp.sum(w, axis=0, keepdims=True) * ROUTED_SCALE


def _mixer_out_kernel(ch_ref, cb_ref, cc_ref, hh_ref, hc_ref, sb_ref, x_ref, cw_ref, gc_ref, gs_ref,
                      wo_ref, gf_ref, wr_ref, rb_ref, tri_ref, low_ref,
                      h1_ref, xt_ref, eid_ref, rank_ref, gate_ref, cnt_ref, *, tiles_per_seq):
    i = pl.program_id(0)
    u = cc_ref[...].astype(F32) * ch_ref[...].astype(F32)
    halo = hc_ref[...].astype(F32) * hh_ref[...].astype(F32)
    halo = jnp.where(i % tiles_per_seq == 0, 0.0, halo)
    prev1 = halo[V7X_SUBLANES - 1:V7X_SUBLANES, :]
    prev2 = halo[V7X_SUBLANES - 2:V7X_SUBLANES - 1, :]
    row = lax.broadcasted_iota(I32, u.shape, 0)
    u1 = jnp.where(row == 0, prev1, pltpu.roll(u, 1, 0))
    u2 = jnp.where(row == 0, prev2, jnp.where(row == 1, prev1, pltpu.roll(u, 2, 0)))
    cw = cw_ref[...]
    conv = cb_ref[...].astype(F32) * (cw[0:1, :] * u2 + cw[1:2, :] * u1 + cw[2:3, :] * u)

    y = jnp.concatenate([_rms(conv, gc_ref[...]), _rms(sb_ref[...].astype(F32), gs_ref[...])], axis=-1)
    h1 = x_ref[...] + _dot(y.astype(BF16), wo_ref[...])
    h1_ref[...] = h1
    xt = _rms(h1, gf_ref[...])
    xt_ref[...] = xt

    n_exp = wr_ref.shape[1] // 2
    xt_hi = xt.astype(BF16)
    xt_lo = (xt - xt_hi.astype(F32)).astype(BF16)
    by_hi = _dot(xt_hi, wr_ref[...])
    logits = by_hi[:, :n_exp] + by_hi[:, n_exp:] + _dot(xt_lo, wr_ref[:, :n_exp])
    chosen, gates = _router(logits.T, rb_ref[...])
    tm = chosen.shape[1]

    @pl.when(i == 0)
    def _():
        cnt_ref[...] = jnp.zeros_like(cnt_ref)

    counts = _dot(chosen.astype(BF16), tri_ref[...])
    seen = cnt_ref[...]
    rank = seen + counts[:, :tm]
    cnt_ref[...] = seen + counts[:, tm:]

    slot = _dot(low_ref[...], chosen.astype(BF16))
    expert = lax.broadcasted_iota(I32, chosen.shape, 0).astype(F32)
    eids, ranks, gsel = [], [], []
    for k in range(TOP_K):
        pick = chosen * jnp.where(slot == float(k), 1.0, 0.0)
        eids.append(jnp.sum(pick * expert, axis=0, keepdims=True))
        ranks.append(jnp.sum(pick * rank, axis=0, keepdims=True))
        gsel.append(jnp.sum(pick * gates, axis=0, keepdims=True))
    eid_ref[...] = jnp.concatenate(eids, axis=0).astype(I32)
    rank_ref[...] = jnp.concatenate(ranks, axis=0).astype(I32)
    gate_ref[...] = jnp.concatenate(gsel, axis=0).T


def _mixer_out(proj, sb, h, conv_w, gc, gs, w_out, gf, w_router, router_bias, *, seq, conv_width, tm=256):
    t, d = h.shape
    sb_width = sb.shape[1]
    n_exp = w_router.shape[1]
    router_hi = w_router.astype(BF16)
    router_lo = (w_router - router_hi.astype(F32)).astype(BF16)
    w_router = jnp.concatenate([router_hi, router_lo], axis=1)
    halo_blocks = tm // V7X_SUBLANES
    conv_spec = lambda c: pl.BlockSpec((tm, conv_width), lambda i: (i, c))
    halo_spec = lambda c: pl.BlockSpec((V7X_SUBLANES, conv_width), lambda i: (jnp.maximum(i * halo_blocks - 1, 0), c))
    earlier = lax.broadcasted_iota(I32, (tm, tm), 0) < lax.broadcasted_iota(I32, (tm, tm), 1)
    tri = jnp.concatenate([earlier.astype(BF16), jnp.ones((tm, tm), BF16)], axis=1)
    lower = (lax.broadcasted_iota(I32, (n_exp, n_exp), 1) < lax.broadcasted_iota(I32, (n_exp, n_exp), 0)).astype(BF16)
    blocks = (3 * _nbytes((tm, conv_width), BF16) + _nbytes((tm, sb_width), BF16) + 3 * _nbytes((tm, d), F32))
    temps = 10 * _nbytes((tm, d), F32) + _nbytes(w_out.shape, BF16) + _nbytes(w_router.shape, BF16)
    row_block = pl.BlockSpec((tm, d), lambda i: (i, 0))
    slot_block = pl.BlockSpec((TOP_K, tm), lambda i: (0, i))
    return pl.pallas_call(
        functools.partial(_mixer_out_kernel, tiles_per_seq=seq // tm),
        grid=(t // tm,),
        in_specs=[
            conv_spec(0), conv_spec(1), conv_spec(2), halo_spec(0), halo_spec(2),
            pl.BlockSpec((tm, sb_width), lambda i: (i, 0)),
            row_block,
            _resident(conv_w.shape, 1), _resident(gc.shape, 1), _resident(gs.shape, 1), _resident(w_out.shape, 1),
            _resident(gf.shape, 1), _resident(w_router.shape, 1), _resident(router_bias.shape, 1),
            _resident(tri.shape, 1), _resident(lower.shape, 1),
        ],
        out_specs=[
            row_block, row_block, slot_block, slot_block,
            pl.BlockSpec((tm, TOP_K), lambda i: (i, 0)),
            pl.BlockSpec((n_exp, tm), lambda i: (0, 0)),
        ],
        out_shape=[
            jax.ShapeDtypeStruct((t, d), F32),
            jax.ShapeDtypeStruct((t, d), F32),
            jax.ShapeDtypeStruct((TOP_K, t), I32),
            jax.ShapeDtypeStruct((TOP_K, t), I32),
            jax.ShapeDtypeStruct((t, TOP_K), F32),
            jax.ShapeDtypeStruct((n_exp, tm), F32),
        ],
        compiler_params=pltpu.CompilerParams(
            dimension_semantics=("arbitrary",),
            vmem_limit_bytes=_vmem_limit(blocks, temps),
        ),
        name="mixer_out",
    )(proj, proj, proj, proj, proj, sb, h, conv_w, gc, gs, w_out, gf, w_router, router_bias, tri, lower)


def _moe_dispatch_kernel(cnt_ref, eid_ref, rank_ref, xt_ref, xs_hbm, pos_ref, texp_ref, nused_ref,
                         off_ref, zero_ref, row_sem, pad_sem, *, n_exp, n_tiles, n_assigned):
    ts = xt_ref.shape[0]

    @pl.when(pl.program_id(0) == 0)
    def _():
        zero_ref[...] = jnp.zeros_like(zero_ref)

        def layout(e, off):
            off_ref[e] = off
            tiles = (cnt_ref[e] + EXPERT_TILE - 1) >> EXPERT_TILE_LOG2
            first = off >> EXPERT_TILE_LOG2
            end = off + (tiles << EXPERT_TILE_LOG2)

            def mark(j, c):
                texp_ref[first + j] = e
                return c

            def zero_row(r, c):
                pltpu.make_async_copy(zero_ref.at[0], xs_hbm.at[r], pad_sem).start()
                return c

            lax.fori_loop(0, tiles, mark, 0)
            lax.fori_loop(off + cnt_ref[e], end, zero_row, 0)
            return end

        total = lax.fori_loop(0, n_exp, layout, 0)
        used_tiles = total >> EXPERT_TILE_LOG2
        nused_ref[0] = used_tiles

        def tail(j, c):
            texp_ref[j] = n_exp - 1
            return c

        def drain_row(r, c):
            pltpu.make_async_copy(zero_ref.at[0], xs_hbm.at[0], pad_sem).wait()
            return c

        lax.fori_loop(used_tiles, n_tiles, tail, 0)
        lax.fori_loop(n_assigned, total, drain_row, 0)

    def token_group(g, c):
        base = pl.multiple_of(g * V7X_SUBLANES, V7X_SUBLANES)
        for u in range(V7X_SUBLANES):
            t = base + u
            for k in range(TOP_K):
                dst = off_ref[eid_ref[k, t]] + rank_ref[k, t]
                pos_ref[k, t] = dst
                pltpu.make_async_copy(xt_ref.at[t], xs_hbm.at[dst], row_sem).start()
        return c

    lax.fori_loop(0, ts // V7X_SUBLANES, token_group, 0)
    for k in range(TOP_K):
        pltpu.make_async_copy(xt_ref, xs_hbm.at[pl.ds(0, ts), :], row_sem).wait()


def _moe_dispatch(counts, eid, rank, xt, *, ts=512):
    t, d = xt.shape
    n_exp = counts.shape[0]
    n_tiles = t * TOP_K // EXPERT_TILE + n_exp
    slot_block = pl.BlockSpec((TOP_K, ts), lambda i, cnt: (0, i), memory_space=pltpu.SMEM)
    whole_smem = lambda n: pl.BlockSpec((n,), lambda i, cnt: (0,), memory_space=pltpu.SMEM)
    blocks = _nbytes((ts, d), F32)
    temps = _nbytes((V7X_SUBLANES, d), F32)
    return pl.pallas_call(
        functools.partial(_moe_dispatch_kernel, n_exp=n_exp, n_tiles=n_tiles, n_assigned=t * TOP_K),
        grid_spec=pltpu.PrefetchScalarGridSpec(
            num_scalar_prefetch=1,
            grid=(t // ts,),
            in_specs=[slot_block, slot_block, pl.BlockSpec((ts, d), lambda i, cnt: (i, 0))],
            out_specs=[pl.BlockSpec(memory_space=pl.ANY), slot_block, whole_smem(n_tiles), whole_smem(1)],
            scratch_shapes=[
                pltpu.SMEM((n_exp,), I32),
                pltpu.VMEM((V7X_SUBLANES, d), F32),
                pltpu.SemaphoreType.DMA(()),
                pltpu.SemaphoreType.DMA(()),
            ],
        ),
        out_shape=[
            jax.ShapeDtypeStruct((n_tiles * EXPERT_TILE, d), F32),
            jax.ShapeDtypeStruct((TOP_K, t), I32),
            jax.ShapeDtypeStruct((n_tiles,), I32),
            jax.ShapeDtypeStruct((1,), I32),
        ],
        compiler_params=pltpu.CompilerParams(
            dimension_semantics=("arbitrary",),
            vmem_limit_bytes=_vmem_limit(blocks, temps),
        ),
        name="moe_dispatch",
    )(counts, eid, rank, xt)


def _moe_experts_kernel(texp_ref, nused_ref, xs_ref, wg_ref, wu_ref, wd_ref, ys_ref):
    i = pl.program_id(0)

    @pl.when(i < nused_ref[0])
    def _():
        hid = _swiglu_hidden(xs_ref[...].astype(BF16), wg_ref[0], wu_ref[0])
        ys_ref[...] = _dot(hid.astype(BF16), wd_ref[0])

    @pl.when(i >= nused_ref[0])
    def _():
        ys_ref[...] = jnp.zeros_like(ys_ref)


def _moe_experts(texp, nused, xs, wg, wu, wd):
    rows, d = xs.shape
    f = wg.shape[2]
    n_tiles = rows // EXPERT_TILE
    blocks = 2 * _nbytes((EXPERT_TILE, d), F32) + 3 * _nbytes((d, f), BF16)
    temps = 4 * _nbytes((EXPERT_TILE, f), F32) + _nbytes((EXPERT_TILE, d), F32) + _nbytes((EXPERT_TILE, d), BF16)
    expert_block = lambda shape: pl.BlockSpec(shape, lambda i, texp, nused: (texp[i], 0, 0))
    return pl.pallas_call(
        _moe_experts_kernel,
        grid_spec=pltpu.PrefetchScalarGridSpec(
            num_scalar_prefetch=2,
            grid=(n_tiles,),
            in_specs=[
                pl.BlockSpec((EXPERT_TILE, d), lambda i, texp, nused: (jnp.minimum(i, nused[0] - 1), 0)),
                expert_block((1, d, f)), expert_block((1, d, f)), expert_block((1, f, d)),
            ],
            out_specs=pl.BlockSpec((EXPERT_TILE, d), lambda i, texp, nused: (i, 0)),
        ),
        out_shape=jax.ShapeDtypeStruct((rows, d), F32),
        compiler_params=pltpu.CompilerParams(
            dimension_semantics=("arbitrary",),
            vmem_limit_bytes=_vmem_limit(blocks, temps),
        ),
        name="moe_experts",
    )(texp, nused, xs, wg, wu, wd)


def _moe_combine_kernel(pos_ref, pos_next_ref, gate_ref, h1_ref, xt_ref, p_ref, sg_ref, su_ref, sd_ref, gp_ref,
                        wg_ref, wp_ref, gf_ref, ys_hbm, o_ref, buf_even, buf_odd, sem, *, final_norm):
    tc = h1_ref.shape[0]
    i = pl.program_id(0)

    def wait_rows(buf, half):
        for k in range(TOP_K):
            pltpu.make_async_copy(ys_hbm.at[pl.ds(0, tc), :], buf.at[k], sem.at[half]).wait()

    @pl.when(i == 0)
    def _():
        def token_group(g, c):
            base = pl.multiple_of(g * V7X_SUBLANES, V7X_SUBLANES)
            for u in range(V7X_SUBLANES):
                for k in range(TOP_K):
                    pltpu.make_async_copy(ys_hbm.at[pos_ref[k, base + u]], buf_even.at[k, base + u], sem.at[0]).start()
            return c

        lax.fori_loop(0, tc // V7X_SUBLANES, token_group, 0)

    def step(cur_buf, cur_half, nxt_buf, nxt_half):
        wait_rows(cur_buf, cur_half)
        for t in range(tc):
            for k in range(TOP_K):
                pltpu.make_async_copy(ys_hbm.at[pos_next_ref[k, t]], nxt_buf.at[k, t], sem.at[nxt_half]).start()

        hs = _swiglu_hidden(xt_ref[...].astype(BF16), sg_ref[...], su_ref[...])
        h2 = h1_ref[...] + _dot(hs.astype(BF16), sd_ref[...])
        emb = _dot(p_ref[...].astype(BF16), wp_ref[...])
        gate = gate_ref[...]
        for k in range(TOP_K):
            h2 = h2 + gate[:, k:k + 1] * cur_buf[k]

        gate_ple = _sigmoid(_dot(_rms(h2, gp_ref[...]).astype(BF16), wg_ref[...]))
        h3 = h2 + gate_ple * emb
        o_ref[...] = _rms(h3, gf_ref[...]) if final_norm else h3

        @pl.when(i + 1 == pl.num_programs(0))
        def _():
            wait_rows(nxt_buf, nxt_half)

    @pl.when(i % 2 == 0)
    def _():
        step(buf_even, 0, buf_odd, 1)

    @pl.when(i % 2 == 1)
    def _():
        step(buf_odd, 1, buf_even, 0)


def _moe_combine(pos, gate, h1, xt, p, sg, su, sd, gp, wg, wp, gf, ys, *, final_norm, tc=128):
    t, d = h1.shape
    pd = p.shape[1]
    row_block = pl.BlockSpec((tc, d), lambda i: (i, 0))
    blocks = 3 * _nbytes((tc, d), F32) + _nbytes((tc, pd), F32) + _nbytes((tc, V7X_LANES), F32)
    temps = (2 * _nbytes((TOP_K, tc, d), F32) + 6 * _nbytes((tc, d), F32) + _nbytes(wg.shape, BF16)
             + _nbytes(wp.shape, BF16) + 3 * _nbytes(sg.shape, BF16))
    last = t // tc - 1
    return pl.pallas_call(
        functools.partial(_moe_combine_kernel, final_norm=final_norm),
        grid=(t // tc,),
        in_specs=[
            pl.BlockSpec((TOP_K, tc), lambda i: (0, i), memory_space=pltpu.SMEM),
            pl.BlockSpec((TOP_K, tc), lambda i: (0, jnp.minimum(i + 1, last)), memory_space=pltpu.SMEM),
            pl.BlockSpec((tc, TOP_K), lambda i: (i, 0)),
            row_block, row_block,
            pl.BlockSpec((tc, pd), lambda i: (i, 0)),
            _resident(sg.shape, 1), _resident(su.shape, 1), _resident(sd.shape, 1), _resident(gp.shape, 1),
            _resident(wg.shape, 1), _resident(wp.shape, 1), _resident(gf.shape, 1),
            pl.BlockSpec(memory_space=pl.ANY),
        ],
        out_specs=row_block,
        out_shape=jax.ShapeDtypeStruct((t, d), F32),
        scratch_shapes=[pltpu.VMEM((TOP_K, tc, d), F32), pltpu.VMEM((TOP_K, tc, d), F32),
                        pltpu.SemaphoreType.DMA((2,))],
        compiler_params=pltpu.CompilerParams(
            dimension_semantics=("arbitrary",),
            vmem_limit_bytes=_vmem_limit(blocks, temps),
        ),
        name="moe_combine",
    )(pos, pos, gate, h1, xt, p, sg, su, sd, gp, wg, wp, gf, ys)


def kernel(x, p, norm_mix_g, w_in, conv_w, gnorm_conv_g, gnorm_sb_g, w_out, norm_ffn_g, w_router, router_bias,
           w_exp_gate, w_exp_up, w_exp_down, w_sh_gate, w_sh_up, w_sh_down, norm_ple_g, w_ple_gate, w_ple_proj,
           norm_final_g):
    bsz, seq, d = x.shape
    depth = p.shape[0]
    t = bsz * seq
    conv_width = conv_w.shape[-1]
    sb_width = gnorm_sb_g.shape[-1]
    n_qk = 3 * conv_width + 2 * sb_width
    assert w_in.shape[-1] == n_qk + sb_width
    assert w_router.shape[-1] % N_GROUPS == 0

    q_lo = 3 * conv_width
    col = jnp.arange(n_qk)
    q_scale = HEAD_DIM ** -0.5 * LOG2_E
    col_scale = jnp.where((col >= q_lo) & (col < q_lo + sb_width), q_scale, 1.0).astype(F32)[None, :]
    row = lambda v: v.astype(F32)[None, :]

    h = x.reshape(t, d)
    for i in range(depth):
        w_in_bf = w_in[i].astype(BF16)
        proj, vt = _norm_inproj(h, row(norm_mix_g[i]), w_in_bf[:, :n_qk], col_scale, w_in_bf[:, n_qk:].T,
                                key_block=ATTN_BLOCK)
        sb = _sb_attention(proj.reshape(bsz, seq, n_qk), vt, conv_width=conv_width, sb_width=sb_width,
                           blk=ATTN_BLOCK)
        h1, xt, eid, rank, gate, cnt = _mixer_out(
            proj, sb.reshape(t, sb_width), h, conv_w[i], row(gnorm_conv_g[i]), row(gnorm_sb_g[i]),
            w_out[i].astype(BF16), row(norm_ffn_g[i]), w_router[i], router_bias[i].astype(F32)[:, None],
            seq=seq, conv_width=conv_width)
        xs, pos, texp, nused = _moe_dispatch(cnt[:, 0].astype(I32), eid, rank, xt)
        ys = _moe_experts(texp, nused, xs, w_exp_gate[i].astype(BF16), w_exp_up[i].astype(BF16),
                          w_exp_down[i].astype(BF16))
        h = _moe_combine(pos, gate, h1, xt, p[i].reshape(t, -1), w_sh_gate[i].astype(BF16),
                         w_sh_up[i].astype(BF16), w_sh_down[i].astype(BF16), row(norm_ple_g[i]),
                         w_ple_gate[i].astype(BF16), w_ple_proj[i].astype(BF16), row(norm_final_g), ys,
                         final_norm=(i == depth - 1))
    return h.reshape(bsz, seq, d)
```

```python
import functools

import jax
import jax.numpy as jnp
from jax import lax
from jax.experimental import pallas as pl
from jax.experimental.pallas import tpu as pltpu

NORM_EPS = 1e-6
HEAD_DIM = 128
CONV_KERNEL = 3
N_GROUPS = 8
TOPK_GROUPS = 4
TOP_K = 8
ROUTED_SCALE = 2.5
LOG2_E = 1.4426950408889634

V7X_VMEM_BYTES = 64 * 1024 * 1024
V7X_SUBLANES = 8
V7X_LANES = 128
DMA_THREADS = 2

F32 = jnp.float32
BF16 = jnp.bfloat16
I32 = jnp.int32
U32 = jnp.uint32

ATTN_BLOCK = 256
EXPERT_TILE = 512
EXPERT_TILE_LOG2 = 9
assert 1 << EXPERT_TILE_LOG2 == EXPERT_TILE


def _vmem_limit(pipelined_bytes, resident_bytes):
    want = 2 * pipelined_bytes + resident_bytes
    return int(min(want, V7X_VMEM_BYTES - 4 * 1024 * 1024))


def _nbytes(shape, dtype):
    n = 1
    for s in shape:
        n *= s
    return n * jnp.dtype(dtype).itemsize


def _resident(shape, n_grid_axes, n_prefetch=0):
    zeros = (0,) * len(shape)
    return pl.BlockSpec(shape, lambda *_: zeros, pipeline_mode=pl.Buffered(1))


def _rms(y, g):
    return y * lax.rsqrt(jnp.mean(y * y, axis=-1, keepdims=True) + NORM_EPS) * g


def _sigmoid(a):
    return 1.0 / (1.0 + jnp.exp(-a))


def _softplus2(z):
    return jnp.maximum(z, 0.0) + jnp.log2(1.0 + jnp.exp2(-jnp.abs(z)))


def _dot(a, b):
    return jnp.dot(a, b, preferred_element_type=F32)


def _dot_nt(a, b):
    return lax.dot_general(a, b, (((1,), (1,)), ((), ())), preferred_element_type=F32)


def _swiglu_hidden(x, wg, wu):
    a = _dot(x, wg)
    return a * _sigmoid(a) * _dot(x, wu)


def _pack_rows(x):
    half = x.shape[1] // 2
    as_bits = lambda v: lax.bitcast_convert_type(v.astype(BF16).astype(F32), U32)
    return (as_bits(x[:, :half]) >> 16) | (as_bits(x[:, half:]) & jnp.uint32(0xFFFF0000))


def _unpack_rows(w):
    lo = lax.bitcast_convert_type(w << 16, F32)
    hi = lax.bitcast_convert_type(w & jnp.uint32(0xFFFF0000), F32)
    return jnp.concatenate([lo, hi], axis=1)


def _norm_inproj_kernel(x_ref, g_ref, w_ref, cs_ref, wvt_ref, o_ref, vt_ref, hn_ref, *, n_col_tiles):
    j = pl.program_id(1)

    @pl.when(j == 0)
    def _():
        hn_ref[...] = _rms(x_ref[...], g_ref[...]).astype(hn_ref.dtype)

    @pl.when(j < n_col_tiles)
    def _():
        o_ref[...] = (_dot(hn_ref[...], w_ref[...]) * cs_ref[...]).astype(o_ref.dtype)

    @pl.when(j == n_col_tiles)
    def _():
        vt = _dot_nt(wvt_ref[...], hn_ref[...]).astype(vt_ref.dtype)
        kb = vt_ref.shape[2]
        for c in range(vt_ref.shape[0]):
            vt_ref[c] = vt[:, c * kb:(c + 1) * kb]


def _norm_inproj(h, g, w, col_scale, wvt, *, key_block, tm=1024, tn=1024):
    t, d = h.shape
    n = w.shape[1]
    vw = wvt.shape[0]
    n_col_tiles = n // tn
    last = n_col_tiles - 1
    blocks = _nbytes((tm, d), F32) + _nbytes((d, tn), BF16) + _nbytes((tm, tn), BF16) + _nbytes((vw, tm), BF16)
    temps = _nbytes((tm, d), BF16) + 2 * _nbytes((tm, d), F32) + _nbytes((tm, tn), F32) + _nbytes((vw, d), BF16)
    return pl.pallas_call(
        functools.partial(_norm_inproj_kernel, n_col_tiles=n_col_tiles),
        grid=(t // tm, n_col_tiles + 1),
        in_specs=[
            pl.BlockSpec((tm, d), lambda i, j: (i, 0)),
            _resident((1, d), 2),
            pl.BlockSpec((d, tn), lambda i, j: (0, jnp.minimum(j, last))),
            pl.BlockSpec((1, tn), lambda i, j: (0, jnp.minimum(j, last))),
            _resident((vw, d), 2),
        ],
        out_specs=[
            pl.BlockSpec((tm, tn), lambda i, j: (i, jnp.minimum(j, last))),
            pl.BlockSpec((tm // key_block, vw, key_block), lambda i, j: (i, 0, 0)),
        ],
        out_shape=[jax.ShapeDtypeStruct((t, n), BF16), jax.ShapeDtypeStruct((t // key_block, vw, key_block), BF16)],
        scratch_shapes=[pltpu.VMEM((tm, d), BF16)],
        compiler_params=pltpu.CompilerParams(
            dimension_semantics=("arbitrary", "arbitrary"),
            vmem_limit_bytes=_vmem_limit(blocks, temps),
        ),
        name="norm_inproj",
    )(h, g, w, col_scale, wvt)


def _sb_attn_kernel(q_ref, k_ref, vt_ref, lo_ref, o_ref, acc_ref, run_ref, *, blk, heads):
    qi = pl.program_id(2)
    lo = lo_ref[...]
    acc_ref[...] = jnp.zeros_like(acc_ref)
    run_ref[...] = jnp.zeros_like(run_ref)

    def block(j, masked):
        start = pl.multiple_of(j * blk, blk)
        cols = [slice(h * HEAD_DIM, (h + 1) * HEAD_DIM) for h in range(heads)]
        if masked:
            causal = lax.broadcasted_iota(I32, (blk, blk), 0) < lax.broadcasted_iota(I32, (blk, blk), 1)
        zs = [_dot_nt(k_ref[0, pl.ds(start, blk), c], q_ref[0, :, c]) for c in cols]
        sps = [_softplus2(z) for z in zs]
        keeps = [jnp.where(causal, sp, 0.0) if masked else sp for sp in sps]
        css = [_dot(lo, keep.astype(BF16)) for keep in keeps]
        for h in range(heads):
            run = run_ref[h]
            run_all = jnp.concatenate([run] * (blk // V7X_SUBLANES), axis=0)
            a = jnp.exp2(zs[h] - sps[h] + css[h][:blk] + run_all)
            if masked:
                a = jnp.where(causal, a, 0.0)
            acc_ref[h] += _dot(vt_ref[j, cols[h], :], a.astype(BF16))
            run_ref[h] = run + css[h][blk:]

    block(qi, True)

    def body(n, carry):
        block(qi - 1 - n, False)
        return carry

    lax.fori_loop(0, qi, body, 0)
    for h in range(heads):
        o_ref[0, :, h * HEAD_DIM:(h + 1) * HEAD_DIM] = acc_ref[h].T.astype(o_ref.dtype)


def _sb_attention(proj3, vt, *, conv_width, sb_width, blk, heads=8):
    b, s, _ = proj3.shape
    gw = heads * HEAD_DIM
    groups = sb_width // gw
    q_off = 3 * conv_width // gw
    k_off = q_off + groups
    later = lax.broadcasted_iota(I32, (blk, blk), 1) > lax.broadcasted_iota(I32, (blk, blk), 0)
    lo = -jnp.concatenate([later.astype(BF16), jnp.ones((V7X_SUBLANES, blk), BF16)], axis=0)
    blocks = 2 * _nbytes((blk, gw), BF16) + 2 * _nbytes((s, gw), BF16)
    temps = heads * 8 * _nbytes((blk, blk), F32)
    return pl.pallas_call(
        functools.partial(_sb_attn_kernel, blk=blk, heads=heads),
        grid=(b, groups, s // blk),
        in_specs=[
            pl.BlockSpec((1, blk, gw), lambda bi, gi, qi: (bi, qi, q_off + gi)),
            pl.BlockSpec((1, s, gw), lambda bi, gi, qi: (bi, 0, k_off + gi)),
            pl.BlockSpec((s // blk, gw, blk), lambda bi, gi, qi: (bi, gi, 0)),
            _resident((blk + V7X_SUBLANES, blk), 3),
        ],
        out_specs=pl.BlockSpec((1, blk, gw), lambda bi, gi, qi: (bi, qi, gi)),
        out_shape=jax.ShapeDtypeStruct((b, s, sb_width), BF16),
        scratch_shapes=[pltpu.VMEM((heads, HEAD_DIM, blk), F32), pltpu.VMEM((heads, V7X_SUBLANES, blk), F32)],
        compiler_params=pltpu.CompilerParams(
            dimension_semantics=("arbitrary", "arbitrary", "arbitrary"),
            vmem_limit_bytes=_vmem_limit(blocks, temps),
        ),
        name="sb_attn",
    )(proj3, proj3, vt, lo)


def _rank_of(vals):
    n = vals.shape[0]
    idx = lax.broadcasted_iota(I32, vals.shape, 0)
    rank = jnp.zeros(vals.shape, F32)
    for other in range(n):
        o = vals[other:other + 1, :]
        rank = rank + jnp.where(o > vals, 1.0, 0.0) + jnp.where(o == vals, jnp.where(idx > other, 1.0, 0.0), 0.0)
    return rank


def _router(logits_t, bias):
    n_exp, tm = logits_t.shape
    per_group = n_exp // N_GROUPS
    scores = _sigmoid(logits_t)
    biased = scores + bias

    grouped = biased.reshape(N_GROUPS, per_group, tm)
    top1 = jnp.max(grouped, axis=1, keepdims=True)
    n_top = jnp.sum(jnp.where(grouped == top1, 1.0, 0.0), axis=1, keepdims=True)
    below = jnp.max(jnp.where(grouped < top1, grouped, -jnp.inf), axis=1, keepdims=True)
    group_score = (top1 + jnp.where(n_top >= 2.0, top1, below)).reshape(N_GROUPS, tm)

    group_ok = _rank_of(group_score) < float(TOPK_GROUPS)
    expert_ok = jnp.broadcast_to(group_ok.reshape(N_GROUPS, 1, tm), (N_GROUPS, per_group, tm)).reshape(n_exp, tm)
    chosen = jnp.where(_rank_of(jnp.where(expert_ok, biased, -jnp.inf)) < float(TOP_K), 1.0, 0.0)
    w = chosen * scores
    return chosen, w / jnp.sum(w, axis=0, keepdims=True) * ROUTED_SCALE


def _mixer_out_kernel(ch_ref, cb_ref, cc_ref, hh_ref, hc_ref, sb_ref, x_ref, cw_ref, gc_ref, gs_ref,
                      wo_ref, gf_ref, wr_ref, rb_ref, tri_ref, low_ref,
                      h1_ref, xt_ref, eid_ref, rank_ref, gate_ref, cnt_ref, *, tiles_per_seq):
    i = pl.program_id(0)
    u = cc_ref[...].astype(F32) * ch_ref[...].astype(F32)
    halo = hc_ref[...].astype(F32) * hh_ref[...].astype(F32)
    halo = jnp.where(i % tiles_per_seq == 0, 0.0, halo)
    prev1 = halo[V7X_SUBLANES - 1:V7X_SUBLANES, :]
    prev2 = halo[V7X_SUBLANES - 2:V7X_SUBLANES - 1, :]
    row = lax.broadcasted_iota(I32, u.shape, 0)
    u1 = jnp.where(row == 0, prev1, pltpu.roll(u, 1, 0))
    u2 = jnp.where(row == 0, prev2, jnp.where(row == 1, prev1, pltpu.roll(u, 2, 0)))
    cw = cw_ref[...]
    conv = cb_ref[...].astype(F32) * (cw[0:1, :] * u2 + cw[1:2, :] * u1 + cw[2:3, :] * u)

    y = jnp.concatenate([_rms(conv, gc_ref[...]), _rms(sb_ref[...].astype(F32), gs_ref[...])], axis=-1)
    h1 = x_ref[...] + _dot(y.astype(BF16), wo_ref[...])
    h1_ref[...] = h1
    xt = _rms(h1, gf_ref[...])
    xt_ref[...] = _pack_rows(xt)

    n_exp = wr_ref.shape[1] // 2
    xt_hi = xt.astype(BF16)
    xt_lo = (xt - xt_hi.astype(F32)).astype(BF16)
    by_hi = _dot(xt_hi, wr_ref[...])
    logits = by_hi[:, :n_exp] + by_hi[:, n_exp:] + _dot(xt_lo, wr_ref[:, :n_exp])
    chosen, gates = _router(logits.T, rb_ref[...])
    tm = chosen.shape[1]

    @pl.when(i == 0)
    def _():
        cnt_ref[...] = jnp.zeros_like(cnt_ref)

    counts = _dot(chosen.astype(BF16), tri_ref[...])
    seen = cnt_ref[...]
    rank = seen + counts[:, :tm]
    cnt_ref[...] = seen + counts[:, tm:]

    slot = _dot(low_ref[...], chosen.astype(BF16))
    expert = lax.broadcasted_iota(I32, chosen.shape, 0).astype(F32)
    eids, ranks, gsel = [], [], []
    for k in range(TOP_K):
        pick = chosen * jnp.where(slot == float(k), 1.0, 0.0)
        eids.append(jnp.sum(pick * expert, axis=0, keepdims=True))
        ranks.append(jnp.sum(pick * rank, axis=0, keepdims=True))
        gsel.append(jnp.sum(pick * gates, axis=0, keepdims=True))
    eid_ref[...] = jnp.concatenate(eids, axis=0).astype(I32)
    rank_ref[...] = jnp.concatenate(ranks, axis=0).astype(I32)
    gate_ref[...] = jnp.concatenate(gsel, axis=0).T


def _mixer_out(proj, sb, h, conv_w, gc, gs, w_out, gf, w_router, router_bias, *, seq, conv_width, tm=256):
    t, d = h.shape
    sb_width = sb.shape[1]
    n_exp = w_router.shape[1]
    router_hi = w_router.astype(BF16)
    router_lo = (w_router - router_hi.astype(F32)).astype(BF16)
    w_router = jnp.concatenate([router_hi, router_lo], axis=1)
    halo_blocks = tm // V7X_SUBLANES
    conv_spec = lambda c: pl.BlockSpec((tm, conv_width), lambda i: (i, c))
    halo_spec = lambda c: pl.BlockSpec((V7X_SUBLANES, conv_width), lambda i: (jnp.maximum(i * halo_blocks - 1, 0), c))
    earlier = lax.broadcasted_iota(I32, (tm, tm), 0) < lax.broadcasted_iota(I32, (tm, tm), 1)
    tri = jnp.concatenate([earlier.astype(BF16), jnp.ones((tm, tm), BF16)], axis=1)
    lower = (lax.broadcasted_iota(I32, (n_exp, n_exp), 1) < lax.broadcasted_iota(I32, (n_exp, n_exp), 0)).astype(BF16)
    blocks = (3 * _nbytes((tm, conv_width), BF16) + _nbytes((tm, sb_width), BF16) + 3 * _nbytes((tm, d), F32))
    temps = 10 * _nbytes((tm, d), F32) + _nbytes(w_out.shape, BF16) + _nbytes(w_router.shape, BF16)
    row_block = pl.BlockSpec((tm, d), lambda i: (i, 0))
    slot_block = pl.BlockSpec((TOP_K, tm), lambda i: (0, i))
    return pl.pallas_call(
        functools.partial(_mixer_out_kernel, tiles_per_seq=seq // tm),
        grid=(t // tm,),
        in_specs=[
            conv_spec(0), conv_spec(1), conv_spec(2), halo_spec(0), halo_spec(2),
            pl.BlockSpec((tm, sb_width), lambda i: (i, 0)),
            row_block,
            _resident(conv_w.shape, 1), _resident(gc.shape, 1), _resident(gs.shape, 1), _resident(w_out.shape, 1),
            _resident(gf.shape, 1), _resident(w_router.shape, 1), _resident(router_bias.shape, 1),
            _resident(tri.shape, 1), _resident(lower.shape, 1),
        ],
        out_specs=[
            row_block, pl.BlockSpec((tm, d // 2), lambda i: (i, 0)), slot_block, slot_block,
            pl.BlockSpec((tm, TOP_K), lambda i: (i, 0)),
            pl.BlockSpec((n_exp, tm), lambda i: (0, 0)),
        ],
        out_shape=[
            jax.ShapeDtypeStruct((t, d), F32),
            jax.ShapeDtypeStruct((t, d // 2), U32),
            jax.ShapeDtypeStruct((TOP_K, t), I32),
            jax.ShapeDtypeStruct((TOP_K, t), I32),
            jax.ShapeDtypeStruct((t, TOP_K), F32),
            jax.ShapeDtypeStruct((n_exp, tm), F32),
        ],
        compiler_params=pltpu.CompilerParams(
            dimension_semantics=("arbitrary",),
            vmem_limit_bytes=_vmem_limit(blocks, temps),
        ),
        name="mixer_out",
    )(proj, proj, proj, proj, proj, sb, h, conv_w, gc, gs, w_out, gf, w_router, router_bias, tri, lower)


def _moe_dispatch_kernel(cnt_ref, eid_ref, rank_ref, xt_ref, xs_hbm, pos_ref, texp_ref, nused_ref,
                         off_ref, zero_ref, row_sem, pad_sem, *, n_exp, n_tiles, n_assigned):
    ts = xt_ref.shape[0]

    @pl.when(pl.program_id(0) == 0)
    def _():
        zero_ref[...] = jnp.zeros_like(zero_ref)

        def layout(e, off):
            off_ref[e] = off
            tiles = (cnt_ref[e] + EXPERT_TILE - 1) >> EXPERT_TILE_LOG2
            first = off >> EXPERT_TILE_LOG2
            end = off + (tiles << EXPERT_TILE_LOG2)

            def mark(j, c):
                texp_ref[first + j] = e
                return c

            def zero_row(r, c):
                pltpu.make_async_copy(zero_ref.at[0], xs_hbm.at[r], pad_sem).start()
                return c

            lax.fori_loop(0, tiles, mark, 0)
            lax.fori_loop(off + cnt_ref[e], end, zero_row, 0)
            return end

        total = lax.fori_loop(0, n_exp, layout, 0)
        used_tiles = total >> EXPERT_TILE_LOG2
        nused_ref[0] = used_tiles

        def tail(j, c):
            texp_ref[j] = n_exp - 1
            return c

        def drain_row(r, c):
            pltpu.make_async_copy(zero_ref.at[0], xs_hbm.at[0], pad_sem).wait()
            return c

        lax.fori_loop(used_tiles, n_tiles, tail, 0)
        lax.fori_loop(n_assigned, total, drain_row, 0)

    def token_group(g, c):
        base = pl.multiple_of(g * V7X_SUBLANES, V7X_SUBLANES)
        for u in range(V7X_SUBLANES):
            t = base + u
            for k in range(TOP_K):
                dst = off_ref[eid_ref[k, t]] + rank_ref[k, t]
                pos_ref[k, t] = dst
                pltpu.make_async_copy(xt_ref.at[t], xs_hbm.at[dst], row_sem).start(priority=k % DMA_THREADS)
        return c

    lax.fori_loop(0, ts // V7X_SUBLANES, token_group, 0)
    for k in range(TOP_K):
        pltpu.make_async_copy(xt_ref, xs_hbm.at[pl.ds(0, ts), :], row_sem).wait()


def _moe_dispatch(counts, eid, rank, xt, *, ts=512):
    t, d = xt.shape
    n_exp = counts.shape[0]
    n_tiles = t * TOP_K // EXPERT_TILE + n_exp
    slot_block = pl.BlockSpec((TOP_K, ts), lambda i, cnt: (0, i), memory_space=pltpu.SMEM)
    whole_smem = lambda n: pl.BlockSpec((n,), lambda i, cnt: (0,), memory_space=pltpu.SMEM)
    blocks = _nbytes((ts, d), xt.dtype)
    temps = _nbytes((V7X_SUBLANES, d), xt.dtype)
    return pl.pallas_call(
        functools.partial(_moe_dispatch_kernel, n_exp=n_exp, n_tiles=n_tiles, n_assigned=t * TOP_K),
        grid_spec=pltpu.PrefetchScalarGridSpec(
            num_scalar_prefetch=1,
            grid=(t // ts,),
            in_specs=[slot_block, slot_block, pl.BlockSpec((ts, d), lambda i, cnt: (i, 0))],
            out_specs=[pl.BlockSpec(memory_space=pl.ANY), slot_block, whole_smem(n_tiles), whole_smem(1)],
            scratch_shapes=[
                pltpu.SMEM((n_exp,), I32),
                pltpu.VMEM((V7X_SUBLANES, d), xt.dtype),
                pltpu.SemaphoreType.DMA(()),
                pltpu.SemaphoreType.DMA(()),
            ],
        ),
        out_shape=[
            jax.ShapeDtypeStruct((n_tiles * EXPERT_TILE, d), xt.dtype),
            jax.ShapeDtypeStruct((TOP_K, t), I32),
            jax.ShapeDtypeStruct((n_tiles,), I32),
            jax.ShapeDtypeStruct((1,), I32),
        ],
        compiler_params=pltpu.CompilerParams(
            dimension_semantics=("arbitrary",),
            vmem_limit_bytes=_vmem_limit(blocks, temps),
        ),
        name="moe_dispatch",
    )(counts, eid, rank, xt)


def _moe_experts_kernel(texp_ref, nused_ref, xs_ref, wg_ref, wu_ref, wd_ref, ys_ref):
    i = pl.program_id(0)

    @pl.when(i < nused_ref[0])
    def _():
        hid = _swiglu_hidden(_unpack_rows(xs_ref[...]).astype(BF16), wg_ref[0], wu_ref[0])
        ys_ref[...] = _pack_rows(_dot(hid.astype(BF16), wd_ref[0]))

    @pl.when(i >= nused_ref[0])
    def _():
        ys_ref[...] = jnp.zeros_like(ys_ref)


def _moe_experts(texp, nused, xs, wg, wu, wd):
    rows, dw = xs.shape
    d, f = wg.shape[1:]
    n_tiles = rows // EXPERT_TILE
    blocks = 2 * _nbytes((EXPERT_TILE, dw), U32) + 3 * _nbytes((d, f), BF16)
    temps = 4 * _nbytes((EXPERT_TILE, f), F32) + 2 * _nbytes((EXPERT_TILE, d), F32) + _nbytes((EXPERT_TILE, d), BF16)
    expert_block = lambda shape: pl.BlockSpec(shape, lambda i, texp, nused: (texp[i], 0, 0))
    return pl.pallas_call(
        _moe_experts_kernel,
        grid_spec=pltpu.PrefetchScalarGridSpec(
            num_scalar_prefetch=2,
            grid=(n_tiles,),
            in_specs=[
                pl.BlockSpec((EXPERT_TILE, dw), lambda i, texp, nused: (jnp.minimum(i, nused[0] - 1), 0)),
                expert_block((1, d, f)), expert_block((1, d, f)), expert_block((1, f, d)),
            ],
            out_specs=pl.BlockSpec((EXPERT_TILE, dw), lambda i, texp, nused: (i, 0)),
        ),
        out_shape=jax.ShapeDtypeStruct((rows, dw), U32),
        compiler_params=pltpu.CompilerParams(
            dimension_semantics=("arbitrary",),
            vmem_limit_bytes=_vmem_limit(blocks, temps),
        ),
        name="moe_experts",
    )(texp, nused, xs, wg, wu, wd)


def _moe_combine_kernel(pos_ref, pos_next_ref, gate_ref, h1_ref, xt_ref, p_ref, sg_ref, su_ref, sd_ref, gp_ref,
                        wg_ref, wp_ref, gf_ref, ys_hbm, o_ref, buf_even, buf_odd, sem, *, final_norm):
    tc = h1_ref.shape[0]
    i = pl.program_id(0)

    def wait_rows(buf, half):
        for k in range(TOP_K):
            pltpu.make_async_copy(ys_hbm.at[pl.ds(0, tc), :], buf.at[k], sem.at[half]).wait()

    @pl.when(i == 0)
    def _():
        def token_group(g, c):
            base = pl.multiple_of(g * V7X_SUBLANES, V7X_SUBLANES)
            for u in range(V7X_SUBLANES):
                for k in range(TOP_K):
                    pltpu.make_async_copy(ys_hbm.at[pos_ref[k, base + u]], buf_even.at[k, base + u], sem.at[0]).start()
            return c

        lax.fori_loop(0, tc // V7X_SUBLANES, token_group, 0)

    def step(cur_buf, cur_half, nxt_buf, nxt_half):
        wait_rows(cur_buf, cur_half)
        for t in range(tc):
            for k in range(TOP_K):
                pltpu.make_async_copy(ys_hbm.at[pos_next_ref[k, t]], nxt_buf.at[k, t],
                                      sem.at[nxt_half]).start(priority=k % DMA_THREADS)

        hs = _swiglu_hidden(_unpack_rows(xt_ref[...]).astype(BF16), sg_ref[...], su_ref[...])
        h2 = h1_ref[...] + _dot(hs.astype(BF16), sd_ref[...])
        emb = _dot(p_ref[...].astype(BF16), wp_ref[...])
        gate = gate_ref[...]
        for k in range(TOP_K):
            h2 = h2 + gate[:, k:k + 1] * _unpack_rows(cur_buf[k])

        gate_ple = _sigmoid(_dot(_rms(h2, gp_ref[...]).astype(BF16), wg_ref[...]))
        h3 = h2 + gate_ple * emb
        o_ref[...] = _rms(h3, gf_ref[...]) if final_norm else h3

        @pl.when(i + 1 == pl.num_programs(0))
        def _():
            wait_rows(nxt_buf, nxt_half)

    @pl.when(i % 2 == 0)
    def _():
        step(buf_even, 0, buf_odd, 1)

    @pl.when(i % 2 == 1)
    def _():
        step(buf_odd, 1, buf_even, 0)


def _moe_combine(pos, gate, h1, xt, p, sg, su, sd, gp, wg, wp, gf, ys, *, final_norm, tc=128):
    t, d = h1.shape
    pd = p.shape[1]
    dw = ys.shape[1]
    row_block = pl.BlockSpec((tc, d), lambda i: (i, 0))
    blocks = 2 * _nbytes((tc, d), F32) + _nbytes((tc, dw), U32) + _nbytes((tc, pd), F32) + _nbytes((tc, V7X_LANES), F32)
    temps = (2 * _nbytes((TOP_K, tc, dw), U32) + 8 * _nbytes((tc, d), F32) + _nbytes(wg.shape, BF16)
             + _nbytes(wp.shape, BF16) + 3 * _nbytes(sg.shape, BF16))
    last = t // tc - 1
    return pl.pallas_call(
        functools.partial(_moe_combine_kernel, final_norm=final_norm),
        grid=(t // tc,),
        in_specs=[
            pl.BlockSpec((TOP_K, tc), lambda i: (0, i), memory_space=pltpu.SMEM),
            pl.BlockSpec((TOP_K, tc), lambda i: (0, jnp.minimum(i + 1, last)), memory_space=pltpu.SMEM),
            pl.BlockSpec((tc, TOP_K), lambda i: (i, 0)),
            row_block, pl.BlockSpec((tc, dw), lambda i: (i, 0)),
            pl.BlockSpec((tc, pd), lambda i: (i, 0)),
            _resident(sg.shape, 1), _resident(su.shape, 1), _resident(sd.shape, 1), _resident(gp.shape, 1),
            _resident(wg.shape, 1), _resident(wp.shape, 1), _resident(gf.shape, 1),
            pl.BlockSpec(memory_space=pl.ANY),
        ],
        out_specs=row_block,
        out_shape=jax.ShapeDtypeStruct((t, d), F32),
        scratch_shapes=[pltpu.VMEM((TOP_K, tc, dw), U32), pltpu.VMEM((TOP_K, tc, dw), U32),
                        pltpu.SemaphoreType.DMA((2,))],
        compiler_params=pltpu.CompilerParams(
            dimension_semantics=("arbitrary",),
            vmem_limit_bytes=_vmem_limit(blocks, temps),
        ),
        name="moe_combine",
    )(pos, pos, gate, h1, xt, p, sg, su, sd, gp, wg, wp, gf, ys)


def kernel(x, p, norm_mix_g, w_in, conv_w, gnorm_conv_g, gnorm_sb_g, w_out, norm_ffn_g, w_router, router_bias,
           w_exp_gate, w_exp_up, w_exp_down, w_sh_gate, w_sh_up, w_sh_down, norm_ple_g, w_ple_gate, w_ple_proj,
           norm_final_g):
    bsz, seq, d = x.shape
    depth = p.shape[0]
    t = bsz * seq
    conv_width = conv_w.shape[-1]
    sb_width = gnorm_sb_g.shape[-1]
    n_qk = 3 * conv_width + 2 * sb_width
    assert w_in.shape[-1] == n_qk + sb_width
    assert w_router.shape[-1] % N_GROUPS == 0

    q_lo = 3 * conv_width
    col = jnp.arange(n_qk)
    q_scale = HEAD_DIM ** -0.5 * LOG2_E
    col_scale = jnp.where((col >= q_lo) & (col < q_lo + sb_width), q_scale, 1.0).astype(F32)[None, :]
    row = lambda v: v.astype(F32)[None, :]

    h = x.reshape(t, d)
    for i in range(depth):
        w_in_bf = w_in[i].astype(BF16)
        proj, vt = _norm_inproj(h, row(norm_mix_g[i]), w_in_bf[:, :n_qk], col_scale, w_in_bf[:, n_qk:].T,
                                key_block=ATTN_BLOCK)
        sb = _sb_attention(proj.reshape(bsz, seq, n_qk), vt, conv_width=conv_width, sb_width=sb_width,
                           blk=ATTN_BLOCK)
        h1, xt, eid, rank, gate, cnt = _mixer_out(
            proj, sb.reshape(t, sb_width), h, conv_w[i], row(gnorm_conv_g[i]), row(gnorm_sb_g[i]),
            w_out[i].astype(BF16), row(norm_ffn_g[i]), w_router[i], router_bias[i].astype(F32)[:, None],
            seq=seq, conv_width=conv_width)
        xs, pos, texp, nused = _moe_dispatch(cnt[:, 0].astype(I32), eid, rank, xt)
        ys = _moe_experts(texp, nused, xs, w_exp_gate[i].astype(BF16), w_exp_up[i].astype(BF16),
                          w_exp_down[i].astype(BF16))
        h = _moe_combine(pos, gate, h1, xt, p[i].reshape(t, -1), w_sh_gate[i].astype(BF16),
                         w_sh_up[i].astype(BF16), w_sh_down[i].astype(BF16), row(norm_ple_g[i]),
                         w_ple_gate[i].astype(BF16), w_ple_proj[i].astype(BF16), row(norm_final_g), ys,
                         final_norm=(i == depth - 1))
    return h.reshape(bsz, seq, d)
```

```python
import functools

import jax
import jax.numpy as jnp
from jax import lax
from jax.experimental import pallas as pl
from jax.experimental.pallas import tpu as pltpu

NORM_EPS = 1e-6
HEAD_DIM = 128
CONV_KERNEL = 3
N_GROUPS = 8
TOPK_GROUPS = 4
TOP_K = 8
ROUTED_SCALE = 2.5
LOG2_E = 1.4426950408889634

V7X_VMEM_BYTES = 64 * 1024 * 1024
V7X_SUBLANES = 8
V7X_LANES = 128
DMA_THREADS = 2

F32 = jnp.float32
BF16 = jnp.bfloat16
I32 = jnp.int32
U32 = jnp.uint32

ATTN_BLOCK = 256
EXPERT_TILE = 512
EXPERT_TILE_LOG2 = 9
assert 1 << EXPERT_TILE_LOG2 == EXPERT_TILE


def _vmem_limit(pipelined_bytes, resident_bytes):
    want = 2 * pipelined_bytes + resident_bytes
    return int(min(want, V7X_VMEM_BYTES - 4 * 1024 * 1024))


def _nbytes(shape, dtype):
    n = 1
    for s in shape:
        n *= s
    return n * jnp.dtype(dtype).itemsize


def _resident(shape, n_grid_axes, n_prefetch=0):
    zeros = (0,) * len(shape)
    return pl.BlockSpec(shape, lambda *_: zeros, pipeline_mode=pl.Buffered(1))


def _rms(y, g):
    return y * lax.rsqrt(jnp.mean(y * y, axis=-1, keepdims=True) + NORM_EPS) * g


def _sigmoid(a):
    return 1.0 / (1.0 + jnp.exp(-a))


def _softplus2(z):
    return jnp.maximum(z, 0.0) + jnp.log2(1.0 + jnp.exp2(-jnp.abs(z)))


def _dot(a, b):
    return jnp.dot(a, b, preferred_element_type=F32)


def _dot_nt(a, b):
    return lax.dot_general(a, b, (((1,), (1,)), ((), ())), preferred_element_type=F32)


def _swiglu_hidden(x, wg, wu):
    a = _dot(x, wg)
    return a * _sigmoid(a) * _dot(x, wu)


def _pack_rows(x):
    half = x.shape[1] // 2
    as_bits = lambda v: lax.bitcast_convert_type(v.astype(BF16).astype(F32), U32)
    return (as_bits(x[:, :half]) >> 16) | (as_bits(x[:, half:]) & jnp.uint32(0xFFFF0000))


def _unpack_rows(w):
    lo = lax.bitcast_convert_type(w << 16, F32)
    hi = lax.bitcast_convert_type(w & jnp.uint32(0xFFFF0000), F32)
    return jnp.concatenate([lo, hi], axis=1)


def _slab_shape(words):
    return (words // V7X_LANES, V7X_LANES)


def _store_slabs(ref, words):
    by_chunk = jnp.stack([words[:, j * V7X_LANES:(j + 1) * V7X_LANES] for j in range(ref.shape[1])], axis=0)
    ref[...] = pltpu.einshape("jrl->rjl", by_chunk)


def _load_slabs(ref):
    by_chunk = pltpu.einshape("rjl->jrl", ref[...])
    return jnp.concatenate([by_chunk[j] for j in range(ref.shape[1])], axis=1)


def _norm_inproj_kernel(x_ref, g_ref, w_ref, cs_ref, wvt_ref, o_ref, vt_ref, hn_ref, *, n_col_tiles):
    j = pl.program_id(1)

    @pl.when(j == 0)
    def _():
        hn_ref[...] = _rms(x_ref[...], g_ref[...]).astype(hn_ref.dtype)

    @pl.when(j < n_col_tiles)
    def _():
        o_ref[...] = (_dot(hn_ref[...], w_ref[...]) * cs_ref[...]).astype(o_ref.dtype)

    @pl.when(j == n_col_tiles)
    def _():
        vt = _dot_nt(wvt_ref[...], hn_ref[...]).astype(vt_ref.dtype)
        kb = vt_ref.shape[2]
        for c in range(vt_ref.shape[0]):
            vt_ref[c] = vt[:, c * kb:(c + 1) * kb]


def _norm_inproj(h, g, w, col_scale, wvt, *, key_block, tm=1024, tn=1024):
    t, d = h.shape
    n = w.shape[1]
    vw = wvt.shape[0]
    n_col_tiles = n // tn
    last = n_col_tiles - 1
    blocks = _nbytes((tm, d), F32) + _nbytes((d, tn), BF16) + _nbytes((tm, tn), BF16) + _nbytes((vw, tm), BF16)
    temps = _nbytes((tm, d), BF16) + 2 * _nbytes((tm, d), F32) + _nbytes((tm, tn), F32) + _nbytes((vw, d), BF16)
    return pl.pallas_call(
        functools.partial(_norm_inproj_kernel, n_col_tiles=n_col_tiles),
        grid=(t // tm, n_col_tiles + 1),
        in_specs=[
            pl.BlockSpec((tm, d), lambda i, j: (i, 0)),
            _resident((1, d), 2),
            pl.BlockSpec((d, tn), lambda i, j: (0, jnp.minimum(j, last))),
            pl.BlockSpec((1, tn), lambda i, j: (0, jnp.minimum(j, last))),
            _resident((vw, d), 2),
        ],
        out_specs=[
            pl.BlockSpec((tm, tn), lambda i, j: (i, jnp.minimum(j, last))),
            pl.BlockSpec((tm // key_block, vw, key_block), lambda i, j: (i, 0, 0)),
        ],
        out_shape=[jax.ShapeDtypeStruct((t, n), BF16), jax.ShapeDtypeStruct((t // key_block, vw, key_block), BF16)],
        scratch_shapes=[pltpu.VMEM((tm, d), BF16)],
        compiler_params=pltpu.CompilerParams(
            dimension_semantics=("arbitrary", "arbitrary"),
            vmem_limit_bytes=_vmem_limit(blocks, temps),
        ),
        name="norm_inproj",
    )(h, g, w, col_scale, wvt)


def _sb_attn_kernel(q_ref, k_ref, vt_ref, lo_ref, o_ref, acc_ref, run_ref, *, blk, heads):
    qi = pl.program_id(2)
    lo = lo_ref[...]
    acc_ref[...] = jnp.zeros_like(acc_ref)
    run_ref[...] = jnp.zeros_like(run_ref)

    def block(j, masked):
        start = pl.multiple_of(j * blk, blk)
        cols = [slice(h * HEAD_DIM, (h + 1) * HEAD_DIM) for h in range(heads)]
        if masked:
            causal = lax.broadcasted_iota(I32, (blk, blk), 0) < lax.broadcasted_iota(I32, (blk, blk), 1)
        zs = [_dot_nt(k_ref[0, pl.ds(start, blk), c], q_ref[0, :, c]) for c in cols]
        sps = [_softplus2(z) for z in zs]
        keeps = [jnp.where(causal, sp, 0.0) if masked else sp for sp in sps]
        css = [_dot(lo, keep.astype(BF16)) for keep in keeps]
        for h in range(heads):
            run = run_ref[h]
            run_all = jnp.concatenate([run] * (blk // V7X_SUBLANES), axis=0)
            a = jnp.exp2(zs[h] - sps[h] + css[h][:blk] + run_all)
            if masked:
                a = jnp.where(causal, a, 0.0)
            acc_ref[h] += _dot(vt_ref[j, cols[h], :], a.astype(BF16))
            run_ref[h] = run + css[h][blk:]

    block(qi, True)

    def body(n, carry):
        block(qi - 1 - n, False)
        return carry

    lax.fori_loop(0, qi, body, 0)
    for h in range(heads):
        o_ref[0, :, h * HEAD_DIM:(h + 1) * HEAD_DIM] = acc_ref[h].T.astype(o_ref.dtype)


def _sb_attention(proj3, vt, *, conv_width, sb_width, blk, heads=8):
    b, s, _ = proj3.shape
    gw = heads * HEAD_DIM
    groups = sb_width // gw
    q_off = 3 * conv_width // gw
    k_off = q_off + groups
    later = lax.broadcasted_iota(I32, (blk, blk), 1) > lax.broadcasted_iota(I32, (blk, blk), 0)
    lo = -jnp.concatenate([later.astype(BF16), jnp.ones((V7X_SUBLANES, blk), BF16)], axis=0)
    blocks = 2 * _nbytes((blk, gw), BF16) + 2 * _nbytes((s, gw), BF16)
    temps = heads * 8 * _nbytes((blk, blk), F32)
    return pl.pallas_call(
        functools.partial(_sb_attn_kernel, blk=blk, heads=heads),
        grid=(b, groups, s // blk),
        in_specs=[
            pl.BlockSpec((1, blk, gw), lambda bi, gi, qi: (bi, qi, q_off + gi)),
            pl.BlockSpec((1, s, gw), lambda bi, gi, qi: (bi, 0, k_off + gi)),
            pl.BlockSpec((s // blk, gw, blk), lambda bi, gi, qi: (bi, gi, 0)),
            _resident((blk + V7X_SUBLANES, blk), 3),
        ],
        out_specs=pl.BlockSpec((1, blk, gw), lambda bi, gi, qi: (bi, qi, gi)),
        out_shape=jax.ShapeDtypeStruct((b, s, sb_width), BF16),
        scratch_shapes=[pltpu.VMEM((heads, HEAD_DIM, blk), F32), pltpu.VMEM((heads, V7X_SUBLANES, blk), F32)],
        compiler_params=pltpu.CompilerParams(
            dimension_semantics=("arbitrary", "arbitrary", "arbitrary"),
            vmem_limit_bytes=_vmem_limit(blocks, temps),
        ),
        name="sb_attn",
    )(proj3, proj3, vt, lo)


def _rank_of(vals):
    n = vals.shape[0]
    idx = lax.broadcasted_iota(I32, vals.shape, 0)
    rank = jnp.zeros(vals.shape, F32)
    for other in range(n):
        o = vals[other:other + 1, :]
        rank = rank + jnp.where(o > vals, 1.0, 0.0) + jnp.where(o == vals, jnp.where(idx > other, 1.0, 0.0), 0.0)
    return rank


def _router(logits_t, bias):
    n_exp, tm = logits_t.shape
    per_group = n_exp // N_GROUPS
    scores = _sigmoid(logits_t)
    biased = scores + bias

    grouped = biased.reshape(N_GROUPS, per_group, tm)
    top1 = jnp.max(grouped, axis=1, keepdims=True)
    n_top = jnp.sum(jnp.where(grouped == top1, 1.0, 0.0), axis=1, keepdims=True)
    below = jnp.max(jnp.where(grouped < top1, grouped, -jnp.inf), axis=1, keepdims=True)
    group_score = (top1 + jnp.where(n_top >= 2.0, top1, below)).reshape(N_GROUPS, tm)

    group_ok = _rank_of(group_score) < float(TOPK_GROUPS)
    expert_ok = jnp.broadcast_to(group_ok.reshape(N_GROUPS, 1, tm), (N_GROUPS, per_group, tm)).reshape(n_exp, tm)
    chosen = jnp.where(_rank_of(jnp.where(expert_ok, biased, -jnp.inf)) < float(TOP_K), 1.0, 0.0)
    w = chosen * scores
    return chosen, w / jnp.sum(w, axis=0, keepdims=True) * ROUTED_SCALE


def _mixer_out_kernel(ch_ref, cb_ref, cc_ref, hh_ref, hc_ref, sb_ref, x_ref, cw_ref, gc_ref, gs_ref,
                      wo_ref, gf_ref, wr_ref, rb_ref, tri_ref, low_ref,
                      h1_ref, xt_ref, eid_ref, rank_ref, gate_ref, cnt_ref, *, tiles_per_seq):
    i = pl.program_id(0)
    u = cc_ref[...].astype(F32) * ch_ref[...].astype(F32)
    halo = hc_ref[...].astype(F32) * hh_ref[...].astype(F32)
    halo = jnp.where(i % tiles_per_seq == 0, 0.0, halo)
    prev1 = halo[V7X_SUBLANES - 1:V7X_SUBLANES, :]
    prev2 = halo[V7X_SUBLANES - 2:V7X_SUBLANES - 1, :]
    row = lax.broadcasted_iota(I32, u.shape, 0)
    u1 = jnp.where(row == 0, prev1, pltpu.roll(u, 1, 0))
    u2 = jnp.where(row == 0, prev2, jnp.where(row == 1, prev1, pltpu.roll(u, 2, 0)))
    cw = cw_ref[...]
    conv = cb_ref[...].astype(F32) * (cw[0:1, :] * u2 + cw[1:2, :] * u1 + cw[2:3, :] * u)

    y = jnp.concatenate([_rms(conv, gc_ref[...]), _rms(sb_ref[...].astype(F32), gs_ref[...])], axis=-1)
    h1 = x_ref[...] + _dot(y.astype(BF16), wo_ref[...])
    h1_ref[...] = h1
    xt = _rms(h1, gf_ref[...])
    _store_slabs(xt_ref, _pack_rows(xt))

    n_exp = wr_ref.shape[1] // 2
    xt_hi = xt.astype(BF16)
    xt_lo = (xt - xt_hi.astype(F32)).astype(BF16)
    by_hi = _dot(xt_hi, wr_ref[...])
    logits = by_hi[:, :n_exp] + by_hi[:, n_exp:] + _dot(xt_lo, wr_ref[:, :n_exp])
    chosen, gates = _router(logits.T, rb_ref[...])
    tm = chosen.shape[1]

    @pl.when(i == 0)
    def _():
        cnt_ref[...] = jnp.zeros_like(cnt_ref)

    counts = _dot(chosen.astype(BF16), tri_ref[...])
    seen = cnt_ref[...]
    rank = seen + counts[:, :tm]
    cnt_ref[...] = seen + counts[:, tm:]

    slot = _dot(low_ref[...], chosen.astype(BF16))
    expert = lax.broadcasted_iota(I32, chosen.shape, 0).astype(F32)
    eids, ranks, gsel = [], [], []
    for k in range(TOP_K):
        pick = chosen * jnp.where(slot == float(k), 1.0, 0.0)
        eids.append(jnp.sum(pick * expert, axis=0, keepdims=True))
        ranks.append(jnp.sum(pick * rank, axis=0, keepdims=True))
        gsel.append(jnp.sum(pick * gates, axis=0, keepdims=True))
    eid_ref[...] = jnp.concatenate(eids, axis=0).astype(I32)
    rank_ref[...] = jnp.concatenate(ranks, axis=0).astype(I32)
    gate_ref[...] = jnp.concatenate(gsel, axis=0).T


def _mixer_out(proj, sb, h, conv_w, gc, gs, w_out, gf, w_router, router_bias, *, seq, conv_width, tm=256):
    t, d = h.shape
    sb_width = sb.shape[1]
    n_exp = w_router.shape[1]
    router_hi = w_router.astype(BF16)
    router_lo = (w_router - router_hi.astype(F32)).astype(BF16)
    w_router = jnp.concatenate([router_hi, router_lo], axis=1)
    halo_blocks = tm // V7X_SUBLANES
    conv_spec = lambda c: pl.BlockSpec((tm, conv_width), lambda i: (i, c))
    halo_spec = lambda c: pl.BlockSpec((V7X_SUBLANES, conv_width), lambda i: (jnp.maximum(i * halo_blocks - 1, 0), c))
    earlier = lax.broadcasted_iota(I32, (tm, tm), 0) < lax.broadcasted_iota(I32, (tm, tm), 1)
    tri = jnp.concatenate([earlier.astype(BF16), jnp.ones((tm, tm), BF16)], axis=1)
    lower = (lax.broadcasted_iota(I32, (n_exp, n_exp), 1) < lax.broadcasted_iota(I32, (n_exp, n_exp), 0)).astype(BF16)
    blocks = (3 * _nbytes((tm, conv_width), BF16) + _nbytes((tm, sb_width), BF16) + 3 * _nbytes((tm, d), F32))
    temps = 10 * _nbytes((tm, d), F32) + _nbytes(w_out.shape, BF16) + _nbytes(w_router.shape, BF16)
    row_block = pl.BlockSpec((tm, d), lambda i: (i, 0))
    slot_block = pl.BlockSpec((TOP_K, tm), lambda i: (0, i))
    return pl.pallas_call(
        functools.partial(_mixer_out_kernel, tiles_per_seq=seq // tm),
        grid=(t // tm,),
        in_specs=[
            conv_spec(0), conv_spec(1), conv_spec(2), halo_spec(0), halo_spec(2),
            pl.BlockSpec((tm, sb_width), lambda i: (i, 0)),
            row_block,
            _resident(conv_w.shape, 1), _resident(gc.shape, 1), _resident(gs.shape, 1), _resident(w_out.shape, 1),
            _resident(gf.shape, 1), _resident(w_router.shape, 1), _resident(router_bias.shape, 1),
            _resident(tri.shape, 1), _resident(lower.shape, 1),
        ],
        out_specs=[
            row_block, pl.BlockSpec((tm,) + _slab_shape(d // 2), lambda i: (i, 0, 0)), slot_block, slot_block,
            pl.BlockSpec((tm, TOP_K), lambda i: (i, 0)),
            pl.BlockSpec((n_exp, tm), lambda i: (0, 0)),
        ],
        out_shape=[
            jax.ShapeDtypeStruct((t, d), F32),
            jax.ShapeDtypeStruct((t,) + _slab_shape(d // 2), U32),
            jax.ShapeDtypeStruct((TOP_K, t), I32),
            jax.ShapeDtypeStruct((TOP_K, t), I32),
            jax.ShapeDtypeStruct((t, TOP_K), F32),
            jax.ShapeDtypeStruct((n_exp, tm), F32),
        ],
        compiler_params=pltpu.CompilerParams(
            dimension_semantics=("arbitrary",),
            vmem_limit_bytes=_vmem_limit(blocks, temps),
        ),
        name="mixer_out",
    )(proj, proj, proj, proj, proj, sb, h, conv_w, gc, gs, w_out, gf, w_router, router_bias, tri, lower)


def _moe_layout_kernel(cnt_ref, eid_ref, rank_ref, pos_ref, off_ref, texp_ref, nused_ref, *, n_exp, n_tiles):
    def layout(e, off):
        off_ref[e] = off
        tiles = (cnt_ref[e] + EXPERT_TILE - 1) >> EXPERT_TILE_LOG2
        first = off >> EXPERT_TILE_LOG2

        def mark(j, c):
            texp_ref[first + j] = e
            return c

        lax.fori_loop(0, tiles, mark, 0)
        return off + (tiles << EXPERT_TILE_LOG2)

    total = lax.fori_loop(0, n_exp, layout, 0)
    off_ref[n_exp] = total
    used_tiles = total >> EXPERT_TILE_LOG2
    nused_ref[0] = used_tiles

    def tail(j, c):
        texp_ref[j] = n_exp - 1
        return c

    lax.fori_loop(used_tiles, n_tiles, tail, 0)

    eid = eid_ref[...]
    pos = rank_ref[...]
    for e in range(n_exp):
        pos = pos + jnp.where(eid == e, off_ref[e], 0)
    pos_ref[...] = pos


def _moe_layout(counts, eid, rank):
    n_exp = counts.shape[0]
    t = eid.shape[1]
    n_tiles = t * TOP_K // EXPERT_TILE + n_exp
    whole = lambda shape: pl.BlockSpec(shape, lambda i, cnt: (0,) * len(shape))
    whole_smem = lambda n: pl.BlockSpec((n,), lambda i, cnt: (0,), memory_space=pltpu.SMEM)
    return pl.pallas_call(
        functools.partial(_moe_layout_kernel, n_exp=n_exp, n_tiles=n_tiles),
        grid_spec=pltpu.PrefetchScalarGridSpec(
            num_scalar_prefetch=1,
            grid=(1,),
            in_specs=[whole(eid.shape), whole(rank.shape)],
            out_specs=[whole(eid.shape), whole_smem(n_exp + 1), whole_smem(n_tiles), whole_smem(1)],
        ),
        out_shape=[
            jax.ShapeDtypeStruct(eid.shape, I32),
            jax.ShapeDtypeStruct((n_exp + 1,), I32),
            jax.ShapeDtypeStruct((n_tiles,), I32),
            jax.ShapeDtypeStruct((1,), I32),
        ],
        compiler_params=pltpu.CompilerParams(
            dimension_semantics=("arbitrary",),
            vmem_limit_bytes=_vmem_limit(3 * _nbytes(eid.shape, I32), 4 * _nbytes(eid.shape, I32)),
        ),
        name="moe_layout",
    )(counts, eid, rank)


def _moe_dispatch_kernel(cnt_ref, off_ref, pos_ref, xt_ref, xs_hbm, zero_ref, row_sem, pad_sem, *, n_exp, n_assigned):
    ts = xt_ref.shape[0]

    @pl.when(pl.program_id(0) == 0)
    def _():
        zero_ref[...] = jnp.zeros_like(zero_ref)

        def pad(e, c):
            def zero_row(r, c):
                pltpu.make_async_copy(zero_ref.at[0], xs_hbm.at[r], pad_sem).start()
                return c

            lax.fori_loop(off_ref[e] + cnt_ref[e], off_ref[e + 1], zero_row, 0)
            return c

        def drain_row(r, c):
            pltpu.make_async_copy(zero_ref.at[0], xs_hbm.at[0], pad_sem).wait()
            return c

        lax.fori_loop(0, n_exp, pad, 0)
        lax.fori_loop(n_assigned, off_ref[n_exp], drain_row, 0)

    def token_group(g, c):
        base = pl.multiple_of(g * V7X_SUBLANES, V7X_SUBLANES)
        for u in range(V7X_SUBLANES):
            t = base + u
            for k in range(TOP_K):
                pltpu.make_async_copy(xt_ref.at[t], xs_hbm.at[pos_ref[k, t]], row_sem).start(priority=k % DMA_THREADS)
        return c

    lax.fori_loop(0, ts // V7X_SUBLANES, token_group, 0)
    for k in range(TOP_K):
        pltpu.make_async_copy(xt_ref, xs_hbm.at[pl.ds(0, ts)], row_sem).wait()


def _moe_dispatch(counts, off, pos, xt, *, ts=512):
    t = xt.shape[0]
    slab = xt.shape[1:]
    n_exp = counts.shape[0]
    n_tiles = t * TOP_K // EXPERT_TILE + n_exp
    blocks = _nbytes((ts,) + slab, xt.dtype)
    temps = _nbytes((1,) + slab, xt.dtype)
    return pl.pallas_call(
        functools.partial(_moe_dispatch_kernel, n_exp=n_exp, n_assigned=t * TOP_K),
        grid_spec=pltpu.PrefetchScalarGridSpec(
            num_scalar_prefetch=2,
            grid=(t // ts,),
            in_specs=[
                pl.BlockSpec((TOP_K, ts), lambda i, cnt, off: (0, i), memory_space=pltpu.SMEM),
                pl.BlockSpec((ts,) + slab, lambda i, cnt, off: (i, 0, 0)),
            ],
            out_specs=pl.BlockSpec(memory_space=pl.ANY),
            scratch_shapes=[
                pltpu.VMEM((1,) + slab, xt.dtype),
                pltpu.SemaphoreType.DMA(()),
                pltpu.SemaphoreType.DMA(()),
            ],
        ),
        out_shape=jax.ShapeDtypeStruct((n_tiles * EXPERT_TILE,) + slab, xt.dtype),
        compiler_params=pltpu.CompilerParams(
            dimension_semantics=("arbitrary",),
            vmem_limit_bytes=_vmem_limit(blocks, temps),
        ),
        name="moe_dispatch",
    )(counts, off, pos, xt)


def _moe_experts_kernel(texp_ref, nused_ref, xs_ref, wg_ref, wu_ref, wd_ref, ys_ref):
    i = pl.program_id(0)

    @pl.when(i < nused_ref[0])
    def _():
        hid = _swiglu_hidden(_unpack_rows(_load_slabs(xs_ref)).astype(BF16), wg_ref[0], wu_ref[0])
        _store_slabs(ys_ref, _pack_rows(_dot(hid.astype(BF16), wd_ref[0])))

    @pl.when(i >= nused_ref[0])
    def _():
        ys_ref[...] = jnp.zeros_like(ys_ref)


def _moe_experts(texp, nused, xs, wg, wu, wd):
    rows = xs.shape[0]
    slab = xs.shape[1:]
    d, f = wg.shape[1:]
    n_tiles = rows // EXPERT_TILE
    blocks = 2 * _nbytes((EXPERT_TILE,) + slab, U32) + 3 * _nbytes((d, f), BF16)
    temps = 4 * _nbytes((EXPERT_TILE, f), F32) + 2 * _nbytes((EXPERT_TILE, d), F32) + _nbytes((EXPERT_TILE, d), BF16)
    expert_block = lambda shape: pl.BlockSpec(shape, lambda i, texp, nused: (texp[i], 0, 0))
    return pl.pallas_call(
        _moe_experts_kernel,
        grid_spec=pltpu.PrefetchScalarGridSpec(
            num_scalar_prefetch=2,
            grid=(n_tiles,),
            in_specs=[
                pl.BlockSpec((EXPERT_TILE,) + slab, lambda i, texp, nused: (jnp.minimum(i, nused[0] - 1), 0, 0)),
                expert_block((1, d, f)), expert_block((1, d, f)), expert_block((1, f, d)),
            ],
            out_specs=pl.BlockSpec((EXPERT_TILE,) + slab, lambda i, texp, nused: (i, 0, 0)),
        ),
        out_shape=jax.ShapeDtypeStruct((rows,) + slab, U32),
        compiler_params=pltpu.CompilerParams(
            dimension_semantics=("arbitrary",),
            vmem_limit_bytes=_vmem_limit(blocks, temps),
        ),
        name="moe_experts",
    )(texp, nused, xs, wg, wu, wd)


def _moe_combine_kernel(pos_ref, pos_next_ref, gate_ref, h1_ref, xt_ref, p_ref, sg_ref, su_ref, sd_ref, gp_ref,
                        wg_ref, wp_ref, gf_ref, ys_hbm, o_ref, buf_even, buf_odd, sem, *, final_norm):
    tc = h1_ref.shape[0]
    i = pl.program_id(0)

    def wait_rows(buf, half):
        for k in range(TOP_K):
            pltpu.make_async_copy(ys_hbm.at[pl.ds(0, tc)], buf.at[k], sem.at[half]).wait()

    @pl.when(i == 0)
    def _():
        def token_group(g, c):
            base = pl.multiple_of(g * V7X_SUBLANES, V7X_SUBLANES)
            for u in range(V7X_SUBLANES):
                for k in range(TOP_K):
                    pltpu.make_async_copy(ys_hbm.at[pos_ref[k, base + u]], buf_even.at[k, base + u], sem.at[0]).start()
            return c

        lax.fori_loop(0, tc // V7X_SUBLANES, token_group, 0)

    def step(cur_buf, cur_half, nxt_buf, nxt_half):
        wait_rows(cur_buf, cur_half)
        for t in range(tc):
            for k in range(TOP_K):
                pltpu.make_async_copy(ys_hbm.at[pos_next_ref[k, t]], nxt_buf.at[k, t],
                                      sem.at[nxt_half]).start(priority=k % DMA_THREADS)

        hs = _swiglu_hidden(_unpack_rows(_load_slabs(xt_ref)).astype(BF16), sg_ref[...], su_ref[...])
        h2 = h1_ref[...] + _dot(hs.astype(BF16), sd_ref[...])
        emb = _dot(p_ref[...].astype(BF16), wp_ref[...])
        gate = gate_ref[...]
        for k in range(TOP_K):
            h2 = h2 + gate[:, k:k + 1] * _unpack_rows(_load_slabs(cur_buf.at[k]))

        gate_ple = _sigmoid(_dot(_rms(h2, gp_ref[...]).astype(BF16), wg_ref[...]))
        h3 = h2 + gate_ple * emb
        o_ref[...] = _rms(h3, gf_ref[...]) if final_norm else h3

        @pl.when(i + 1 == pl.num_programs(0))
        def _():
            wait_rows(nxt_buf, nxt_half)

    @pl.when(i % 2 == 0)
    def _():
        step(buf_even, 0, buf_odd, 1)

    @pl.when(i % 2 == 1)
    def _():
        step(buf_odd, 1, buf_even, 0)


def _moe_combine(pos, gate, h1, xt, p, sg, su, sd, gp, wg, wp, gf, ys, *, final_norm, tc=128):
    t, d = h1.shape
    pd = p.shape[1]
    slab = ys.shape[1:]
    row_block = pl.BlockSpec((tc, d), lambda i: (i, 0))
    blocks = (2 * _nbytes((tc, d), F32) + _nbytes((tc,) + slab, U32) + _nbytes((tc, pd), F32)
              + _nbytes((tc, V7X_LANES), F32))
    temps = (2 * _nbytes((TOP_K, tc) + slab, U32) + 8 * _nbytes((tc, d), F32) + _nbytes(wg.shape, BF16)
             + _nbytes(wp.shape, BF16) + 3 * _nbytes(sg.shape, BF16))
    last = t // tc - 1
    return pl.pallas_call(
        functools.partial(_moe_combine_kernel, final_norm=final_norm),
        grid=(t // tc,),
        in_specs=[
            pl.BlockSpec((TOP_K, tc), lambda i: (0, i), memory_space=pltpu.SMEM),
            pl.BlockSpec((TOP_K, tc), lambda i: (0, jnp.minimum(i + 1, last)), memory_space=pltpu.SMEM),
            pl.BlockSpec((tc, TOP_K), lambda i: (i, 0)),
            row_block, pl.BlockSpec((tc,) + slab, lambda i: (i, 0, 0)),
            pl.BlockSpec((tc, pd), lambda i: (i, 0)),
            _resident(sg.shape, 1), _resident(su.shape, 1), _resident(sd.shape, 1), _resident(gp.shape, 1),
            _resident(wg.shape, 1), _resident(wp.shape, 1), _resident(gf.shape, 1),
            pl.BlockSpec(memory_space=pl.ANY),
        ],
        out_specs=row_block,
        out_shape=jax.ShapeDtypeStruct((t, d), F32),
        scratch_shapes=[pltpu.VMEM((TOP_K, tc) + slab, U32), pltpu.VMEM((TOP_K, tc) + slab, U32),
                        pltpu.SemaphoreType.DMA((2,))],
        compiler_params=pltpu.CompilerParams(
            dimension_semantics=("arbitrary",),
            vmem_limit_bytes=_vmem_limit(blocks, temps),
        ),
        name="moe_combine",
    )(pos, pos, gate, h1, xt, p, sg, su, sd, gp, wg, wp, gf, ys)


def kernel(x, p, norm_mix_g, w_in, conv_w, gnorm_conv_g, gnorm_sb_g, w_out, norm_ffn_g, w_router, router_bias,
           w_exp_gate, w_exp_up, w_exp_down, w_sh_gate, w_sh_up, w_sh_down, norm_ple_g, w_ple_gate, w_ple_proj,
           norm_final_g):
    bsz, seq, d = x.shape
    depth = p.shape[0]
    t = bsz * seq
    conv_width = conv_w.shape[-1]
    sb_width = gnorm_sb_g.shape[-1]
    n_qk = 3 * conv_width + 2 * sb_width
    assert w_in.shape[-1] == n_qk + sb_width
    assert w_router.shape[-1] % N_GROUPS == 0

    q_lo = 3 * conv_width
    col = jnp.arange(n_qk)
    q_scale = HEAD_DIM ** -0.5 * LOG2_E
    col_scale = jnp.where((col >= q_lo) & (col < q_lo + sb_width), q_scale, 1.0).astype(F32)[None, :]
    row = lambda v: v.astype(F32)[None, :]

    h = x.reshape(t, d)
    for i in range(depth):
        w_in_bf = w_in[i].astype(BF16)
        proj, vt = _norm_inproj(h, row(norm_mix_g[i]), w_in_bf[:, :n_qk], col_scale, w_in_bf[:, n_qk:].T,
                                key_block=ATTN_BLOCK)
        sb = _sb_attention(proj.reshape(bsz, seq, n_qk), vt, conv_width=conv_width, sb_width=sb_width,
                           blk=ATTN_BLOCK)
        h1, xt, eid, rank, gate, cnt = _mixer_out(
            proj, sb.reshape(t, sb_width), h, conv_w[i], row(gnorm_conv_g[i]), row(gnorm_sb_g[i]),
            w_out[i].astype(BF16), row(norm_ffn_g[i]), w_router[i], router_bias[i].astype(F32)[:, None],
            seq=seq, conv_width=conv_width)
        counts = cnt[:, 0].astype(I32)
        pos, off, texp, nused = _moe_layout(counts, eid, rank)
        xs = _moe_dispatch(counts, off, pos, xt)
        ys = _moe_experts(texp, nused, xs, w_exp_gate[i].astype(BF16), w_exp_up[i].astype(BF16),
                          w_exp_down[i].astype(BF16))
        h = _moe_combine(pos, gate, h1, xt, p[i].reshape(t, -1), w_sh_gate[i].astype(BF16),
                         w_sh_up[i].astype(BF16), w_sh_down[i].astype(BF16), row(norm_ple_g[i]),
                         w_ple_gate[i].astype(BF16), w_ple_proj[i].astype(BF16), row(norm_final_g), ys,
                         final_norm=(i == depth - 1))
    return h.reshape(bsz, seq, d)
```

```python
import functools

import jax
import jax.numpy as jnp
from jax import lax
from jax.experimental import pallas as pl
from jax.experimental.pallas import tpu as pltpu

NORM_EPS = 1e-6
HEAD_DIM = 128
CONV_KERNEL = 3
N_GROUPS = 8
TOPK_GROUPS = 4
TOP_K = 8
ROUTED_SCALE = 2.5
LOG2_E = 1.4426950408889634

V7X_VMEM_BYTES = 64 * 1024 * 1024
V7X_SUBLANES = 8
V7X_LANES = 128
DMA_THREADS = 2

F32 = jnp.float32
BF16 = jnp.bfloat16
I32 = jnp.int32
U32 = jnp.uint32

ATTN_BLOCK = 256
EXPERT_TILE = 512
EXPERT_TILE_LOG2 = 9
assert 1 << EXPERT_TILE_LOG2 == EXPERT_TILE


def _vmem_limit(pipelined_bytes, resident_bytes):
    want = 2 * pipelined_bytes + resident_bytes
    return int(min(want, V7X_VMEM_BYTES - 4 * 1024 * 1024))


def _nbytes(shape, dtype):
    n = 1
    for s in shape:
        n *= s
    return n * jnp.dtype(dtype).itemsize


def _resident(shape, n_grid_axes, n_prefetch=0):
    zeros = (0,) * len(shape)
    return pl.BlockSpec(shape, lambda *_: zeros, pipeline_mode=pl.Buffered(1))


def _rms(y, g):
    return y * lax.rsqrt(jnp.mean(y * y, axis=-1, keepdims=True) + NORM_EPS) * g


def _sigmoid(a):
    return 1.0 / (1.0 + jnp.exp(-a))


def _softplus2(z):
    return jnp.maximum(z, 0.0) + jnp.log2(1.0 + jnp.exp2(-jnp.abs(z)))


def _dot(a, b):
    return jnp.dot(a, b, preferred_element_type=F32)


def _dot_nt(a, b):
    return lax.dot_general(a, b, (((1,), (1,)), ((), ())), preferred_element_type=F32)


def _swiglu_hidden(x, wg, wu):
    a = _dot(x, wg)
    return a * _sigmoid(a) * _dot(x, wu)


def _pack_rows(x):
    half = x.shape[1] // 2
    as_bits = lambda v: lax.bitcast_convert_type(v.astype(BF16).astype(F32), U32)
    return (as_bits(x[:, :half]) >> 16) | (as_bits(x[:, half:]) & jnp.uint32(0xFFFF0000))


def _unpack_rows(w):
    lo = lax.bitcast_convert_type(w << 16, F32)
    hi = lax.bitcast_convert_type(w & jnp.uint32(0xFFFF0000), F32)
    return jnp.concatenate([lo, hi], axis=1)


def _slab_shape(words):
    return (words // V7X_LANES, V7X_LANES)


def _store_slabs(ref, words):
    by_chunk = jnp.stack([words[:, j * V7X_LANES:(j + 1) * V7X_LANES] for j in range(ref.shape[1])], axis=0)
    ref[...] = pltpu.einshape("jrl->rjl", by_chunk)


def _load_slabs(ref):
    by_chunk = pltpu.einshape("rjl->jrl", ref[...])
    return jnp.concatenate([by_chunk[j] for j in range(ref.shape[1])], axis=1)


def _norm_inproj_kernel(x_ref, g_ref, w_ref, cs_ref, wvt_ref, o_ref, vt_ref, hn_ref, *, n_col_tiles):
    j = pl.program_id(1)

    @pl.when(j == 0)
    def _():
        hn_ref[...] = _rms(x_ref[...], g_ref[...]).astype(hn_ref.dtype)

    @pl.when(j < n_col_tiles)
    def _():
        o_ref[...] = (_dot(hn_ref[...], w_ref[...]) * cs_ref[...]).astype(o_ref.dtype)

    @pl.when(j == n_col_tiles)
    def _():
        vt = _dot_nt(wvt_ref[...], hn_ref[...]).astype(vt_ref.dtype)
        kb = vt_ref.shape[2]
        for c in range(vt_ref.shape[0]):
            vt_ref[c] = vt[:, c * kb:(c + 1) * kb]


def _norm_inproj(h, g, w, col_scale, wvt, *, key_block, tm=1024, tn=1024):
    t, d = h.shape
    n = w.shape[1]
    vw = wvt.shape[0]
    n_col_tiles = n // tn
    last = n_col_tiles - 1
    blocks = _nbytes((tm, d), F32) + _nbytes((d, tn), BF16) + _nbytes((tm, tn), BF16) + _nbytes((vw, tm), BF16)
    temps = _nbytes((tm, d), BF16) + 2 * _nbytes((tm, d), F32) + _nbytes((tm, tn), F32) + _nbytes((vw, d), BF16)
    return pl.pallas_call(
        functools.partial(_norm_inproj_kernel, n_col_tiles=n_col_tiles),
        grid=(t // tm, n_col_tiles + 1),
        in_specs=[
            pl.BlockSpec((tm, d), lambda i, j: (i, 0)),
            _resident((1, d), 2),
            pl.BlockSpec((d, tn), lambda i, j: (0, jnp.minimum(j, last))),
            pl.BlockSpec((1, tn), lambda i, j: (0, jnp.minimum(j, last))),
            _resident((vw, d), 2),
        ],
        out_specs=[
            pl.BlockSpec((tm, tn), lambda i, j: (i, jnp.minimum(j, last))),
            pl.BlockSpec((tm // key_block, vw, key_block), lambda i, j: (i, 0, 0)),
        ],
        out_shape=[jax.ShapeDtypeStruct((t, n), BF16), jax.ShapeDtypeStruct((t // key_block, vw, key_block), BF16)],
        scratch_shapes=[pltpu.VMEM((tm, d), BF16)],
        compiler_params=pltpu.CompilerParams(
            dimension_semantics=("arbitrary", "arbitrary"),
            vmem_limit_bytes=_vmem_limit(blocks, temps),
        ),
        name="norm_inproj",
    )(h, g, w, col_scale, wvt)


def _sb_attn_kernel(q_ref, k_ref, vt_ref, lo_ref, o_ref, acc_ref, run_ref, *, blk, heads):
    qi = pl.program_id(2)
    lo = lo_ref[...]
    acc_ref[...] = jnp.zeros_like(acc_ref)
    run_ref[...] = jnp.zeros_like(run_ref)

    def block(j, masked):
        start = pl.multiple_of(j * blk, blk)
        cols = [slice(h * HEAD_DIM, (h + 1) * HEAD_DIM) for h in range(heads)]
        if masked:
            causal = lax.broadcasted_iota(I32, (blk, blk), 0) < lax.broadcasted_iota(I32, (blk, blk), 1)
        zs = [_dot_nt(k_ref[0, pl.ds(start, blk), c], q_ref[0, :, c]) for c in cols]
        sps = [_softplus2(z) for z in zs]
        keeps = [jnp.where(causal, sp, 0.0) if masked else sp for sp in sps]
        css = [_dot(lo, keep.astype(BF16)) for keep in keeps]
        for h in range(heads):
            run = run_ref[h]
            run_all = jnp.concatenate([run] * (blk // V7X_SUBLANES), axis=0)
            a = jnp.exp2(zs[h] - sps[h] + css[h][:blk] + run_all)
            if masked:
                a = jnp.where(causal, a, 0.0)
            acc_ref[h] += _dot(vt_ref[j, cols[h], :], a.astype(BF16))
            run_ref[h] = run + css[h][blk:]

    block(qi, True)

    def body(n, carry):
        block(qi - 1 - n, False)
        return carry

    lax.fori_loop(0, qi, body, 0)
    for h in range(heads):
        o_ref[0, :, h * HEAD_DIM:(h + 1) * HEAD_DIM] = acc_ref[h].T.astype(o_ref.dtype)


def _sb_attention(proj3, vt, *, conv_width, sb_width, blk, heads=8):
    b, s, _ = proj3.shape
    gw = heads * HEAD_DIM
    groups = sb_width // gw
    q_off = 3 * conv_width // gw
    k_off = q_off + groups
    later = lax.broadcasted_iota(I32, (blk, blk), 1) > lax.broadcasted_iota(I32, (blk, blk), 0)
    lo = -jnp.concatenate([later.astype(BF16), jnp.ones((V7X_SUBLANES, blk), BF16)], axis=0)
    blocks = 2 * _nbytes((blk, gw), BF16) + 2 * _nbytes((s, gw), BF16)
    temps = heads * 8 * _nbytes((blk, blk), F32)
    return pl.pallas_call(
        functools.partial(_sb_attn_kernel, blk=blk, heads=heads),
        grid=(b, groups, s // blk),
        in_specs=[
            pl.BlockSpec((1, blk, gw), lambda bi, gi, qi: (bi, qi, q_off + gi)),
            pl.BlockSpec((1, s, gw), lambda bi, gi, qi: (bi, 0, k_off + gi)),
            pl.BlockSpec((s // blk, gw, blk), lambda bi, gi, qi: (bi, gi, 0)),
            _resident((blk + V7X_SUBLANES, blk), 3),
        ],
        out_specs=pl.BlockSpec((1, blk, gw), lambda bi, gi, qi: (bi, qi, gi)),
        out_shape=jax.ShapeDtypeStruct((b, s, sb_width), BF16),
        scratch_shapes=[pltpu.VMEM((heads, HEAD_DIM, blk), F32), pltpu.VMEM((heads, V7X_SUBLANES, blk), F32)],
        compiler_params=pltpu.CompilerParams(
            dimension_semantics=("arbitrary", "arbitrary", "arbitrary"),
            vmem_limit_bytes=_vmem_limit(blocks, temps),
        ),
        name="sb_attn",
    )(proj3, proj3, vt, lo)


def _rank_of(vals):
    n = vals.shape[0]
    idx = lax.broadcasted_iota(I32, vals.shape, 0)
    rank = jnp.zeros(vals.shape, F32)
    for other in range(n):
        o = vals[other:other + 1, :]
        rank = rank + jnp.where(o > vals, 1.0, 0.0) + jnp.where(o == vals, jnp.where(idx > other, 1.0, 0.0), 0.0)
    return rank


def _router(logits_t, bias):
    n_exp, tm = logits_t.shape
    per_group = n_exp // N_GROUPS
    scores = _sigmoid(logits_t)
    biased = scores + bias

    grouped = biased.reshape(N_GROUPS, per_group, tm)
    top1 = jnp.max(grouped, axis=1, keepdims=True)
    n_top = jnp.sum(jnp.where(grouped == top1, 1.0, 0.0), axis=1, keepdims=True)
    below = jnp.max(jnp.where(grouped < top1, grouped, -jnp.inf), axis=1, keepdims=True)
    group_score = (top1 + jnp.where(n_top >= 2.0, top1, below)).reshape(N_GROUPS, tm)

    group_ok = _rank_of(group_score) < float(TOPK_GROUPS)
    expert_ok = jnp.broadcast_to(group_ok.reshape(N_GROUPS, 1, tm), (N_GROUPS, per_group, tm)).reshape(n_exp, tm)
    chosen = jnp.where(_rank_of(jnp.where(expert_ok, biased, -jnp.inf)) < float(TOP_K), 1.0, 0.0)
    w = chosen * scores
    return chosen, w / jnp.sum(w, axis=0, keepdims=True) * ROUTED_SCALE


def _mixer_out_kernel(ch_ref, cb_ref, cc_ref, hh_ref, hc_ref, sb_ref, x_ref, cw_ref, gc_ref, gs_ref,
                      wo_ref, gf_ref, wr_ref, rb_ref, tri_ref, low_ref,
                      h1_ref, xt_ref, eid_ref, rank_ref, gate_ref, cnt_ref, *, tiles_per_seq):
    i = pl.program_id(0)
    u = cc_ref[...].astype(F32) * ch_ref[...].astype(F32)
    halo = hc_ref[...].astype(F32) * hh_ref[...].astype(F32)
    halo = jnp.where(i % tiles_per_seq == 0, 0.0, halo)
    prev1 = halo[V7X_SUBLANES - 1:V7X_SUBLANES, :]
    prev2 = halo[V7X_SUBLANES - 2:V7X_SUBLANES - 1, :]
    row = lax.broadcasted_iota(I32, u.shape, 0)
    u1 = jnp.where(row == 0, prev1, pltpu.roll(u, 1, 0))
    u2 = jnp.where(row == 0, prev2, jnp.where(row == 1, prev1, pltpu.roll(u, 2, 0)))
    cw = cw_ref[...]
    conv = cb_ref[...].astype(F32) * (cw[0:1, :] * u2 + cw[1:2, :] * u1 + cw[2:3, :] * u)

    y = jnp.concatenate([_rms(conv, gc_ref[...]), _rms(sb_ref[...].astype(F32), gs_ref[...])], axis=-1)
    h1 = x_ref[...] + _dot(y.astype(BF16), wo_ref[...])
    h1_ref[...] = h1
    xt = _rms(h1, gf_ref[...])
    _store_slabs(xt_ref, _pack_rows(xt))

    n_exp = wr_ref.shape[1] // 2
    xt_hi = xt.astype(BF16)
    xt_lo = (xt - xt_hi.astype(F32)).astype(BF16)
    by_hi = _dot(xt_hi, wr_ref[...])
    logits = by_hi[:, :n_exp] + by_hi[:, n_exp:] + _dot(xt_lo, wr_ref[:, :n_exp])
    chosen, gates = _router(logits.T, rb_ref[...])
    tm = chosen.shape[1]

    @pl.when(i == 0)
    def _():
        cnt_ref[...] = jnp.zeros_like(cnt_ref)

    counts = _dot(chosen.astype(BF16), tri_ref[...])
    seen = cnt_ref[...]
    rank = seen + counts[:, :tm]
    cnt_ref[...] = seen + counts[:, tm:]

    slot = _dot(low_ref[...], chosen.astype(BF16))
    expert = lax.broadcasted_iota(I32, chosen.shape, 0).astype(F32)
    eids, ranks, gsel = [], [], []
    for k in range(TOP_K):
        pick = chosen * jnp.where(slot == float(k), 1.0, 0.0)
        eids.append(jnp.sum(pick * expert, axis=0, keepdims=True))
        ranks.append(jnp.sum(pick * rank, axis=0, keepdims=True))
        gsel.append(jnp.sum(pick * gates, axis=0, keepdims=True))
    eid_ref[...] = jnp.concatenate(eids, axis=0).astype(I32)
    rank_ref[...] = jnp.concatenate(ranks, axis=0).astype(I32)
    gate_ref[...] = jnp.concatenate(gsel, axis=0).T


def _mixer_out(proj, sb, h, conv_w, gc, gs, w_out, gf, w_router, router_bias, *, seq, conv_width, tm=256):
    t, d = h.shape
    sb_width = sb.shape[1]
    n_exp = w_router.shape[1]
    router_hi = w_router.astype(BF16)
    router_lo = (w_router - router_hi.astype(F32)).astype(BF16)
    w_router = jnp.concatenate([router_hi, router_lo], axis=1)
    halo_blocks = tm // V7X_SUBLANES
    conv_spec = lambda c: pl.BlockSpec((tm, conv_width), lambda i: (i, c))
    halo_spec = lambda c: pl.BlockSpec((V7X_SUBLANES, conv_width), lambda i: (jnp.maximum(i * halo_blocks - 1, 0), c))
    earlier = lax.broadcasted_iota(I32, (tm, tm), 0) < lax.broadcasted_iota(I32, (tm, tm), 1)
    tri = jnp.concatenate([earlier.astype(BF16), jnp.ones((tm, tm), BF16)], axis=1)
    lower = (lax.broadcasted_iota(I32, (n_exp, n_exp), 1) < lax.broadcasted_iota(I32, (n_exp, n_exp), 0)).astype(BF16)
    blocks = (3 * _nbytes((tm, conv_width), BF16) + _nbytes((tm, sb_width), BF16) + 3 * _nbytes((tm, d), F32))
    temps = 10 * _nbytes((tm, d), F32) + _nbytes(w_out.shape, BF16) + _nbytes(w_router.shape, BF16)
    row_block = pl.BlockSpec((tm, d), lambda i: (i, 0))
    slot_block = pl.BlockSpec((TOP_K, tm), lambda i: (0, i))
    return pl.pallas_call(
        functools.partial(_mixer_out_kernel, tiles_per_seq=seq // tm),
        grid=(t // tm,),
        in_specs=[
            conv_spec(0), conv_spec(1), conv_spec(2), halo_spec(0), halo_spec(2),
            pl.BlockSpec((tm, sb_width), lambda i: (i, 0)),
            row_block,
            _resident(conv_w.shape, 1), _resident(gc.shape, 1), _resident(gs.shape, 1), _resident(w_out.shape, 1),
            _resident(gf.shape, 1), _resident(w_router.shape, 1), _resident(router_bias.shape, 1),
            _resident(tri.shape, 1), _resident(lower.shape, 1),
        ],
        out_specs=[
            row_block, pl.BlockSpec((tm,) + _slab_shape(d // 2), lambda i: (i, 0, 0)), slot_block, slot_block,
            pl.BlockSpec((tm, TOP_K), lambda i: (i, 0)),
            pl.BlockSpec((n_exp, tm), lambda i: (0, 0)),
        ],
        out_shape=[
            jax.ShapeDtypeStruct((t, d), F32),
            jax.ShapeDtypeStruct((t,) + _slab_shape(d // 2), U32),
            jax.ShapeDtypeStruct((TOP_K, t), I32),
            jax.ShapeDtypeStruct((TOP_K, t), I32),
            jax.ShapeDtypeStruct((t, TOP_K), F32),
            jax.ShapeDtypeStruct((n_exp, tm), F32),
        ],
        compiler_params=pltpu.CompilerParams(
            dimension_semantics=("arbitrary",),
            vmem_limit_bytes=_vmem_limit(blocks, temps),
        ),
        name="mixer_out",
    )(proj, proj, proj, proj, proj, sb, h, conv_w, gc, gs, w_out, gf, w_router, router_bias, tri, lower)


def _moe_layout_kernel(cnt_ref, eid_ref, rank_ref, pos_ref, off_ref, texp_ref, nused_ref, *, n_exp, n_tiles):
    def layout(e, off):
        off_ref[e] = off
        tiles = (cnt_ref[e] + EXPERT_TILE - 1) >> EXPERT_TILE_LOG2
        first = off >> EXPERT_TILE_LOG2

        def mark(j, c):
            texp_ref[first + j] = e
            return c

        lax.fori_loop(0, tiles, mark, 0)
        return off + (tiles << EXPERT_TILE_LOG2)

    total = lax.fori_loop(0, n_exp, layout, 0)
    off_ref[n_exp] = total
    used_tiles = total >> EXPERT_TILE_LOG2
    nused_ref[0] = used_tiles

    def tail(j, c):
        texp_ref[j] = n_exp - 1
        return c

    lax.fori_loop(used_tiles, n_tiles, tail, 0)

    eid = eid_ref[...]
    pos = rank_ref[...]
    for e in range(n_exp):
        pos = pos + jnp.where(eid == e, off_ref[e], 0)
    pos_ref[...] = pos


def _moe_layout(counts, eid, rank):
    n_exp = counts.shape[0]
    t = eid.shape[1]
    n_tiles = t * TOP_K // EXPERT_TILE + n_exp
    whole = lambda shape: pl.BlockSpec(shape, lambda i, cnt: (0,) * len(shape))
    whole_smem = lambda n: pl.BlockSpec((n,), lambda i, cnt: (0,), memory_space=pltpu.SMEM)
    return pl.pallas_call(
        functools.partial(_moe_layout_kernel, n_exp=n_exp, n_tiles=n_tiles),
        grid_spec=pltpu.PrefetchScalarGridSpec(
            num_scalar_prefetch=1,
            grid=(1,),
            in_specs=[whole(eid.shape), whole(rank.shape)],
            out_specs=[whole(eid.shape), whole_smem(n_exp + 1), whole_smem(n_tiles), whole_smem(1)],
        ),
        out_shape=[
            jax.ShapeDtypeStruct(eid.shape, I32),
            jax.ShapeDtypeStruct((n_exp + 1,), I32),
            jax.ShapeDtypeStruct((n_tiles,), I32),
            jax.ShapeDtypeStruct((1,), I32),
        ],
        compiler_params=pltpu.CompilerParams(
            dimension_semantics=("arbitrary",),
            vmem_limit_bytes=_vmem_limit(3 * _nbytes(eid.shape, I32), 4 * _nbytes(eid.shape, I32)),
        ),
        name="moe_layout",
    )(counts, eid, rank)


def _moe_dispatch_kernel(cnt_ref, off_ref, pos_ref, xt_ref, xs_hbm, zero_ref, row_sem, pad_sem, *, n_exp, n_assigned):
    ts = xt_ref.shape[0]

    @pl.when(pl.program_id(0) == 0)
    def _():
        zero_ref[...] = jnp.zeros_like(zero_ref)

        def pad(e, c):
            def zero_row(r, c):
                pltpu.make_async_copy(zero_ref.at[0], xs_hbm.at[r], pad_sem).start()
                return c

            lax.fori_loop(off_ref[e] + cnt_ref[e], off_ref[e + 1], zero_row, 0)
            return c

        def drain_row(r, c):
            pltpu.make_async_copy(zero_ref.at[0], xs_hbm.at[0], pad_sem).wait()
            return c

        lax.fori_loop(0, n_exp, pad, 0)
        lax.fori_loop(n_assigned, off_ref[n_exp], drain_row, 0)

    def token_group(g, c):
        base = pl.multiple_of(g * V7X_SUBLANES, V7X_SUBLANES)
        for u in range(V7X_SUBLANES):
            t = base + u
            for k in range(TOP_K):
                pltpu.make_async_copy(xt_ref.at[t], xs_hbm.at[pos_ref[k, t]], row_sem).start(priority=k % DMA_THREADS)
        return c

    lax.fori_loop(0, ts // V7X_SUBLANES, token_group, 0)
    for k in range(TOP_K):
        pltpu.make_async_copy(xt_ref, xs_hbm.at[pl.ds(0, ts)], row_sem).wait()


def _moe_dispatch(counts, off, pos, xt, *, ts=512):
    t = xt.shape[0]
    slab = xt.shape[1:]
    n_exp = counts.shape[0]
    n_tiles = t * TOP_K // EXPERT_TILE + n_exp
    blocks = _nbytes((ts,) + slab, xt.dtype)
    temps = _nbytes((1,) + slab, xt.dtype)
    return pl.pallas_call(
        functools.partial(_moe_dispatch_kernel, n_exp=n_exp, n_assigned=t * TOP_K),
        grid_spec=pltpu.PrefetchScalarGridSpec(
            num_scalar_prefetch=2,
            grid=(t // ts,),
            in_specs=[
                pl.BlockSpec((TOP_K, ts), lambda i, cnt, off: (0, i), memory_space=pltpu.SMEM),
                pl.BlockSpec((ts,) + slab, lambda i, cnt, off: (i, 0, 0)),
            ],
            out_specs=pl.BlockSpec(memory_space=pl.ANY),
            scratch_shapes=[
                pltpu.VMEM((1,) + slab, xt.dtype),
                pltpu.SemaphoreType.DMA(()),
                pltpu.SemaphoreType.DMA(()),
            ],
        ),
        out_shape=jax.ShapeDtypeStruct((n_tiles * EXPERT_TILE,) + slab, xt.dtype),
        compiler_params=pltpu.CompilerParams(
            dimension_semantics=("arbitrary",),
            vmem_limit_bytes=_vmem_limit(blocks, temps),
        ),
        name="moe_dispatch",
    )(counts, off, pos, xt)


def _moe_experts_kernel(texp_ref, nused_ref, xs_ref, wg_ref, wu_ref, wd_ref, ys_ref, wg_bf, wu_bf, wd_bf):
    i = pl.program_id(0)

    @pl.when((i == 0) | (texp_ref[i] != texp_ref[jnp.maximum(i - 1, 0)]))
    def _():
        wg_bf[...] = wg_ref[0].astype(BF16)
        wu_bf[...] = wu_ref[0].astype(BF16)
        wd_bf[...] = wd_ref[0].astype(BF16)

    @pl.when(i < nused_ref[0])
    def _():
        hid = _swiglu_hidden(_unpack_rows(_load_slabs(xs_ref)).astype(BF16), wg_bf[...], wu_bf[...])
        _store_slabs(ys_ref, _pack_rows(_dot(hid.astype(BF16), wd_bf[...])))

    @pl.when(i >= nused_ref[0])
    def _():
        ys_ref[...] = jnp.zeros_like(ys_ref)


def _moe_experts(texp, nused, xs, wg, wu, wd):
    rows = xs.shape[0]
    slab = xs.shape[1:]
    d, f = wg.shape[1:]
    n_tiles = rows // EXPERT_TILE
    blocks = 2 * _nbytes((EXPERT_TILE,) + slab, U32) + 3 * _nbytes((d, f), wg.dtype)
    temps = (3 * _nbytes((d, f), BF16) + 4 * _nbytes((EXPERT_TILE, f), F32) + _nbytes((EXPERT_TILE, d), F32)
             + _nbytes((EXPERT_TILE, d), BF16))
    expert_block = lambda shape: pl.BlockSpec(shape, lambda i, texp, nused: (texp[i], 0, 0))
    return pl.pallas_call(
        _moe_experts_kernel,
        grid_spec=pltpu.PrefetchScalarGridSpec(
            num_scalar_prefetch=2,
            grid=(n_tiles,),
            in_specs=[
                pl.BlockSpec((EXPERT_TILE,) + slab, lambda i, texp, nused: (jnp.minimum(i, nused[0] - 1), 0, 0)),
                expert_block((1, d, f)), expert_block((1, d, f)), expert_block((1, f, d)),
            ],
            out_specs=pl.BlockSpec((EXPERT_TILE,) + slab, lambda i, texp, nused: (i, 0, 0)),
            scratch_shapes=[pltpu.VMEM((d, f), BF16), pltpu.VMEM((d, f), BF16), pltpu.VMEM((f, d), BF16)],
        ),
        out_shape=jax.ShapeDtypeStruct((rows,) + slab, U32),
        compiler_params=pltpu.CompilerParams(
            dimension_semantics=("arbitrary",),
            vmem_limit_bytes=_vmem_limit(blocks, temps),
        ),
        name="moe_experts",
    )(texp, nused, xs, wg, wu, wd)


GATHER_DEPTH = 3


def _moe_combine_kernel(pos_ref, pos_1_ref, pos_2_ref, gate_ref, h1_ref, xt_ref, p_ref, sg_ref, su_ref, sd_ref,
                        gp_ref, wg_ref, wp_ref, gf_ref, ys_hbm, o_ref, buf_0, buf_1, buf_2, sem, *, final_norm):
    tc = h1_ref.shape[0]
    i = pl.program_id(0)
    bufs = (buf_0, buf_1, buf_2)

    def wait_rows(b):
        for k in range(TOP_K):
            pltpu.make_async_copy(ys_hbm.at[pl.ds(0, tc)], bufs[b].at[k], sem.at[b]).wait()

    @pl.when(i == 0)
    def _():
        def token_group(g, c):
            base = pl.multiple_of(g * V7X_SUBLANES, V7X_SUBLANES)
            for u in range(V7X_SUBLANES):
                for k in range(TOP_K):
                    t = base + u
                    pltpu.make_async_copy(ys_hbm.at[pos_ref[k, t]], buf_0.at[k, t], sem.at[0]).start()
                    pltpu.make_async_copy(ys_hbm.at[pos_1_ref[k, t]], buf_1.at[k, t], sem.at[1]).start()
            return c

        lax.fori_loop(0, tc // V7X_SUBLANES, token_group, 0)

    def step(cur):
        ahead = (cur + 2) % GATHER_DEPTH
        wait_rows(cur)
        for t in range(tc):
            for k in range(TOP_K):
                pltpu.make_async_copy(ys_hbm.at[pos_2_ref[k, t]], bufs[ahead].at[k, t],
                                      sem.at[ahead]).start(priority=k % DMA_THREADS)

        hs = _swiglu_hidden(_unpack_rows(_load_slabs(xt_ref)).astype(BF16), sg_ref[...], su_ref[...])
        h2 = h1_ref[...] + _dot(hs.astype(BF16), sd_ref[...])
        emb = _dot(p_ref[...].astype(BF16), wp_ref[...])
        gate = gate_ref[...]
        for k in range(TOP_K):
            h2 = h2 + gate[:, k:k + 1] * _unpack_rows(_load_slabs(bufs[cur].at[k]))

        gate_ple = _sigmoid(_dot(_rms(h2, gp_ref[...]).astype(BF16), wg_ref[...]))
        h3 = h2 + gate_ple * emb
        o_ref[...] = _rms(h3, gf_ref[...]) if final_norm else h3

        @pl.when(i + 1 == pl.num_programs(0))
        def _():
            wait_rows((cur + 1) % GATHER_DEPTH)
            wait_rows(ahead)

    for b in range(GATHER_DEPTH):
        @pl.when(i % GATHER_DEPTH == b)
        def _(b=b):
            step(b)


def _moe_combine(pos, gate, h1, xt, p, sg, su, sd, gp, wg, wp, gf, ys, *, final_norm, tc=128):
    t, d = h1.shape
    pd = p.shape[1]
    slab = ys.shape[1:]
    row_block = pl.BlockSpec((tc, d), lambda i: (i, 0))
    blocks = (2 * _nbytes((tc, d), F32) + _nbytes((tc,) + slab, U32) + _nbytes((tc, pd), F32)
              + _nbytes((tc, V7X_LANES), F32))
    temps = (GATHER_DEPTH * _nbytes((TOP_K, tc) + slab, U32) + 8 * _nbytes((tc, d), F32) + _nbytes(wg.shape, BF16)
             + _nbytes(wp.shape, BF16) + 3 * _nbytes(sg.shape, BF16))
    last = t // tc - 1
    slots_of_step = lambda j: pl.BlockSpec((TOP_K, tc), lambda i: (0, jnp.minimum(i + j, last)),
                                           memory_space=pltpu.SMEM)
    return pl.pallas_call(
        functools.partial(_moe_combine_kernel, final_norm=final_norm),
        grid=(t // tc,),
        in_specs=[
            slots_of_step(0), slots_of_step(1), slots_of_step(2),
            pl.BlockSpec((tc, TOP_K), lambda i: (i, 0)),
            row_block, pl.BlockSpec((tc,) + slab, lambda i: (i, 0, 0)),
            pl.BlockSpec((tc, pd), lambda i: (i, 0)),
            _resident(sg.shape, 1), _resident(su.shape, 1), _resident(sd.shape, 1), _resident(gp.shape, 1),
            _resident(wg.shape, 1), _resident(wp.shape, 1), _resident(gf.shape, 1),
            pl.BlockSpec(memory_space=pl.ANY),
        ],
        out_specs=row_block,
        out_shape=jax.ShapeDtypeStruct((t, d), F32),
        scratch_shapes=[pltpu.VMEM((TOP_K, tc) + slab, U32)] * GATHER_DEPTH
                       + [pltpu.SemaphoreType.DMA((GATHER_DEPTH,))],
        compiler_params=pltpu.CompilerParams(
            dimension_semantics=("arbitrary",),
            vmem_limit_bytes=_vmem_limit(blocks, temps),
        ),
        name="moe_combine",
    )(pos, pos, pos, gate, h1, xt, p, sg, su, sd, gp, wg, wp, gf, ys)


def kernel(x, p, norm_mix_g, w_in, conv_w, gnorm_conv_g, gnorm_sb_g, w_out, norm_ffn_g, w_router, router_bias,
           w_exp_gate, w_exp_up, w_exp_down, w_sh_gate, w_sh_up, w_sh_down, norm_ple_g, w_ple_gate, w_ple_proj,
           norm_final_g):
    bsz, seq, d = x.shape
    depth = p.shape[0]
    t = bsz * seq
    conv_width = conv_w.shape[-1]
    sb_width = gnorm_sb_g.shape[-1]
    n_qk = 3 * conv_width + 2 * sb_width
    assert w_in.shape[-1] == n_qk + sb_width
    assert w_router.shape[-1] % N_GROUPS == 0

    q_lo = 3 * conv_width
    col = jnp.arange(n_qk)
    q_scale = HEAD_DIM ** -0.5 * LOG2_E
    col_scale = jnp.where((col >= q_lo) & (col < q_lo + sb_width), q_scale, 1.0).astype(F32)[None, :]
    row = lambda v: v.astype(F32)[None, :]

    h = x.reshape(t, d)
    for i in range(depth):
        w_in_bf = w_in[i].astype(BF16)
        proj, vt = _norm_inproj(h, row(norm_mix_g[i]), w_in_bf[:, :n_qk], col_scale, w_in_bf[:, n_qk:].T,
                                key_block=ATTN_BLOCK)
        sb = _sb_attention(proj.reshape(bsz, seq, n_qk), vt, conv_width=conv_width, sb_width=sb_width,
                           blk=ATTN_BLOCK)
        h1, xt, eid, rank, gate, cnt = _mixer_out(
            proj, sb.reshape(t, sb_width), h, conv_w[i], row(gnorm_conv_g[i]), row(gnorm_sb_g[i]),
            w_out[i].astype(BF16), row(norm_ffn_g[i]), w_router[i], router_bias[i].astype(F32)[:, None],
            seq=seq, conv_width=conv_width)
        counts = cnt[:, 0].astype(I32)
        pos, off, texp, nused = _moe_layout(counts, eid, rank)
        xs = _moe_dispatch(counts, off, pos, xt)
        ys = _moe_experts(texp, nused, xs, w_exp_gate[i], w_exp_up[i], w_exp_down[i])
        h = _moe_combine(pos, gate, h1, xt, p[i].reshape(t, -1), w_sh_gate[i].astype(BF16),
                         w_sh_up[i].astype(BF16), w_sh_down[i].astype(BF16), row(norm_ple_g[i]),
                         w_ple_gate[i].astype(BF16), w_ple_proj[i].astype(BF16), row(norm_final_g), ys,
                         final_norm=(i == depth - 1))
    return h.reshape(bsz, seq, d)
```

```python
import functools

import jax
import jax.numpy as jnp
from jax import lax
from jax.experimental import pallas as pl
from jax.experimental.pallas import tpu as pltpu

NORM_EPS = 1e-6
HEAD_DIM = 128
CONV_KERNEL = 3
N_GROUPS = 8
TOPK_GROUPS = 4
TOP_K = 8
ROUTED_SCALE = 2.5
LOG2_E = 1.4426950408889634
F32_EXP2_UNDERFLOW = -160.0

V7X_VMEM_BYTES = 64 * 1024 * 1024
V7X_SUBLANES = 8
V7X_LANES = 128
DMA_THREADS = 2

F32 = jnp.float32
BF16 = jnp.bfloat16
I32 = jnp.int32
U32 = jnp.uint32

ATTN_BLOCK = 256
EXPERT_TILE = 512
EXPERT_TILE_LOG2 = 9
assert 1 << EXPERT_TILE_LOG2 == EXPERT_TILE


def _vmem_limit(pipelined_bytes, resident_bytes):
    want = 2 * pipelined_bytes + resident_bytes
    return int(min(want, V7X_VMEM_BYTES - 4 * 1024 * 1024))


def _nbytes(shape, dtype):
    n = 1
    for s in shape:
        n *= s
    return n * jnp.dtype(dtype).itemsize


def _resident(shape, n_grid_axes, n_prefetch=0):
    zeros = (0,) * len(shape)
    return pl.BlockSpec(shape, lambda *_: zeros, pipeline_mode=pl.Buffered(1))


def _rms(y, g):
    return y * lax.rsqrt(jnp.mean(y * y, axis=-1, keepdims=True) + NORM_EPS) * g


def _sigmoid(a):
    return 1.0 / (1.0 + jnp.exp(-a))


def _softplus2(z):
    return jnp.maximum(z, 0.0) + jnp.log2(1.0 + jnp.exp2(-jnp.abs(z)))


def _dot(a, b):
    return jnp.dot(a, b, preferred_element_type=F32)


def _dot_nt(a, b):
    return lax.dot_general(a, b, (((1,), (1,)), ((), ())), preferred_element_type=F32)


def _swiglu_hidden(x, wg, wu):
    a = _dot(x, wg)
    return a * _sigmoid(a) * _dot(x, wu)


def _pack_rows(x):
    half = x.shape[1] // 2
    as_bits = lambda v: lax.bitcast_convert_type(v.astype(BF16).astype(F32), U32)
    return (as_bits(x[:, :half]) >> 16) | (as_bits(x[:, half:]) & jnp.uint32(0xFFFF0000))


def _unpack_rows(w):
    lo = lax.bitcast_convert_type(w << 16, F32)
    hi = lax.bitcast_convert_type(w & jnp.uint32(0xFFFF0000), F32)
    return jnp.concatenate([lo, hi], axis=1)


def _slab_shape(words):
    return (words // V7X_LANES, V7X_LANES)


def _store_slabs(ref, words):
    by_chunk = jnp.stack([words[:, j * V7X_LANES:(j + 1) * V7X_LANES] for j in range(ref.shape[1])], axis=0)
    ref[...] = pltpu.einshape("jrl->rjl", by_chunk)


def _load_slabs(ref):
    by_chunk = pltpu.einshape("rjl->jrl", ref[...])
    return jnp.concatenate([by_chunk[j] for j in range(ref.shape[1])], axis=1)


def _norm_inproj_kernel(x_ref, g_ref, w_ref, cs_ref, wvt_ref, o_ref, vt_ref, hn_ref, *, n_col_tiles):
    j = pl.program_id(1)

    @pl.when(j == 0)
    def _():
        hn_ref[...] = _rms(x_ref[...], g_ref[...]).astype(hn_ref.dtype)

    @pl.when(j < n_col_tiles)
    def _():
        o_ref[...] = (_dot(hn_ref[...], w_ref[...]) * cs_ref[...]).astype(o_ref.dtype)

    @pl.when(j == n_col_tiles)
    def _():
        vt = _dot_nt(wvt_ref[...], hn_ref[...]).astype(vt_ref.dtype)
        kb = vt_ref.shape[2]
        for c in range(vt_ref.shape[0]):
            vt_ref[c] = vt[:, c * kb:(c + 1) * kb]


def _norm_inproj(h, g, w, col_scale, wvt, *, key_block, tm=1024, tn=1024):
    t, d = h.shape
    n = w.shape[1]
    vw = wvt.shape[0]
    n_col_tiles = n // tn
    last = n_col_tiles - 1
    blocks = _nbytes((tm, d), F32) + _nbytes((d, tn), BF16) + _nbytes((tm, tn), BF16) + _nbytes((vw, tm), BF16)
    temps = _nbytes((tm, d), BF16) + 2 * _nbytes((tm, d), F32) + _nbytes((tm, tn), F32) + _nbytes((vw, d), BF16)
    return pl.pallas_call(
        functools.partial(_norm_inproj_kernel, n_col_tiles=n_col_tiles),
        grid=(t // tm, n_col_tiles + 1),
        in_specs=[
            pl.BlockSpec((tm, d), lambda i, j: (i, 0)),
            _resident((1, d), 2),
            pl.BlockSpec((d, tn), lambda i, j: (0, jnp.minimum(j, last))),
            pl.BlockSpec((1, tn), lambda i, j: (0, jnp.minimum(j, last))),
            _resident((vw, d), 2),
        ],
        out_specs=[
            pl.BlockSpec((tm, tn), lambda i, j: (i, jnp.minimum(j, last))),
            pl.BlockSpec((tm // key_block, vw, key_block), lambda i, j: (i, 0, 0)),
        ],
        out_shape=[jax.ShapeDtypeStruct((t, n), BF16), jax.ShapeDtypeStruct((t // key_block, vw, key_block), BF16)],
        scratch_shapes=[pltpu.VMEM((tm, d), BF16)],
        compiler_params=pltpu.CompilerParams(
            dimension_semantics=("arbitrary", "arbitrary"),
            vmem_limit_bytes=_vmem_limit(blocks, temps),
        ),
        name="norm_inproj",
    )(h, g, w, col_scale, wvt)


def _sb_attn_kernel(q_ref, k_ref, vt_ref, lo_ref, o_ref, acc_ref, run_ref, *, blk, heads):
    qi = pl.program_id(2)
    lo = lo_ref[...]
    acc_ref[...] = jnp.zeros_like(acc_ref)
    run_ref[...] = jnp.zeros_like(run_ref)

    def block(j, masked):
        start = pl.multiple_of(j * blk, blk)
        cols = [slice(h * HEAD_DIM, (h + 1) * HEAD_DIM) for h in range(heads)]
        if masked:
            causal = lax.broadcasted_iota(I32, (blk, blk), 0) < lax.broadcasted_iota(I32, (blk, blk), 1)
        zs = [_dot_nt(k_ref[0, pl.ds(start, blk), c], q_ref[0, :, c]) for c in cols]
        sps = [_softplus2(z) for z in zs]
        keeps = [jnp.where(causal, sp, 0.0) if masked else sp for sp in sps]
        css = [_dot(lo, keep.astype(BF16)) for keep in keeps]
        for h in range(heads):
            run = run_ref[h]
            run_all = jnp.concatenate([run] * (blk // V7X_SUBLANES), axis=0)
            a = jnp.exp2(zs[h] - sps[h] + css[h][:blk] + run_all)
            if masked:
                a = jnp.where(causal, a, 0.0)
            acc_ref[h] += _dot(vt_ref[j, cols[h], :], a.astype(BF16))
            run_ref[h] = run + css[h][blk:]

    block(qi, True)

    def still_visible():
        return jnp.max(run_ref[...]) > F32_EXP2_UNDERFLOW

    def body(carry):
        n, _ = carry
        block(qi - 1 - n, False)
        return n + 1, still_visible()

    lax.while_loop(lambda carry: (carry[0] < qi) & carry[1], body, (jnp.int32(0), still_visible()))
    for h in range(heads):
        o_ref[0, :, h * HEAD_DIM:(h + 1) * HEAD_DIM] = acc_ref[h].T.astype(o_ref.dtype)


def _sb_attention(proj3, vt, *, conv_width, sb_width, blk, heads=8):
    b, s, _ = proj3.shape
    gw = heads * HEAD_DIM
    groups = sb_width // gw
    q_off = 3 * conv_width // gw
    k_off = q_off + groups
    later = lax.broadcasted_iota(I32, (blk, blk), 1) > lax.broadcasted_iota(I32, (blk, blk), 0)
    lo = -jnp.concatenate([later.astype(BF16), jnp.ones((V7X_SUBLANES, blk), BF16)], axis=0)
    blocks = 2 * _nbytes((blk, gw), BF16) + 2 * _nbytes((s, gw), BF16)
    temps = heads * 8 * _nbytes((blk, blk), F32)
    return pl.pallas_call(
        functools.partial(_sb_attn_kernel, blk=blk, heads=heads),
        grid=(b, groups, s // blk),
        in_specs=[
            pl.BlockSpec((1, blk, gw), lambda bi, gi, qi: (bi, qi, q_off + gi)),
            pl.BlockSpec((1, s, gw), lambda bi, gi, qi: (bi, 0, k_off + gi)),
            pl.BlockSpec((s // blk, gw, blk), lambda bi, gi, qi: (bi, gi, 0)),
            _resident((blk + V7X_SUBLANES, blk), 3),
        ],
        out_specs=pl.BlockSpec((1, blk, gw), lambda bi, gi, qi: (bi, qi, gi)),
        out_shape=jax.ShapeDtypeStruct((b, s, sb_width), BF16),
        scratch_shapes=[pltpu.VMEM((heads, HEAD_DIM, blk), F32), pltpu.VMEM((heads, V7X_SUBLANES, blk), F32)],
        compiler_params=pltpu.CompilerParams(
            dimension_semantics=("arbitrary", "arbitrary", "arbitrary"),
            vmem_limit_bytes=_vmem_limit(blocks, temps),
        ),
        name="sb_attn",
    )(proj3, proj3, vt, lo)


def _rank_of(vals):
    n = vals.shape[0]
    idx = lax.broadcasted_iota(I32, vals.shape, 0)
    rank = jnp.zeros(vals.shape, F32)
    for other in range(n):
        o = vals[other:other + 1, :]
        rank = rank + jnp.where(o > vals, 1.0, 0.0) + jnp.where(o == vals, jnp.where(idx > other, 1.0, 0.0), 0.0)
    return rank


def _router(logits_t, bias):
    n_exp, tm = logits_t.shape
    per_group = n_exp // N_GROUPS
    scores = _sigmoid(logits_t)
    biased = scores + bias

    grouped = biased.reshape(N_GROUPS, per_group, tm)
    top1 = jnp.max(grouped, axis=1, keepdims=True)
    n_top = jnp.sum(jnp.where(grouped == top1, 1.0, 0.0), axis=1, keepdims=True)
    below = jnp.max(jnp.where(grouped < top1, grouped, -jnp.inf), axis=1, keepdims=True)
    group_score = (top1 + jnp.where(n_top >= 2.0, top1, below)).reshape(N_GROUPS, tm)

    group_ok = _rank_of(group_score) < float(TOPK_GROUPS)
    expert_ok = jnp.broadcast_to(group_ok.reshape(N_GROUPS, 1, tm), (N_GROUPS, per_group, tm)).reshape(n_exp, tm)
    chosen = jnp.where(_rank_of(jnp.where(expert_ok, biased, -jnp.inf)) < float(TOP_K), 1.0, 0.0)
    w = chosen * scores
    return chosen, w / jnp.sum(w, axis=0, keepdims=True) * ROUTED_SCALE


def _mixer_out_kernel(ch_ref, cb_ref, cc_ref, hh_ref, hc_ref, sb_ref, x_ref, cw_ref, gc_ref, gs_ref,
                      wo_ref, gf_ref, wr_ref, rb_ref, tri_ref, low_ref,
                      h1_ref, xt_ref, eid_ref, rank_ref, gate_ref, cnt_ref, *, tiles_per_seq):
    i = pl.program_id(0)
    u = cc_ref[...].astype(F32) * ch_ref[...].astype(F32)
    halo = hc_ref[...].astype(F32) * hh_ref[...].astype(F32)
    halo = jnp.where(i % tiles_per_seq == 0, 0.0, halo)
    prev1 = halo[V7X_SUBLANES - 1:V7X_SUBLANES, :]
    prev2 = halo[V7X_SUBLANES - 2:V7X_SUBLANES - 1, :]
    row = lax.broadcasted_iota(I32, u.shape, 0)
    u1 = jnp.where(row == 0, prev1, pltpu.roll(u, 1, 0))
    u2 = jnp.where(row == 0, prev2, jnp.where(row == 1, prev1, pltpu.roll(u, 2, 0)))
    cw = cw_ref[...]
    conv = cb_ref[...].astype(F32) * (cw[0:1, :] * u2 + cw[1:2, :] * u1 + cw[2:3, :] * u)

    y = jnp.concatenate([_rms(conv, gc_ref[...]), _rms(sb_ref[...].astype(F32), gs_ref[...])], axis=-1)
    h1 = x_ref[...] + _dot(y.astype(BF16), wo_ref[...])
    h1_ref[...] = h1
    xt = _rms(h1, gf_ref[...])
    _store_slabs(xt_ref, _pack_rows(xt))

    n_exp = wr_ref.shape[1] // 2
    xt_hi = xt.astype(BF16)
    xt_lo = (xt - xt_hi.astype(F32)).astype(BF16)
    by_hi = _dot(xt_hi, wr_ref[...])
    logits = by_hi[:, :n_exp] + by_hi[:, n_exp:] + _dot(xt_lo, wr_ref[:, :n_exp])
    chosen, gates = _router(logits.T, rb_ref[...])
    tm = chosen.shape[1]

    @pl.when(i == 0)
    def _():
        cnt_ref[...] = jnp.zeros_like(cnt_ref)

    counts = _dot(chosen.astype(BF16), tri_ref[...])
    seen = cnt_ref[...]
    rank = seen + counts[:, :tm]
    cnt_ref[...] = seen + counts[:, tm:]

    slot = _dot(low_ref[...], chosen.astype(BF16))
    expert = lax.broadcasted_iota(I32, chosen.shape, 0).astype(F32)
    eids, ranks, gsel = [], [], []
    for k in range(TOP_K):
        pick = chosen * jnp.where(slot == float(k), 1.0, 0.0)
        eids.append(jnp.sum(pick * expert, axis=0, keepdims=True))
        ranks.append(jnp.sum(pick * rank, axis=0, keepdims=True))
        gsel.append(jnp.sum(pick * gates, axis=0, keepdims=True))
    eid_ref[...] = jnp.concatenate(eids, axis=0).astype(I32)
    rank_ref[...] = jnp.concatenate(ranks, axis=0).astype(I32)
    gate_ref[...] = jnp.concatenate(gsel, axis=0).T


def _mixer_out(proj, sb, h, conv_w, gc, gs, w_out, gf, w_router, router_bias, *, seq, conv_width, tm=256):
    t, d = h.shape
    sb_width = sb.shape[1]
    n_exp = w_router.shape[1]
    router_hi = w_router.astype(BF16)
    router_lo = (w_router - router_hi.astype(F32)).astype(BF16)
    w_router = jnp.concatenate([router_hi, router_lo], axis=1)
    halo_blocks = tm // V7X_SUBLANES
    conv_spec = lambda c: pl.BlockSpec((tm, conv_width), lambda i: (i, c))
    halo_spec = lambda c: pl.BlockSpec((V7X_SUBLANES, conv_width), lambda i: (jnp.maximum(i * halo_blocks - 1, 0), c))
    earlier = lax.broadcasted_iota(I32, (tm, tm), 0) < lax.broadcasted_iota(I32, (tm, tm), 1)
    tri = jnp.concatenate([earlier.astype(BF16), jnp.ones((tm, tm), BF16)], axis=1)
    lower = (lax.broadcasted_iota(I32, (n_exp, n_exp), 1) < lax.broadcasted_iota(I32, (n_exp, n_exp), 0)).astype(BF16)
    blocks = (3 * _nbytes((tm, conv_width), BF16) + _nbytes((tm, sb_width), BF16) + 3 * _nbytes((tm, d), F32))
    temps = 10 * _nbytes((tm, d), F32) + _nbytes(w_out.shape, BF16) + _nbytes(w_router.shape, BF16)
    row_block = pl.BlockSpec((tm, d), lambda i: (i, 0))
    slot_block = pl.BlockSpec((TOP_K, tm), lambda i: (0, i))
    return pl.pallas_call(
        functools.partial(_mixer_out_kernel, tiles_per_seq=seq // tm),
        grid=(t // tm,),
        in_specs=[
            conv_spec(0), conv_spec(1), conv_spec(2), halo_spec(0), halo_spec(2),
            pl.BlockSpec((tm, sb_width), lambda i: (i, 0)),
            row_block,
            _resident(conv_w.shape, 1), _resident(gc.shape, 1), _resident(gs.shape, 1), _resident(w_out.shape, 1),
            _resident(gf.shape, 1), _resident(w_router.shape, 1), _resident(router_bias.shape, 1),
            _resident(tri.shape, 1), _resident(lower.shape, 1),
        ],
        out_specs=[
            row_block, pl.BlockSpec((tm,) + _slab_shape(d // 2), lambda i: (i, 0, 0)), slot_block, slot_block,
            pl.BlockSpec((tm, TOP_K), lambda i: (i, 0)),
            pl.BlockSpec((n_exp, tm), lambda i: (0, 0)),
        ],
        out_shape=[
            jax.ShapeDtypeStruct((t, d), F32),
            jax.ShapeDtypeStruct((t,) + _slab_shape(d // 2), U32),
            jax.ShapeDtypeStruct((TOP_K, t), I32),
            jax.ShapeDtypeStruct((TOP_K, t), I32),
            jax.ShapeDtypeStruct((t, TOP_K), F32),
            jax.ShapeDtypeStruct((n_exp, tm), F32),
        ],
        compiler_params=pltpu.CompilerParams(
            dimension_semantics=("arbitrary",),
            vmem_limit_bytes=_vmem_limit(blocks, temps),
        ),
        name="mixer_out",
    )(proj, proj, proj, proj, proj, sb, h, conv_w, gc, gs, w_out, gf, w_router, router_bias, tri, lower)


def _moe_layout_kernel(cnt_ref, eid_ref, rank_ref, pos_ref, off_ref, texp_ref, nused_ref, *, n_exp, n_tiles):
    def layout(e, off):
        off_ref[e] = off
        tiles = (cnt_ref[e] + EXPERT_TILE - 1) >> EXPERT_TILE_LOG2
        first = off >> EXPERT_TILE_LOG2

        def mark(j, c):
            texp_ref[first + j] = e
            return c

        lax.fori_loop(0, tiles, mark, 0)
        return off + (tiles << EXPERT_TILE_LOG2)

    total = lax.fori_loop(0, n_exp, layout, 0)
    off_ref[n_exp] = total
    used_tiles = total >> EXPERT_TILE_LOG2
    nused_ref[0] = used_tiles

    def tail(j, c):
        texp_ref[j] = n_exp - 1
        return c

    lax.fori_loop(used_tiles, n_tiles, tail, 0)

    eid = eid_ref[...]
    pos = rank_ref[...]
    for e in range(n_exp):
        pos = pos + jnp.where(eid == e, off_ref[e], 0)
    pos_ref[...] = pos


def _moe_layout(counts, eid, rank):
    n_exp = counts.shape[0]
    t = eid.shape[1]
    n_tiles = t * TOP_K // EXPERT_TILE + n_exp
    whole = lambda shape: pl.BlockSpec(shape, lambda i, cnt: (0,) * len(shape))
    whole_smem = lambda n: pl.BlockSpec((n,), lambda i, cnt: (0,), memory_space=pltpu.SMEM)
    return pl.pallas_call(
        functools.partial(_moe_layout_kernel, n_exp=n_exp, n_tiles=n_tiles),
        grid_spec=pltpu.PrefetchScalarGridSpec(
            num_scalar_prefetch=1,
            grid=(1,),
            in_specs=[whole(eid.shape), whole(rank.shape)],
            out_specs=[whole(eid.shape), whole_smem(n_exp + 1), whole_smem(n_tiles), whole_smem(1)],
        ),
        out_shape=[
            jax.ShapeDtypeStruct(eid.shape, I32),
            jax.ShapeDtypeStruct((n_exp + 1,), I32),
            jax.ShapeDtypeStruct((n_tiles,), I32),
            jax.ShapeDtypeStruct((1,), I32),
        ],
        compiler_params=pltpu.CompilerParams(
            dimension_semantics=("arbitrary",),
            vmem_limit_bytes=_vmem_limit(3 * _nbytes(eid.shape, I32), 4 * _nbytes(eid.shape, I32)),
        ),
        name="moe_layout",
    )(counts, eid, rank)


def _moe_dispatch_kernel(cnt_ref, off_ref, pos_ref, xt_ref, xs_hbm, zero_ref, row_sem, pad_sem, *, n_exp, n_assigned):
    ts = xt_ref.shape[0]

    @pl.when(pl.program_id(0) == 0)
    def _():
        zero_ref[...] = jnp.zeros_like(zero_ref)

        def pad(e, c):
            def zero_row(r, c):
                pltpu.make_async_copy(zero_ref.at[0], xs_hbm.at[r], pad_sem).start()
                return c

            lax.fori_loop(off_ref[e] + cnt_ref[e], off_ref[e + 1], zero_row, 0)
            return c

        def drain_row(r, c):
            pltpu.make_async_copy(zero_ref.at[0], xs_hbm.at[0], pad_sem).wait()
            return c

        lax.fori_loop(0, n_exp, pad, 0)
        lax.fori_loop(n_assigned, off_ref[n_exp], drain_row, 0)

    def token_group(g, c):
        base = pl.multiple_of(g * V7X_SUBLANES, V7X_SUBLANES)
        for u in range(V7X_SUBLANES):
            t = base + u
            for k in range(TOP_K):
                pltpu.make_async_copy(xt_ref.at[t], xs_hbm.at[pos_ref[k, t]], row_sem).start(priority=k % DMA_THREADS)
        return c

    lax.fori_loop(0, ts // V7X_SUBLANES, token_group, 0)
    for k in range(TOP_K):
        pltpu.make_async_copy(xt_ref, xs_hbm.at[pl.ds(0, ts)], row_sem).wait()


def _moe_dispatch(counts, off, pos, xt, *, ts=512):
    t = xt.shape[0]
    slab = xt.shape[1:]
    n_exp = counts.shape[0]
    n_tiles = t * TOP_K // EXPERT_TILE + n_exp
    blocks = _nbytes((ts,) + slab, xt.dtype)
    temps = _nbytes((1,) + slab, xt.dtype)
    return pl.pallas_call(
        functools.partial(_moe_dispatch_kernel, n_exp=n_exp, n_assigned=t * TOP_K),
        grid_spec=pltpu.PrefetchScalarGridSpec(
            num_scalar_prefetch=2,
            grid=(t // ts,),
            in_specs=[
                pl.BlockSpec((TOP_K, ts), lambda i, cnt, off: (0, i), memory_space=pltpu.SMEM),
                pl.BlockSpec((ts,) + slab, lambda i, cnt, off: (i, 0, 0)),
            ],
            out_specs=pl.BlockSpec(memory_space=pl.ANY),
            scratch_shapes=[
                pltpu.VMEM((1,) + slab, xt.dtype),
                pltpu.SemaphoreType.DMA(()),
                pltpu.SemaphoreType.DMA(()),
            ],
        ),
        out_shape=jax.ShapeDtypeStruct((n_tiles * EXPERT_TILE,) + slab, xt.dtype),
        compiler_params=pltpu.CompilerParams(
            dimension_semantics=("arbitrary",),
            vmem_limit_bytes=_vmem_limit(blocks, temps),
        ),
        name="moe_dispatch",
    )(counts, off, pos, xt)


def _moe_experts_kernel(texp_ref, nused_ref, xs_ref, wg_ref, wu_ref, wd_ref, ys_ref, wg_bf, wu_bf, wd_bf):
    i = pl.program_id(0)

    @pl.when((i == 0) | (texp_ref[i] != texp_ref[jnp.maximum(i - 1, 0)]))
    def _():
        wg_bf[...] = wg_ref[0].astype(BF16)
        wu_bf[...] = wu_ref[0].astype(BF16)
        wd_bf[...] = wd_ref[0].astype(BF16)

    @pl.when(i < nused_ref[0])
    def _():
        hid = _swiglu_hidden(_unpack_rows(_load_slabs(xs_ref)).astype(BF16), wg_bf[...], wu_bf[...])
        _store_slabs(ys_ref, _pack_rows(_dot(hid.astype(BF16), wd_bf[...])))

    @pl.when(i >= nused_ref[0])
    def _():
        ys_ref[...] = jnp.zeros_like(ys_ref)


def _moe_experts(texp, nused, xs, wg, wu, wd):
    rows = xs.shape[0]
    slab = xs.shape[1:]
    d, f = wg.shape[1:]
    n_tiles = rows // EXPERT_TILE
    blocks = 2 * _nbytes((EXPERT_TILE,) + slab, U32) + 3 * _nbytes((d, f), wg.dtype)
    temps = (3 * _nbytes((d, f), BF16) + 4 * _nbytes((EXPERT_TILE, f), F32) + _nbytes((EXPERT_TILE, d), F32)
             + _nbytes((EXPERT_TILE, d), BF16))
    expert_block = lambda shape: pl.BlockSpec(shape, lambda i, texp, nused: (texp[i], 0, 0))
    return pl.pallas_call(
        _moe_experts_kernel,
        grid_spec=pltpu.PrefetchScalarGridSpec(
            num_scalar_prefetch=2,
            grid=(n_tiles,),
            in_specs=[
                pl.BlockSpec((EXPERT_TILE,) + slab, lambda i, texp, nused: (jnp.minimum(i, nused[0] - 1), 0, 0)),
                expert_block((1, d, f)), expert_block((1, d, f)), expert_block((1, f, d)),
            ],
            out_specs=pl.BlockSpec((EXPERT_TILE,) + slab, lambda i, texp, nused: (i, 0, 0)),
            scratch_shapes=[pltpu.VMEM((d, f), BF16), pltpu.VMEM((d, f), BF16), pltpu.VMEM((f, d), BF16)],
        ),
        out_shape=jax.ShapeDtypeStruct((rows,) + slab, U32),
        compiler_params=pltpu.CompilerParams(
            dimension_semantics=("arbitrary",),
            vmem_limit_bytes=_vmem_limit(blocks, temps),
        ),
        name="moe_experts",
    )(texp, nused, xs, wg, wu, wd)


GATHER_DEPTH = 3


def _moe_combine_kernel(pos_ref, pos_1_ref, pos_2_ref, gate_ref, h1_ref, xt_ref, p_ref, sg_ref, su_ref, sd_ref,
                        gp_ref, wg_ref, wp_ref, gf_ref, ys_hbm, o_ref, buf_0, buf_1, buf_2, sem, *, final_norm):
    tc = h1_ref.shape[0]
    i = pl.program_id(0)
    bufs = (buf_0, buf_1, buf_2)

    def wait_rows(b):
        for k in range(TOP_K):
            pltpu.make_async_copy(ys_hbm.at[pl.ds(0, tc)], bufs[b].at[k], sem.at[b]).wait()

    @pl.when(i == 0)
    def _():
        def token_group(g, c):
            base = pl.multiple_of(g * V7X_SUBLANES, V7X_SUBLANES)
            for u in range(V7X_SUBLANES):
                for k in range(TOP_K):
                    t = base + u
                    pltpu.make_async_copy(ys_hbm.at[pos_ref[k, t]], buf_0.at[k, t], sem.at[0]).start()
                    pltpu.make_async_copy(ys_hbm.at[pos_1_ref[k, t]], buf_1.at[k, t], sem.at[1]).start()
            return c

        lax.fori_loop(0, tc // V7X_SUBLANES, token_group, 0)

    def step(cur):
        ahead = (cur + 2) % GATHER_DEPTH
        wait_rows(cur)
        for t in range(tc):
            for k in range(TOP_K):
                pltpu.make_async_copy(ys_hbm.at[pos_2_ref[k, t]], bufs[ahead].at[k, t],
                                      sem.at[ahead]).start(priority=k % DMA_THREADS)

        hs = _swiglu_hidden(_unpack_rows(_load_slabs(xt_ref)).astype(BF16), sg_ref[...], su_ref[...])
        h2 = h1_ref[...] + _dot(hs.astype(BF16), sd_ref[...])
        emb = _dot(p_ref[...].astype(BF16), wp_ref[...])
        gate = gate_ref[...]
        for k in range(TOP_K):
            h2 = h2 + gate[:, k:k + 1] * _unpack_rows(_load_slabs(bufs[cur].at[k]))

        gate_ple = _sigmoid(_dot(_rms(h2, gp_ref[...]).astype(BF16), wg_ref[...]))
        h3 = h2 + gate_ple * emb
        o_ref[...] = _rms(h3, gf_ref[...]) if final_norm else h3

        @pl.when(i + 1 == pl.num_programs(0))
        def _():
            wait_rows((cur + 1) % GATHER_DEPTH)
            wait_rows(ahead)

    for b in range(GATHER_DEPTH):
        @pl.when(i % GATHER_DEPTH == b)
        def _(b=b):
            step(b)


def _moe_combine(pos, gate, h1, xt, p, sg, su, sd, gp, wg, wp, gf, ys, *, final_norm, tc=128):
    t, d = h1.shape
    pd = p.shape[1]
    slab = ys.shape[1:]
    row_block = pl.BlockSpec((tc, d), lambda i: (i, 0))
    blocks = (2 * _nbytes((tc, d), F32) + _nbytes((tc,) + slab, U32) + _nbytes((tc, pd), F32)
              + _nbytes((tc, V7X_LANES), F32))
    temps = (GATHER_DEPTH * _nbytes((TOP_K, tc) + slab, U32) + 8 * _nbytes((tc, d), F32) + _nbytes(wg.shape, BF16)
             + _nbytes(wp.shape, BF16) + 3 * _nbytes(sg.shape, BF16))
    last = t // tc - 1
    slots_of_step = lambda j: pl.BlockSpec((TOP_K, tc), lambda i: (0, jnp.minimum(i + j, last)),
                                           memory_space=pltpu.SMEM)
    return pl.pallas_call(
        functools.partial(_moe_combine_kernel, final_norm=final_norm),
        grid=(t // tc,),
        in_specs=[
            slots_of_step(0), slots_of_step(1), slots_of_step(2),
            pl.BlockSpec((tc, TOP_K), lambda i: (i, 0)),
            row_block, pl.BlockSpec((tc,) + slab, lambda i: (i, 0, 0)),
            pl.BlockSpec((tc, pd), lambda i: (i, 0)),
            _resident(sg.shape, 1), _resident(su.shape, 1), _resident(sd.shape, 1), _resident(gp.shape, 1),
            _resident(wg.shape, 1), _resident(wp.shape, 1), _resident(gf.shape, 1),
            pl.BlockSpec(memory_space=pl.ANY),
        ],
        out_specs=row_block,
        out_shape=jax.ShapeDtypeStruct((t, d), F32),
        scratch_shapes=[pltpu.VMEM((TOP_K, tc) + slab, U32)] * GATHER_DEPTH
                       + [pltpu.SemaphoreType.DMA((GATHER_DEPTH,))],
        compiler_params=pltpu.CompilerParams(
            dimension_semantics=("arbitrary",),
            vmem_limit_bytes=_vmem_limit(blocks, temps),
        ),
        name="moe_combine",
    )(pos, pos, pos, gate, h1, xt, p, sg, su, sd, gp, wg, wp, gf, ys)


def kernel(x, p, norm_mix_g, w_in, conv_w, gnorm_conv_g, gnorm_sb_g, w_out, norm_ffn_g, w_router, router_bias,
           w_exp_gate, w_exp_up, w_exp_down, w_sh_gate, w_sh_up, w_sh_down, norm_ple_g, w_ple_gate, w_ple_proj,
           norm_final_g):
    bsz, seq, d = x.shape
    depth = p.shape[0]
    t = bsz * seq
    conv_width = conv_w.shape[-1]
    sb_width = gnorm_sb_g.shape[-1]
    n_qk = 3 * conv_width + 2 * sb_width
    assert w_in.shape[-1] == n_qk + sb_width
    assert w_router.shape[-1] % N_GROUPS == 0

    q_lo = 3 * conv_width
    col = jnp.arange(n_qk)
    q_scale = HEAD_DIM ** -0.5 * LOG2_E
    col_scale = jnp.where((col >= q_lo) & (col < q_lo + sb_width), q_scale, 1.0).astype(F32)[None, :]
    row = lambda v: v.astype(F32)[None, :]

    h = x.reshape(t, d)
    for i in range(depth):
        w_in_bf = w_in[i].astype(BF16)
        proj, vt = _norm_inproj(h, row(norm_mix_g[i]), w_in_bf[:, :n_qk], col_scale, w_in_bf[:, n_qk:].T,
                                key_block=ATTN_BLOCK)
        sb = _sb_attention(proj.reshape(bsz, seq, n_qk), vt, conv_width=conv_width, sb_width=sb_width,
                           blk=ATTN_BLOCK)
        h1, xt, eid, rank, gate, cnt = _mixer_out(
            proj, sb.reshape(t, sb_width), h, conv_w[i], row(gnorm_conv_g[i]), row(gnorm_sb_g[i]),
            w_out[i].astype(BF16), row(norm_ffn_g[i]), w_router[i], router_bias[i].astype(F32)[:, None],
            seq=seq, conv_width=conv_width)
        counts = cnt[:, 0].astype(I32)
        pos, off, texp, nused = _moe_layout(counts, eid, rank)
        xs = _moe_dispatch(counts, off, pos, xt)
        ys = _moe_experts(texp, nused, xs, w_exp_gate[i], w_exp_up[i], w_exp_down[i])
        h = _moe_combine(pos, gate, h1, xt, p[i].reshape(t, -1), w_sh_gate[i].astype(BF16),
                         w_sh_up[i].astype(BF16), w_sh_down[i].astype(BF16), row(norm_ple_g[i]),
                         w_ple_gate[i].astype(BF16), w_ple_proj[i].astype(BF16), row(norm_final_g), ys,
                         final_norm=(i == depth - 1))
    return h.reshape(bsz, seq, d)
```

```python
import functools

import jax
import jax.numpy as jnp
from jax import lax
from jax.experimental import pallas as pl
from jax.experimental.pallas import tpu as pltpu

NORM_EPS = 1e-6
HEAD_DIM = 128
CONV_KERNEL = 3
N_GROUPS = 8
TOPK_GROUPS = 4
TOP_K = 8
ROUTED_SCALE = 2.5
LOG2_E = 1.4426950408889634
F32_EXP2_UNDERFLOW = -160.0

V7X_VMEM_BYTES = 64 * 1024 * 1024
V7X_SUBLANES = 8
V7X_LANES = 128
DMA_THREADS = 2

F32 = jnp.float32
BF16 = jnp.bfloat16
I32 = jnp.int32
U32 = jnp.uint32

ATTN_BLOCK = 256
EXPERT_TILE = 512
EXPERT_TILE_LOG2 = 9
assert 1 << EXPERT_TILE_LOG2 == EXPERT_TILE


def _vmem_limit(pipelined_bytes, resident_bytes):
    want = 2 * pipelined_bytes + resident_bytes
    return int(min(want, V7X_VMEM_BYTES - 4 * 1024 * 1024))


def _nbytes(shape, dtype):
    n = 1
    for s in shape:
        n *= s
    return n * jnp.dtype(dtype).itemsize


def _resident(shape, n_grid_axes, n_prefetch=0):
    zeros = (0,) * len(shape)
    return pl.BlockSpec(shape, lambda *_: zeros, pipeline_mode=pl.Buffered(1))


def _rms(y, g):
    return y * lax.rsqrt(jnp.mean(y * y, axis=-1, keepdims=True) + NORM_EPS) * g


def _sigmoid(a):
    return 1.0 / (1.0 + jnp.exp(-a))


def _softplus2(z):
    return jnp.maximum(z, 0.0) + jnp.log2(1.0 + jnp.exp2(-jnp.abs(z)))


def _dot(a, b):
    return jnp.dot(a, b, preferred_element_type=F32)


def _dot_nt(a, b):
    return lax.dot_general(a, b, (((1,), (1,)), ((), ())), preferred_element_type=F32)


def _swiglu_hidden(x, wg, wu):
    a = _dot(x, wg)
    return a * _sigmoid(a) * _dot(x, wu)


def _pack_rows(x):
    half = x.shape[1] // 2
    as_bits = lambda v: lax.bitcast_convert_type(v.astype(BF16).astype(F32), U32)
    return (as_bits(x[:, :half]) >> 16) | (as_bits(x[:, half:]) & jnp.uint32(0xFFFF0000))


def _unpack_rows(w):
    lo = lax.bitcast_convert_type(w << 16, F32)
    hi = lax.bitcast_convert_type(w & jnp.uint32(0xFFFF0000), F32)
    return jnp.concatenate([lo, hi], axis=1)


def _slab_shape(words):
    return (words // V7X_LANES, V7X_LANES)


def _store_slabs(ref, words):
    by_chunk = jnp.stack([words[:, j * V7X_LANES:(j + 1) * V7X_LANES] for j in range(ref.shape[1])], axis=0)
    ref[...] = pltpu.einshape("jrl->rjl", by_chunk)


def _load_slabs(ref):
    by_chunk = pltpu.einshape("rjl->jrl", ref[...])
    return jnp.concatenate([by_chunk[j] for j in range(ref.shape[1])], axis=1)


def _norm_inproj_kernel(x_ref, g_ref, w_ref, cs_ref, wvt_ref, o_ref, vt_ref, hn_ref, *, n_col_tiles):
    j = pl.program_id(1)

    @pl.when(j == 0)
    def _():
        hn_ref[...] = _rms(x_ref[...], g_ref[...]).astype(hn_ref.dtype)

    @pl.when(j < n_col_tiles)
    def _():
        o_ref[...] = (_dot(hn_ref[...], w_ref[...]) * cs_ref[...]).astype(o_ref.dtype)

    @pl.when(j == n_col_tiles)
    def _():
        vt = _dot_nt(wvt_ref[...], hn_ref[...]).astype(vt_ref.dtype)
        kb = vt_ref.shape[2]
        for c in range(vt_ref.shape[0]):
            vt_ref[c] = vt[:, c * kb:(c + 1) * kb]


def _norm_inproj(h, g, w, col_scale, wvt, *, key_block, tm=1024, tn=1024):
    t, d = h.shape
    n = w.shape[1]
    vw = wvt.shape[0]
    n_col_tiles = n // tn
    last = n_col_tiles - 1
    blocks = _nbytes((tm, d), F32) + _nbytes((d, tn), BF16) + _nbytes((tm, tn), BF16) + _nbytes((vw, tm), BF16)
    temps = _nbytes((tm, d), BF16) + 2 * _nbytes((tm, d), F32) + _nbytes((tm, tn), F32) + _nbytes((vw, d), BF16)
    return pl.pallas_call(
        functools.partial(_norm_inproj_kernel, n_col_tiles=n_col_tiles),
        grid=(t // tm, n_col_tiles + 1),
        in_specs=[
            pl.BlockSpec((tm, d), lambda i, j: (i, 0)),
            _resident((1, d), 2),
            pl.BlockSpec((d, tn), lambda i, j: (0, jnp.minimum(j, last))),
            pl.BlockSpec((1, tn), lambda i, j: (0, jnp.minimum(j, last))),
            _resident((vw, d), 2),
        ],
        out_specs=[
            pl.BlockSpec((tm, tn), lambda i, j: (i, jnp.minimum(j, last))),
            pl.BlockSpec((tm // key_block, vw, key_block), lambda i, j: (i, 0, 0)),
        ],
        out_shape=[jax.ShapeDtypeStruct((t, n), BF16), jax.ShapeDtypeStruct((t // key_block, vw, key_block), BF16)],
        scratch_shapes=[pltpu.VMEM((tm, d), BF16)],
        compiler_params=pltpu.CompilerParams(
            dimension_semantics=("arbitrary", "arbitrary"),
            vmem_limit_bytes=_vmem_limit(blocks, temps),
        ),
        name="norm_inproj",
    )(h, g, w, col_scale, wvt)


def _sb_attn_kernel(q_ref, k_ref, vt_ref, lo_ref, o_ref, acc_ref, run_ref, *, blk, heads):
    qi = pl.program_id(2)
    lo = lo_ref[...]
    acc_ref[...] = jnp.zeros_like(acc_ref)
    run_ref[...] = jnp.zeros_like(run_ref)

    def block(j, masked):
        start = pl.multiple_of(j * blk, blk)
        cols = [slice(h * HEAD_DIM, (h + 1) * HEAD_DIM) for h in range(heads)]
        if masked:
            causal = lax.broadcasted_iota(I32, (blk, blk), 0) < lax.broadcasted_iota(I32, (blk, blk), 1)
        zs = [_dot_nt(k_ref[0, pl.ds(start, blk), c], q_ref[0, :, c]) for c in cols]
        sps = [_softplus2(z) for z in zs]
        keeps = [jnp.where(causal, sp, 0.0) if masked else sp for sp in sps]
        css = [_dot(lo, keep.astype(BF16)) for keep in keeps]
        for h in range(heads):
            run = run_ref[h]
            run_all = jnp.concatenate([run] * (blk // V7X_SUBLANES), axis=0)
            a = jnp.exp2(zs[h] - sps[h] + css[h][:blk] + run_all)
            if masked:
                a = jnp.where(causal, a, 0.0)
            acc_ref[h] += _dot(vt_ref[j, cols[h], :], a.astype(BF16))
            run_ref[h] = run + css[h][blk:]

    block(qi, True)

    def still_visible():
        return jnp.max(run_ref[...]) > F32_EXP2_UNDERFLOW

    def body(carry):
        n, _ = carry
        block(qi - 1 - n, False)
        return n + 1, still_visible()

    lax.while_loop(lambda carry: (carry[0] < qi) & carry[1], body, (jnp.int32(0), still_visible()))
    for h in range(heads):
        o_ref[0, :, h * HEAD_DIM:(h + 1) * HEAD_DIM] = acc_ref[h].T.astype(o_ref.dtype)


def _sb_attention(proj3, vt, *, conv_width, sb_width, blk, heads=8):
    b, s, _ = proj3.shape
    gw = heads * HEAD_DIM
    groups = sb_width // gw
    q_off = 3 * conv_width // gw
    k_off = q_off + groups
    later = lax.broadcasted_iota(I32, (blk, blk), 1) > lax.broadcasted_iota(I32, (blk, blk), 0)
    lo = -jnp.concatenate([later.astype(BF16), jnp.ones((V7X_SUBLANES, blk), BF16)], axis=0)
    blocks = 2 * _nbytes((blk, gw), BF16) + 2 * _nbytes((s, gw), BF16)
    temps = heads * 8 * _nbytes((blk, blk), F32)
    return pl.pallas_call(
        functools.partial(_sb_attn_kernel, blk=blk, heads=heads),
        grid=(b, groups, s // blk),
        in_specs=[
            pl.BlockSpec((1, blk, gw), lambda bi, gi, qi: (bi, qi, q_off + gi)),
            pl.BlockSpec((1, s, gw), lambda bi, gi, qi: (bi, 0, k_off + gi)),
            pl.BlockSpec((s // blk, gw, blk), lambda bi, gi, qi: (bi, gi, 0)),
            _resident((blk + V7X_SUBLANES, blk), 3),
        ],
        out_specs=pl.BlockSpec((1, blk, gw), lambda bi, gi, qi: (bi, qi, gi)),
        out_shape=jax.ShapeDtypeStruct((b, s, sb_width), BF16),
        scratch_shapes=[pltpu.VMEM((heads, HEAD_DIM, blk), F32), pltpu.VMEM((heads, V7X_SUBLANES, blk), F32)],
        compiler_params=pltpu.CompilerParams(
            dimension_semantics=("arbitrary", "arbitrary", "arbitrary"),
            vmem_limit_bytes=_vmem_limit(blocks, temps),
        ),
        name="sb_attn",
    )(proj3, proj3, vt, lo)


def _rank_of(vals):
    n = vals.shape[0]
    idx = lax.broadcasted_iota(I32, vals.shape, 0)
    rank = jnp.zeros(vals.shape, F32)
    for other in range(n):
        o = vals[other:other + 1, :]
        rank = rank + jnp.where(o > vals, 1.0, 0.0) + jnp.where(o == vals, jnp.where(idx > other, 1.0, 0.0), 0.0)
    return rank


def _router(logits_t, bias):
    n_exp, tm = logits_t.shape
    per_group = n_exp // N_GROUPS
    scores = _sigmoid(logits_t)
    biased = scores + bias

    grouped = biased.reshape(N_GROUPS, per_group, tm)
    top1 = jnp.max(grouped, axis=1, keepdims=True)
    n_top = jnp.sum(jnp.where(grouped == top1, 1.0, 0.0), axis=1, keepdims=True)
    below = jnp.max(jnp.where(grouped < top1, grouped, -jnp.inf), axis=1, keepdims=True)
    group_score = (top1 + jnp.where(n_top >= 2.0, top1, below)).reshape(N_GROUPS, tm)

    group_ok = _rank_of(group_score) < float(TOPK_GROUPS)
    expert_ok = jnp.broadcast_to(group_ok.reshape(N_GROUPS, 1, tm), (N_GROUPS, per_group, tm)).reshape(n_exp, tm)
    chosen = jnp.where(_rank_of(jnp.where(expert_ok, biased, -jnp.inf)) < float(TOP_K), 1.0, 0.0)
    w = chosen * scores
    return chosen, w / jnp.sum(w, axis=0, keepdims=True) * ROUTED_SCALE


def _mixer_out_kernel(ch_ref, cb_ref, cc_ref, hh_ref, hc_ref, sb_ref, x_ref, cw_ref, gc_ref, gs_ref,
                      wo_ref, gf_ref, wr_ref, rb_ref, tri_ref, low_ref, sg_ref, su_ref, sd_ref,
                      h1_ref, xt_ref, eid_ref, rank_ref, gate_ref, cnt_ref, *, tiles_per_seq):
    i = pl.program_id(0)
    u = cc_ref[...].astype(F32) * ch_ref[...].astype(F32)
    halo = hc_ref[...].astype(F32) * hh_ref[...].astype(F32)
    halo = jnp.where(i % tiles_per_seq == 0, 0.0, halo)
    prev1 = halo[V7X_SUBLANES - 1:V7X_SUBLANES, :]
    prev2 = halo[V7X_SUBLANES - 2:V7X_SUBLANES - 1, :]
    row = lax.broadcasted_iota(I32, u.shape, 0)
    u1 = jnp.where(row == 0, prev1, pltpu.roll(u, 1, 0))
    u2 = jnp.where(row == 0, prev2, jnp.where(row == 1, prev1, pltpu.roll(u, 2, 0)))
    cw = cw_ref[...]
    conv = cb_ref[...].astype(F32) * (cw[0:1, :] * u2 + cw[1:2, :] * u1 + cw[2:3, :] * u)

    y = jnp.concatenate([_rms(conv, gc_ref[...]), _rms(sb_ref[...].astype(F32), gs_ref[...])], axis=-1)
    h1 = x_ref[...] + _dot(y.astype(BF16), wo_ref[...])
    xt = _rms(h1, gf_ref[...])
    _store_slabs(xt_ref, _pack_rows(xt))
    xt_hi = xt.astype(BF16)
    shared = _swiglu_hidden(xt_hi, sg_ref[...], su_ref[...])
    h1_ref[...] = h1 + _dot(shared.astype(BF16), sd_ref[...])

    n_exp = wr_ref.shape[1] // 2
    xt_lo = (xt - xt_hi.astype(F32)).astype(BF16)
    by_hi = _dot(xt_hi, wr_ref[...])
    logits = by_hi[:, :n_exp] + by_hi[:, n_exp:] + _dot(xt_lo, wr_ref[:, :n_exp])
    chosen, gates = _router(logits.T, rb_ref[...])
    tm = chosen.shape[1]

    @pl.when(i == 0)
    def _():
        cnt_ref[...] = jnp.zeros_like(cnt_ref)

    counts = _dot(chosen.astype(BF16), tri_ref[...])
    seen = cnt_ref[...]
    rank = seen + counts[:, :tm]
    cnt_ref[...] = seen + counts[:, tm:]

    slot = _dot(low_ref[...], chosen.astype(BF16))
    expert = lax.broadcasted_iota(I32, chosen.shape, 0).astype(F32)
    eids, ranks, gsel = [], [], []
    for k in range(TOP_K):
        pick = chosen * jnp.where(slot == float(k), 1.0, 0.0)
        eids.append(jnp.sum(pick * expert, axis=0, keepdims=True))
        ranks.append(jnp.sum(pick * rank, axis=0, keepdims=True))
        gsel.append(jnp.sum(pick * gates, axis=0, keepdims=True))
    eid_ref[...] = jnp.concatenate(eids, axis=0).astype(I32)
    rank_ref[...] = jnp.concatenate(ranks, axis=0).astype(I32)
    gate_ref[...] = jnp.concatenate(gsel, axis=0).T


def _mixer_out(proj, sb, h, conv_w, gc, gs, w_out, gf, w_router, router_bias, sg, su, sd, *, seq, conv_width,
               tm=256):
    t, d = h.shape
    sb_width = sb.shape[1]
    n_exp = w_router.shape[1]
    router_hi = w_router.astype(BF16)
    router_lo = (w_router - router_hi.astype(F32)).astype(BF16)
    w_router = jnp.concatenate([router_hi, router_lo], axis=1)
    halo_blocks = tm // V7X_SUBLANES
    conv_spec = lambda c: pl.BlockSpec((tm, conv_width), lambda i: (i, c))
    halo_spec = lambda c: pl.BlockSpec((V7X_SUBLANES, conv_width), lambda i: (jnp.maximum(i * halo_blocks - 1, 0), c))
    earlier = lax.broadcasted_iota(I32, (tm, tm), 0) < lax.broadcasted_iota(I32, (tm, tm), 1)
    tri = jnp.concatenate([earlier.astype(BF16), jnp.ones((tm, tm), BF16)], axis=1)
    lower = (lax.broadcasted_iota(I32, (n_exp, n_exp), 1) < lax.broadcasted_iota(I32, (n_exp, n_exp), 0)).astype(BF16)
    blocks = (3 * _nbytes((tm, conv_width), BF16) + _nbytes((tm, sb_width), BF16) + 3 * _nbytes((tm, d), F32))
    temps = (10 * _nbytes((tm, d), F32) + _nbytes(w_out.shape, BF16) + _nbytes(w_router.shape, BF16)
             + 3 * _nbytes(sg.shape, BF16))
    row_block = pl.BlockSpec((tm, d), lambda i: (i, 0))
    slot_block = pl.BlockSpec((TOP_K, tm), lambda i: (0, i))
    return pl.pallas_call(
        functools.partial(_mixer_out_kernel, tiles_per_seq=seq // tm),
        grid=(t // tm,),
        in_specs=[
            conv_spec(0), conv_spec(1), conv_spec(2), halo_spec(0), halo_spec(2),
            pl.BlockSpec((tm, sb_width), lambda i: (i, 0)),
            row_block,
            _resident(conv_w.shape, 1), _resident(gc.shape, 1), _resident(gs.shape, 1), _resident(w_out.shape, 1),
            _resident(gf.shape, 1), _resident(w_router.shape, 1), _resident(router_bias.shape, 1),
            _resident(tri.shape, 1), _resident(lower.shape, 1),
            _resident(sg.shape, 1), _resident(su.shape, 1), _resident(sd.shape, 1),
        ],
        out_specs=[
            row_block, pl.BlockSpec((tm,) + _slab_shape(d // 2), lambda i: (i, 0, 0)), slot_block, slot_block,
            pl.BlockSpec((tm, TOP_K), lambda i: (i, 0)),
            pl.BlockSpec((n_exp, tm), lambda i: (0, 0)),
        ],
        out_shape=[
            jax.ShapeDtypeStruct((t, d), F32),
            jax.ShapeDtypeStruct((t,) + _slab_shape(d // 2), U32),
            jax.ShapeDtypeStruct((TOP_K, t), I32),
            jax.ShapeDtypeStruct((TOP_K, t), I32),
            jax.ShapeDtypeStruct((t, TOP_K), F32),
            jax.ShapeDtypeStruct((n_exp, tm), F32),
        ],
        compiler_params=pltpu.CompilerParams(
            dimension_semantics=("arbitrary",),
            vmem_limit_bytes=_vmem_limit(blocks, temps),
        ),
        name="mixer_out",
    )(proj, proj, proj, proj, proj, sb, h, conv_w, gc, gs, w_out, gf, w_router, router_bias, tri, lower, sg, su, sd)


def _moe_layout_kernel(cnt_ref, eid_ref, rank_ref, pos_ref, off_ref, texp_ref, trows_ref, nused_ref, *, n_exp,
                       n_tiles):
    def layout(e, off):
        off_ref[e] = off
        tiles = (cnt_ref[e] + EXPERT_TILE - 1) >> EXPERT_TILE_LOG2
        first = off >> EXPERT_TILE_LOG2

        def mark(j, c):
            texp_ref[first + j] = e
            trows_ref[first + j] = jnp.minimum(cnt_ref[e] - (j << EXPERT_TILE_LOG2), EXPERT_TILE)
            return c

        lax.fori_loop(0, tiles, mark, 0)
        return off + (tiles << EXPERT_TILE_LOG2)

    total = lax.fori_loop(0, n_exp, layout, 0)
    off_ref[n_exp] = total
    used_tiles = total >> EXPERT_TILE_LOG2
    nused_ref[0] = used_tiles

    def tail(j, c):
        texp_ref[j] = n_exp - 1
        trows_ref[j] = 0
        return c

    lax.fori_loop(used_tiles, n_tiles, tail, 0)

    eid = eid_ref[...]
    pos = rank_ref[...]
    for e in range(n_exp):
        pos = pos + jnp.where(eid == e, off_ref[e], 0)
    pos_ref[...] = pos


def _moe_layout(counts, eid, rank):
    n_exp = counts.shape[0]
    t = eid.shape[1]
    n_tiles = t * TOP_K // EXPERT_TILE + n_exp
    whole = lambda shape: pl.BlockSpec(shape, lambda i, cnt: (0,) * len(shape))
    whole_smem = lambda n: pl.BlockSpec((n,), lambda i, cnt: (0,), memory_space=pltpu.SMEM)
    return pl.pallas_call(
        functools.partial(_moe_layout_kernel, n_exp=n_exp, n_tiles=n_tiles),
        grid_spec=pltpu.PrefetchScalarGridSpec(
            num_scalar_prefetch=1,
            grid=(1,),
            in_specs=[whole(eid.shape), whole(rank.shape)],
            out_specs=[whole(eid.shape), whole_smem(n_exp + 1), whole_smem(n_tiles), whole_smem(n_tiles),
                       whole_smem(1)],
        ),
        out_shape=[
            jax.ShapeDtypeStruct(eid.shape, I32),
            jax.ShapeDtypeStruct((n_exp + 1,), I32),
            jax.ShapeDtypeStruct((n_tiles,), I32),
            jax.ShapeDtypeStruct((n_tiles,), I32),
            jax.ShapeDtypeStruct((1,), I32),
        ],
        compiler_params=pltpu.CompilerParams(
            dimension_semantics=("arbitrary",),
            vmem_limit_bytes=_vmem_limit(3 * _nbytes(eid.shape, I32), 4 * _nbytes(eid.shape, I32)),
        ),
        name="moe_layout",
    )(counts, eid, rank)


def _moe_dispatch_kernel(cnt_ref, off_ref, pos_ref, xt_ref, xs_hbm, zero_ref, row_sem, pad_sem, *, n_exp, n_assigned):
    ts = xt_ref.shape[0]

    @pl.when(pl.program_id(0) == 0)
    def _():
        zero_ref[...] = jnp.zeros_like(zero_ref)

        def pad(e, c):
            def zero_row(r, c):
                pltpu.make_async_copy(zero_ref.at[0], xs_hbm.at[r], pad_sem).start()
                return c

            lax.fori_loop(off_ref[e] + cnt_ref[e], off_ref[e + 1], zero_row, 0)
            return c

        def drain_row(r, c):
            pltpu.make_async_copy(zero_ref.at[0], xs_hbm.at[0], pad_sem).wait()
            return c

        lax.fori_loop(0, n_exp, pad, 0)
        lax.fori_loop(n_assigned, off_ref[n_exp], drain_row, 0)

    def token_group(g, c):
        base = pl.multiple_of(g * V7X_SUBLANES, V7X_SUBLANES)
        for u in range(V7X_SUBLANES):
            t = base + u
            for k in range(TOP_K):
                pltpu.make_async_copy(xt_ref.at[t], xs_hbm.at[pos_ref[k, t]], row_sem).start(priority=k % DMA_THREADS)
        return c

    lax.fori_loop(0, ts // V7X_SUBLANES, token_group, 0)
    for k in range(TOP_K):
        pltpu.make_async_copy(xt_ref, xs_hbm.at[pl.ds(0, ts)], row_sem).wait()


def _moe_dispatch(counts, off, pos, xt, *, ts=512):
    t = xt.shape[0]
    slab = xt.shape[1:]
    n_exp = counts.shape[0]
    n_tiles = t * TOP_K // EXPERT_TILE + n_exp
    blocks = _nbytes((ts,) + slab, xt.dtype)
    temps = _nbytes((1,) + slab, xt.dtype)
    return pl.pallas_call(
        functools.partial(_moe_dispatch_kernel, n_exp=n_exp, n_assigned=t * TOP_K),
        grid_spec=pltpu.PrefetchScalarGridSpec(
            num_scalar_prefetch=2,
            grid=(t // ts,),
            in_specs=[
                pl.BlockSpec((TOP_K, ts), lambda i, cnt, off: (0, i), memory_space=pltpu.SMEM),
                pl.BlockSpec((ts,) + slab, lambda i, cnt, off: (i, 0, 0)),
            ],
            out_specs=pl.BlockSpec(memory_space=pl.ANY),
            scratch_shapes=[
                pltpu.VMEM((1,) + slab, xt.dtype),
                pltpu.SemaphoreType.DMA(()),
                pltpu.SemaphoreType.DMA(()),
            ],
        ),
        out_shape=jax.ShapeDtypeStruct((n_tiles * EXPERT_TILE,) + slab, xt.dtype),
        compiler_params=pltpu.CompilerParams(
            dimension_semantics=("arbitrary",),
            vmem_limit_bytes=_vmem_limit(blocks, temps),
        ),
        name="moe_dispatch",
    )(counts, off, pos, xt)


def _moe_experts_kernel(texp_ref, trows_ref, nused_ref, xs_ref, wg_ref, wu_ref, wd_ref, ys_ref, wg_bf, wu_bf, wd_bf):
    i = pl.program_id(0)
    occupied = trows_ref[i]

    @pl.when((i == 0) | (texp_ref[i] != texp_ref[jnp.maximum(i - 1, 0)]))
    def _():
        wg_bf[...] = wg_ref[0].astype(BF16)
        wu_bf[...] = wu_ref[0].astype(BF16)
        wd_bf[...] = wd_ref[0].astype(BF16)

    def experts_on(rows):
        x = _unpack_rows(_load_slabs(xs_ref.at[pl.ds(0, rows)])).astype(BF16)
        hid = _swiglu_hidden(x, wg_bf[...], wu_bf[...])
        _store_slabs(ys_ref.at[pl.ds(0, rows)], _pack_rows(_dot(hid.astype(BF16), wd_bf[...])))
        if rows < EXPERT_TILE:
            ys_ref[pl.ds(rows, EXPERT_TILE - rows)] = jnp.zeros((EXPERT_TILE - rows,) + ys_ref.shape[1:], ys_ref.dtype)

    lower = 0
    for rows in (EXPERT_TILE // 4, EXPERT_TILE // 2, EXPERT_TILE):
        @pl.when((occupied > lower) & (occupied <= rows))
        def _(rows=rows):
            experts_on(rows)
        lower = rows

    @pl.when(occupied == 0)
    def _():
        ys_ref[...] = jnp.zeros_like(ys_ref)


def _moe_experts(texp, trows, nused, xs, wg, wu, wd):
    rows = xs.shape[0]
    slab = xs.shape[1:]
    d, f = wg.shape[1:]
    n_tiles = rows // EXPERT_TILE
    blocks = 2 * _nbytes((EXPERT_TILE,) + slab, U32) + 3 * _nbytes((d, f), wg.dtype)
    temps = (3 * _nbytes((d, f), BF16) + 4 * _nbytes((EXPERT_TILE, f), F32) + _nbytes((EXPERT_TILE, d), F32)
             + _nbytes((EXPERT_TILE, d), BF16))
    expert_block = lambda shape: pl.BlockSpec(shape, lambda i, texp, trows, nused: (texp[i], 0, 0))
    return pl.pallas_call(
        _moe_experts_kernel,
        grid_spec=pltpu.PrefetchScalarGridSpec(
            num_scalar_prefetch=3,
            grid=(n_tiles,),
            in_specs=[
                pl.BlockSpec((EXPERT_TILE,) + slab,
                             lambda i, texp, trows, nused: (jnp.minimum(i, nused[0] - 1), 0, 0)),
                expert_block((1, d, f)), expert_block((1, d, f)), expert_block((1, f, d)),
            ],
            out_specs=pl.BlockSpec((EXPERT_TILE,) + slab, lambda i, texp, trows, nused: (i, 0, 0)),
            scratch_shapes=[pltpu.VMEM((d, f), BF16), pltpu.VMEM((d, f), BF16), pltpu.VMEM((f, d), BF16)],
        ),
        out_shape=jax.ShapeDtypeStruct((rows,) + slab, U32),
        compiler_params=pltpu.CompilerParams(
            dimension_semantics=("arbitrary",),
            vmem_limit_bytes=_vmem_limit(blocks, temps),
        ),
        name="moe_experts",
    )(texp, trows, nused, xs, wg, wu, wd)


GATHER_DEPTH = 3


def _moe_combine_kernel(pos_ref, pos_1_ref, pos_2_ref, gate_ref, h1_ref, p_ref, gp_ref, wg_ref, wp_ref, gf_ref,
                        ys_hbm, o_ref, buf_0, buf_1, buf_2, sem, *, final_norm):
    tc = h1_ref.shape[0]
    i = pl.program_id(0)
    bufs = (buf_0, buf_1, buf_2)

    def wait_rows(b):
        for k in range(TOP_K):
            pltpu.make_async_copy(ys_hbm.at[pl.ds(0, tc)], bufs[b].at[k], sem.at[b]).wait()

    @pl.when(i == 0)
    def _():
        def token_group(g, c):
            base = pl.multiple_of(g * V7X_SUBLANES, V7X_SUBLANES)
            for u in range(V7X_SUBLANES):
                for k in range(TOP_K):
                    t = base + u
                    pltpu.make_async_copy(ys_hbm.at[pos_ref[k, t]], buf_0.at[k, t], sem.at[0]).start()
                    pltpu.make_async_copy(ys_hbm.at[pos_1_ref[k, t]], buf_1.at[k, t], sem.at[1]).start()
            return c

        lax.fori_loop(0, tc // V7X_SUBLANES, token_group, 0)

    def step(cur):
        ahead = (cur + 2) % GATHER_DEPTH
        wait_rows(cur)
        for t in range(tc):
            for k in range(TOP_K):
                pltpu.make_async_copy(ys_hbm.at[pos_2_ref[k, t]], bufs[ahead].at[k, t],
                                      sem.at[ahead]).start(priority=k % DMA_THREADS)

        h2 = h1_ref[...]
        emb = _dot(p_ref[...].astype(BF16), wp_ref[...])
        gate = gate_ref[...]
        for k in range(TOP_K):
            h2 = h2 + gate[:, k:k + 1] * _unpack_rows(_load_slabs(bufs[cur].at[k]))

        gate_ple = _sigmoid(_dot(_rms(h2, gp_ref[...]).astype(BF16), wg_ref[...]))
        h3 = h2 + gate_ple * emb
        o_ref[...] = _rms(h3, gf_ref[...]) if final_norm else h3

        @pl.when(i + 1 == pl.num_programs(0))
        def _():
            wait_rows((cur + 1) % GATHER_DEPTH)
            wait_rows(ahead)

    for b in range(GATHER_DEPTH):
        @pl.when(i % GATHER_DEPTH == b)
        def _(b=b):
            step(b)


def _moe_combine(pos, gate, h1, p, gp, wg, wp, gf, ys, *, final_norm, tc=256):
    t, d = h1.shape
    pd = p.shape[1]
    slab = ys.shape[1:]
    row_block = pl.BlockSpec((tc, d), lambda i: (i, 0))
    blocks = 2 * _nbytes((tc, d), F32) + _nbytes((tc, pd), F32) + _nbytes((tc, V7X_LANES), F32)
    temps = (GATHER_DEPTH * _nbytes((TOP_K, tc) + slab, U32) + 8 * _nbytes((tc, d), F32) + _nbytes(wg.shape, BF16)
             + _nbytes(wp.shape, BF16))
    last = t // tc - 1
    slots_of_step = lambda j: pl.BlockSpec((TOP_K, tc), lambda i: (0, jnp.minimum(i + j, last)),
                                           memory_space=pltpu.SMEM)
    return pl.pallas_call(
        functools.partial(_moe_combine_kernel, final_norm=final_norm),
        grid=(t // tc,),
        in_specs=[
            slots_of_step(0), slots_of_step(1), slots_of_step(2),
            pl.BlockSpec((tc, TOP_K), lambda i: (i, 0)),
            row_block,
            pl.BlockSpec((tc, pd), lambda i: (i, 0)),
            _resident(gp.shape, 1), _resident(wg.shape, 1), _resident(wp.shape, 1), _resident(gf.shape, 1),
            pl.BlockSpec(memory_space=pl.ANY),
        ],
        out_specs=row_block,
        out_shape=jax.ShapeDtypeStruct((t, d), F32),
        scratch_shapes=[pltpu.VMEM((TOP_K, tc) + slab, U32)] * GATHER_DEPTH
                       + [pltpu.SemaphoreType.DMA((GATHER_DEPTH,))],
        compiler_params=pltpu.CompilerParams(
            dimension_semantics=("arbitrary",),
            vmem_limit_bytes=_vmem_limit(blocks, temps),
        ),
        name="moe_combine",
    )(pos, pos, pos, gate, h1, p, gp, wg, wp, gf, ys)


def kernel(x, p, norm_mix_g, w_in, conv_w, gnorm_conv_g, gnorm_sb_g, w_out, norm_ffn_g, w_router, router_bias,
           w_exp_gate, w_exp_up, w_exp_down, w_sh_gate, w_sh_up, w_sh_down, norm_ple_g, w_ple_gate, w_ple_proj,
           norm_final_g):
    bsz, seq, d = x.shape
    depth = p.shape[0]
    t = bsz * seq
    conv_width = conv_w.shape[-1]
    sb_width = gnorm_sb_g.shape[-1]
    n_qk = 3 * conv_width + 2 * sb_width
    assert w_in.shape[-1] == n_qk + sb_width
    assert w_router.shape[-1] % N_GROUPS == 0

    q_lo = 3 * conv_width
    col = jnp.arange(n_qk)
    q_scale = HEAD_DIM ** -0.5 * LOG2_E
    col_scale = jnp.where((col >= q_lo) & (col < q_lo + sb_width), q_scale, 1.0).astype(F32)[None, :]
    row = lambda v: v.astype(F32)[None, :]

    h = x.reshape(t, d)
    for i in range(depth):
        w_in_bf = w_in[i].astype(BF16)
        proj, vt = _norm_inproj(h, row(norm_mix_g[i]), w_in_bf[:, :n_qk], col_scale, w_in_bf[:, n_qk:].T,
                                key_block=ATTN_BLOCK)
        sb = _sb_attention(proj.reshape(bsz, seq, n_qk), vt, conv_width=conv_width, sb_width=sb_width,
                           blk=ATTN_BLOCK)
        h1, xt, eid, rank, gate, cnt = _mixer_out(
            proj, sb.reshape(t, sb_width), h, conv_w[i], row(gnorm_conv_g[i]), row(gnorm_sb_g[i]),
            w_out[i].astype(BF16), row(norm_ffn_g[i]), w_router[i], router_bias[i].astype(F32)[:, None],
            w_sh_gate[i].astype(BF16), w_sh_up[i].astype(BF16), w_sh_down[i].astype(BF16),
            seq=seq, conv_width=conv_width)
        counts = cnt[:, 0].astype(I32)
        pos, off, texp, trows, nused = _moe_layout(counts, eid, rank)
        xs = _moe_dispatch(counts, off, pos, xt)
        ys = _moe_experts(texp, trows, nused, xs, w_exp_gate[i], w_exp_up[i], w_exp_down[i])
        h = _moe_combine(pos, gate, h1, p[i].reshape(t, -1), row(norm_ple_g[i]), w_ple_gate[i].astype(BF16),
                         w_ple_proj[i].astype(BF16), row(norm_final_g), ys, final_norm=(i == depth - 1))
    return h.reshape(bsz, seq, d)
```

```python
import functools

import jax
import jax.numpy as jnp
from jax import lax
from jax.experimental import pallas as pl
from jax.experimental.pallas import tpu as pltpu

NORM_EPS = 1e-6
HEAD_DIM = 128
CONV_KERNEL = 3
N_GROUPS = 8
TOPK_GROUPS = 4
TOP_K = 8
ROUTED_SCALE = 2.5
LOG2_E = 1.4426950408889634
F32_EXP2_UNDERFLOW = -160.0

V7X_VMEM_BYTES = 64 * 1024 * 1024
V7X_SUBLANES = 8
V7X_LANES = 128
DMA_THREADS = 2

F32 = jnp.float32
BF16 = jnp.bfloat16
I32 = jnp.int32
U32 = jnp.uint32

ATTN_BLOCK = 256
EXPERT_TILE = 512
EXPERT_TILE_LOG2 = 9
assert 1 << EXPERT_TILE_LOG2 == EXPERT_TILE


def _vmem_limit(pipelined_bytes, resident_bytes):
    want = 2 * pipelined_bytes + resident_bytes
    return int(min(want, V7X_VMEM_BYTES - 4 * 1024 * 1024))


def _nbytes(shape, dtype):
    n = 1
    for s in shape:
        n *= s
    return n * jnp.dtype(dtype).itemsize


def _resident(shape, n_grid_axes, n_prefetch=0):
    zeros = (0,) * len(shape)
    return pl.BlockSpec(shape, lambda *_: zeros, pipeline_mode=pl.Buffered(1))


def _rms(y, g):
    return y * lax.rsqrt(jnp.mean(y * y, axis=-1, keepdims=True) + NORM_EPS) * g


def _sigmoid(a):
    return 1.0 / (1.0 + jnp.exp(-a))


def _softplus2(z):
    return jnp.maximum(z, 0.0) + jnp.log2(1.0 + jnp.exp2(-jnp.abs(z)))


def _dot(a, b):
    return jnp.dot(a, b, preferred_element_type=F32)


def _dot_nt(a, b):
    return lax.dot_general(a, b, (((1,), (1,)), ((), ())), preferred_element_type=F32)


def _swiglu_hidden(x, wg, wu):
    a = _dot(x, wg)
    return a * _sigmoid(a) * _dot(x, wu)


def _pack_rows(x):
    half = x.shape[1] // 2
    as_bits = lambda v: lax.bitcast_convert_type(v.astype(BF16).astype(F32), U32)
    return (as_bits(x[:, :half]) >> 16) | (as_bits(x[:, half:]) & jnp.uint32(0xFFFF0000))


def _unpack_rows(w):
    lo = lax.bitcast_convert_type(w << 16, F32)
    hi = lax.bitcast_convert_type(w & jnp.uint32(0xFFFF0000), F32)
    return jnp.concatenate([lo, hi], axis=1)


def _slab_shape(words):
    return (words // V7X_LANES, V7X_LANES)


def _store_slabs(ref, words):
    by_chunk = jnp.stack([words[:, j * V7X_LANES:(j + 1) * V7X_LANES] for j in range(ref.shape[1])], axis=0)
    ref[...] = pltpu.einshape("jrl->rjl", by_chunk)


def _load_slabs(ref):
    by_chunk = pltpu.einshape("rjl->jrl", ref[...])
    return jnp.concatenate([by_chunk[j] for j in range(ref.shape[1])], axis=1)


def _norm_inproj_kernel(x_ref, g_ref, w_ref, cs_ref, wvt_ref, o_ref, vt_ref, hn_ref, *, n_col_tiles):
    j = pl.program_id(1)

    @pl.when(j == 0)
    def _():
        hn_ref[...] = _rms(x_ref[...], g_ref[...]).astype(hn_ref.dtype)

    @pl.when(j < n_col_tiles)
    def _():
        o_ref[...] = (_dot(hn_ref[...], w_ref[...]) * cs_ref[...]).astype(o_ref.dtype)

    @pl.when(j == n_col_tiles)
    def _():
        vt = _dot_nt(wvt_ref[...], hn_ref[...]).astype(vt_ref.dtype)
        kb = vt_ref.shape[2]
        for c in range(vt_ref.shape[0]):
            vt_ref[c] = vt[:, c * kb:(c + 1) * kb]


def _norm_inproj(h, g, w, col_scale, wvt, *, key_block, tm=1024, tn=1024):
    t, d = h.shape
    n = w.shape[1]
    vw = wvt.shape[0]
    n_col_tiles = n // tn
    last = n_col_tiles - 1
    blocks = _nbytes((tm, d), F32) + _nbytes((d, tn), BF16) + _nbytes((tm, tn), BF16) + _nbytes((vw, tm), BF16)
    temps = _nbytes((tm, d), BF16) + 2 * _nbytes((tm, d), F32) + _nbytes((tm, tn), F32) + _nbytes((vw, d), BF16)
    return pl.pallas_call(
        functools.partial(_norm_inproj_kernel, n_col_tiles=n_col_tiles),
        grid=(t // tm, n_col_tiles + 1),
        in_specs=[
            pl.BlockSpec((tm, d), lambda i, j: (i, 0)),
            _resident((1, d), 2),
            pl.BlockSpec((d, tn), lambda i, j: (0, jnp.minimum(j, last))),
            pl.BlockSpec((1, tn), lambda i, j: (0, jnp.minimum(j, last))),
            _resident((vw, d), 2),
        ],
        out_specs=[
            pl.BlockSpec((tm, tn), lambda i, j: (i, jnp.minimum(j, last))),
            pl.BlockSpec((tm // key_block, vw, key_block), lambda i, j: (i, 0, 0)),
        ],
        out_shape=[jax.ShapeDtypeStruct((t, n), BF16), jax.ShapeDtypeStruct((t // key_block, vw, key_block), BF16)],
        scratch_shapes=[pltpu.VMEM((tm, d), BF16)],
        compiler_params=pltpu.CompilerParams(
            dimension_semantics=("arbitrary", "arbitrary"),
            vmem_limit_bytes=_vmem_limit(blocks, temps),
        ),
        name="norm_inproj",
    )(h, g, w, col_scale, wvt)


def _sb_attn_kernel(q_ref, k_ref, vt_ref, lo_ref, o_ref, acc_ref, run_ref, *, blk, heads):
    qi = pl.program_id(2)
    lo = lo_ref[...]
    acc_ref[...] = jnp.zeros_like(acc_ref)
    run_ref[...] = jnp.zeros_like(run_ref)

    def block(j, masked):
        start = pl.multiple_of(j * blk, blk)
        cols = [slice(h * HEAD_DIM, (h + 1) * HEAD_DIM) for h in range(heads)]
        if masked:
            causal = lax.broadcasted_iota(I32, (blk, blk), 0) < lax.broadcasted_iota(I32, (blk, blk), 1)
        zs = [_dot_nt(k_ref[0, pl.ds(start, blk), c], q_ref[0, :, c]) for c in cols]
        sps = [_softplus2(z) for z in zs]
        keeps = [jnp.where(causal, sp, 0.0) if masked else sp for sp in sps]
        css = [_dot(lo, keep.astype(BF16)) for keep in keeps]
        for h in range(heads):
            run = run_ref[h]
            run_all = jnp.concatenate([run] * (blk // V7X_SUBLANES), axis=0)
            a = jnp.exp2(zs[h] - sps[h] + css[h][:blk] + run_all)
            if masked:
                a = jnp.where(causal, a, 0.0)
            acc_ref[h] += _dot(vt_ref[j, cols[h], :], a.astype(BF16))
            run_ref[h] = run + css[h][blk:]

    block(qi, True)

    def still_visible():
        return jnp.max(run_ref[...]) > F32_EXP2_UNDERFLOW

    def body(carry):
        n, _ = carry
        block(qi - 1 - n, False)
        return n + 1, still_visible()

    lax.while_loop(lambda carry: (carry[0] < qi) & carry[1], body, (jnp.int32(0), still_visible()))
    for h in range(heads):
        o_ref[0, :, h * HEAD_DIM:(h + 1) * HEAD_DIM] = acc_ref[h].T.astype(o_ref.dtype)


def _sb_attention(proj3, vt, *, conv_width, sb_width, blk, heads=8):
    b, s, _ = proj3.shape
    gw = heads * HEAD_DIM
    groups = sb_width // gw
    q_off = 3 * conv_width // gw
    k_off = q_off + groups
    later = lax.broadcasted_iota(I32, (blk, blk), 1) > lax.broadcasted_iota(I32, (blk, blk), 0)
    lo = -jnp.concatenate([later.astype(BF16), jnp.ones((V7X_SUBLANES, blk), BF16)], axis=0)
    blocks = 2 * _nbytes((blk, gw), BF16) + 2 * _nbytes((s, gw), BF16)
    temps = heads * 8 * _nbytes((blk, blk), F32)
    return pl.pallas_call(
        functools.partial(_sb_attn_kernel, blk=blk, heads=heads),
        grid=(b, groups, s // blk),
        in_specs=[
            pl.BlockSpec((1, blk, gw), lambda bi, gi, qi: (bi, qi, q_off + gi)),
            pl.BlockSpec((1, s, gw), lambda bi, gi, qi: (bi, 0, k_off + gi)),
            pl.BlockSpec((s // blk, gw, blk), lambda bi, gi, qi: (bi, gi, 0)),
            _resident((blk + V7X_SUBLANES, blk), 3),
        ],
        out_specs=pl.BlockSpec((1, blk, gw), lambda bi, gi, qi: (bi, qi, gi)),
        out_shape=jax.ShapeDtypeStruct((b, s, sb_width), BF16),
        scratch_shapes=[pltpu.VMEM((heads, HEAD_DIM, blk), F32), pltpu.VMEM((heads, V7X_SUBLANES, blk), F32)],
        compiler_params=pltpu.CompilerParams(
            dimension_semantics=("arbitrary", "arbitrary", "arbitrary"),
            vmem_limit_bytes=_vmem_limit(blocks, temps),
        ),
        name="sb_attn",
    )(proj3, proj3, vt, lo)


def _rank_of(vals):
    n = vals.shape[0]
    idx = lax.broadcasted_iota(I32, vals.shape, 0)
    rank = jnp.zeros(vals.shape, F32)
    for other in range(n):
        o = vals[other:other + 1, :]
        rank = rank + jnp.where(o > vals, 1.0, 0.0) + jnp.where(o == vals, jnp.where(idx > other, 1.0, 0.0), 0.0)
    return rank


def _router(logits_t, bias):
    n_exp, tm = logits_t.shape
    per_group = n_exp // N_GROUPS
    scores = _sigmoid(logits_t)
    biased = scores + bias

    grouped = biased.reshape(N_GROUPS, per_group, tm)
    top1 = jnp.max(grouped, axis=1, keepdims=True)
    n_top = jnp.sum(jnp.where(grouped == top1, 1.0, 0.0), axis=1, keepdims=True)
    below = jnp.max(jnp.where(grouped < top1, grouped, -jnp.inf), axis=1, keepdims=True)
    group_score = (top1 + jnp.where(n_top >= 2.0, top1, below)).reshape(N_GROUPS, tm)

    group_ok = _rank_of(group_score) < float(TOPK_GROUPS)
    expert_ok = jnp.broadcast_to(group_ok.reshape(N_GROUPS, 1, tm), (N_GROUPS, per_group, tm)).reshape(n_exp, tm)
    chosen = jnp.where(_rank_of(jnp.where(expert_ok, biased, -jnp.inf)) < float(TOP_K), 1.0, 0.0)
    w = chosen * scores
    return chosen, w / jnp.sum(w, axis=0, keepdims=True) * ROUTED_SCALE


def _mixer_out_kernel(ch_ref, cb_ref, cc_ref, hh_ref, hc_ref, sb_ref, x_ref, cw_ref, gc_ref, gs_ref,
                      wo_ref, gf_ref, wr_ref, rb_ref, tri_ref, low_ref, sg_ref, su_ref, sd_ref,
                      h1_ref, xt_ref, eid_ref, rank_ref, gate_ref, cnt_ref, *, tiles_per_seq):
    i = pl.program_id(0)
    u = cc_ref[...].astype(F32) * ch_ref[...].astype(F32)
    halo = hc_ref[...].astype(F32) * hh_ref[...].astype(F32)
    halo = jnp.where(i % tiles_per_seq == 0, 0.0, halo)
    prev1 = halo[V7X_SUBLANES - 1:V7X_SUBLANES, :]
    prev2 = halo[V7X_SUBLANES - 2:V7X_SUBLANES - 1, :]
    row = lax.broadcasted_iota(I32, u.shape, 0)
    u1 = jnp.where(row == 0, prev1, pltpu.roll(u, 1, 0))
    u2 = jnp.where(row == 0, prev2, jnp.where(row == 1, prev1, pltpu.roll(u, 2, 0)))
    cw = cw_ref[...]
    conv = cb_ref[...].astype(F32) * (cw[0:1, :] * u2 + cw[1:2, :] * u1 + cw[2:3, :] * u)

    y = jnp.concatenate([_rms(conv, gc_ref[...]), _rms(sb_ref[...].astype(F32), gs_ref[...])], axis=-1)
    h1 = x_ref[...] + _dot(y.astype(BF16), wo_ref[...])
    xt = _rms(h1, gf_ref[...])
    _store_slabs(xt_ref, _pack_rows(xt))
    xt_hi = xt.astype(BF16)
    shared = _swiglu_hidden(xt_hi, sg_ref[...], su_ref[...])
    h1_ref[...] = h1 + _dot(shared.astype(BF16), sd_ref[...])

    n_exp = wr_ref.shape[1] // 2
    xt_lo = (xt - xt_hi.astype(F32)).astype(BF16)
    by_hi = _dot(xt_hi, wr_ref[...])
    logits = by_hi[:, :n_exp] + by_hi[:, n_exp:] + _dot(xt_lo, wr_ref[:, :n_exp])
    chosen, gates = _router(logits.T, rb_ref[...])
    tm = chosen.shape[1]

    @pl.when(i == 0)
    def _():
        cnt_ref[...] = jnp.zeros_like(cnt_ref)

    counts = _dot(chosen.astype(BF16), tri_ref[...])
    seen = cnt_ref[...]
    rank = seen + counts[:, :tm]
    cnt_ref[...] = seen + counts[:, tm:]

    slot = _dot(low_ref[...], chosen.astype(BF16))
    expert = lax.broadcasted_iota(I32, chosen.shape, 0).astype(F32)
    eids, ranks, gsel = [], [], []
    for k in range(TOP_K):
        pick = chosen * jnp.where(slot == float(k), 1.0, 0.0)
        eids.append(jnp.sum(pick * expert, axis=0, keepdims=True))
        ranks.append(jnp.sum(pick * rank, axis=0, keepdims=True))
        gsel.append(jnp.sum(pick * gates, axis=0, keepdims=True))
    eid_ref[...] = jnp.concatenate(eids, axis=0).astype(I32)
    rank_ref[...] = jnp.concatenate(ranks, axis=0).astype(I32)
    gate_ref[...] = jnp.concatenate(gsel, axis=0).T


def _mixer_out(proj, sb, h, conv_w, gc, gs, w_out, gf, w_router, router_bias, sg, su, sd, *, seq, conv_width,
               tm=256):
    t, d = h.shape
    sb_width = sb.shape[1]
    n_exp = w_router.shape[1]
    router_hi = w_router.astype(BF16)
    router_lo = (w_router - router_hi.astype(F32)).astype(BF16)
    w_router = jnp.concatenate([router_hi, router_lo], axis=1)
    halo_blocks = tm // V7X_SUBLANES
    conv_spec = lambda c: pl.BlockSpec((tm, conv_width), lambda i: (i, c))
    halo_spec = lambda c: pl.BlockSpec((V7X_SUBLANES, conv_width), lambda i: (jnp.maximum(i * halo_blocks - 1, 0), c))
    earlier = lax.broadcasted_iota(I32, (tm, tm), 0) < lax.broadcasted_iota(I32, (tm, tm), 1)
    tri = jnp.concatenate([earlier.astype(BF16), jnp.ones((tm, tm), BF16)], axis=1)
    lower = (lax.broadcasted_iota(I32, (n_exp, n_exp), 1) < lax.broadcasted_iota(I32, (n_exp, n_exp), 0)).astype(BF16)
    blocks = (3 * _nbytes((tm, conv_width), BF16) + _nbytes((tm, sb_width), BF16) + 3 * _nbytes((tm, d), F32))
    temps = (10 * _nbytes((tm, d), F32) + _nbytes(w_out.shape, BF16) + _nbytes(w_router.shape, BF16)
             + 3 * _nbytes(sg.shape, BF16))
    row_block = pl.BlockSpec((tm, d), lambda i: (i, 0))
    slot_block = pl.BlockSpec((TOP_K, tm), lambda i: (0, i))
    return pl.pallas_call(
        functools.partial(_mixer_out_kernel, tiles_per_seq=seq // tm),
        grid=(t // tm,),
        in_specs=[
            conv_spec(0), conv_spec(1), conv_spec(2), halo_spec(0), halo_spec(2),
            pl.BlockSpec((tm, sb_width), lambda i: (i, 0)),
            row_block,
            _resident(conv_w.shape, 1), _resident(gc.shape, 1), _resident(gs.shape, 1), _resident(w_out.shape, 1),
            _resident(gf.shape, 1), _resident(w_router.shape, 1), _resident(router_bias.shape, 1),
            _resident(tri.shape, 1), _resident(lower.shape, 1),
            _resident(sg.shape, 1), _resident(su.shape, 1), _resident(sd.shape, 1),
        ],
        out_specs=[
            row_block, pl.BlockSpec((tm,) + _slab_shape(d // 2), lambda i: (i, 0, 0)), slot_block, slot_block,
            pl.BlockSpec((tm, TOP_K), lambda i: (i, 0)),
            pl.BlockSpec((n_exp, tm), lambda i: (0, 0)),
        ],
        out_shape=[
            jax.ShapeDtypeStruct((t, d), F32),
            jax.ShapeDtypeStruct((t,) + _slab_shape(d // 2), U32),
            jax.ShapeDtypeStruct((TOP_K, t), I32),
            jax.ShapeDtypeStruct((TOP_K, t), I32),
            jax.ShapeDtypeStruct((t, TOP_K), F32),
            jax.ShapeDtypeStruct((n_exp, tm), F32),
        ],
        compiler_params=pltpu.CompilerParams(
            dimension_semantics=("arbitrary",),
            vmem_limit_bytes=_vmem_limit(blocks, temps),
        ),
        name="mixer_out",
    )(proj, proj, proj, proj, proj, sb, h, conv_w, gc, gs, w_out, gf, w_router, router_bias, tri, lower, sg, su, sd)


def _moe_layout_kernel(cnt_ref, eid_ref, rank_ref, pos_ref, off_ref, texp_ref, trows_ref, nused_ref, *, n_exp,
                       n_tiles):
    def layout(e, off):
        off_ref[e] = off
        tiles = (cnt_ref[e] + EXPERT_TILE - 1) >> EXPERT_TILE_LOG2
        first = off >> EXPERT_TILE_LOG2

        def mark(j, c):
            texp_ref[first + j] = e
            trows_ref[first + j] = jnp.minimum(cnt_ref[e] - (j << EXPERT_TILE_LOG2), EXPERT_TILE)
            return c

        lax.fori_loop(0, tiles, mark, 0)
        return off + (tiles << EXPERT_TILE_LOG2)

    total = lax.fori_loop(0, n_exp, layout, 0)
    off_ref[n_exp] = total
    used_tiles = total >> EXPERT_TILE_LOG2
    nused_ref[0] = used_tiles

    def tail(j, c):
        texp_ref[j] = n_exp - 1
        trows_ref[j] = 0
        return c

    lax.fori_loop(used_tiles, n_tiles, tail, 0)

    eid = eid_ref[...]
    pos = rank_ref[...]
    for e in range(n_exp):
        pos = pos + jnp.where(eid == e, off_ref[e], 0)
    pos_ref[...] = pos


def _moe_layout(counts, eid, rank):
    n_exp = counts.shape[0]
    t = eid.shape[1]
    n_tiles = t * TOP_K // EXPERT_TILE + n_exp
    whole = lambda shape: pl.BlockSpec(shape, lambda i, cnt: (0,) * len(shape))
    whole_smem = lambda n: pl.BlockSpec((n,), lambda i, cnt: (0,), memory_space=pltpu.SMEM)
    return pl.pallas_call(
        functools.partial(_moe_layout_kernel, n_exp=n_exp, n_tiles=n_tiles),
        grid_spec=pltpu.PrefetchScalarGridSpec(
            num_scalar_prefetch=1,
            grid=(1,),
            in_specs=[whole(eid.shape), whole(rank.shape)],
            out_specs=[whole(eid.shape), whole_smem(n_exp + 1), whole_smem(n_tiles), whole_smem(n_tiles),
                       whole_smem(1)],
        ),
        out_shape=[
            jax.ShapeDtypeStruct(eid.shape, I32),
            jax.ShapeDtypeStruct((n_exp + 1,), I32),
            jax.ShapeDtypeStruct((n_tiles,), I32),
            jax.ShapeDtypeStruct((n_tiles,), I32),
            jax.ShapeDtypeStruct((1,), I32),
        ],
        compiler_params=pltpu.CompilerParams(
            dimension_semantics=("arbitrary",),
            vmem_limit_bytes=_vmem_limit(3 * _nbytes(eid.shape, I32), 4 * _nbytes(eid.shape, I32)),
        ),
        name="moe_layout",
    )(counts, eid, rank)


def _moe_dispatch_kernel(cnt_ref, off_ref, pos_ref, xt_ref, wg_ref, wu_ref, wd_ref, xs_hbm, wg_bf, wu_bf, wd_bf,
                         zero_ref, row_sem, pad_sem, *, n_exp, n_row_steps, n_assigned):
    ts = xt_ref.shape[0]
    has_rows = pl.program_id(0) < n_row_steps

    @pl.when(pl.program_id(0) == 0)
    def _():
        zero_ref[...] = jnp.zeros_like(zero_ref)

        def pad(e, c):
            def zero_row(r, c):
                pltpu.make_async_copy(zero_ref.at[0], xs_hbm.at[r], pad_sem).start()
                return c

            lax.fori_loop(off_ref[e] + cnt_ref[e], off_ref[e + 1], zero_row, 0)
            return c

        def drain_row(r, c):
            pltpu.make_async_copy(zero_ref.at[0], xs_hbm.at[0], pad_sem).wait()
            return c

        lax.fori_loop(0, n_exp, pad, 0)
        lax.fori_loop(n_assigned, off_ref[n_exp], drain_row, 0)

    def token_group(g, c):
        base = pl.multiple_of(g * V7X_SUBLANES, V7X_SUBLANES)
        for u in range(V7X_SUBLANES):
            t = base + u
            for k in range(TOP_K):
                pltpu.make_async_copy(xt_ref.at[t], xs_hbm.at[pos_ref[k, t]], row_sem).start(priority=k % DMA_THREADS)
        return c

    @pl.when(has_rows)
    def _():
        lax.fori_loop(0, ts // V7X_SUBLANES, token_group, 0)

    @pl.when(pl.program_id(0) < n_exp)
    def _():
        wg_bf[...] = wg_ref[...].astype(BF16)
        wu_bf[...] = wu_ref[...].astype(BF16)
        wd_bf[...] = wd_ref[...].astype(BF16)

    @pl.when(has_rows)
    def _():
        for k in range(TOP_K):
            pltpu.make_async_copy(xt_ref, xs_hbm.at[pl.ds(0, ts)], row_sem).wait()


def _moe_dispatch(counts, off, pos, xt, wg, wu, wd, *, ts=512):
    t = xt.shape[0]
    slab = xt.shape[1:]
    n_exp, d, f = wg.shape
    n_tiles = t * TOP_K // EXPERT_TILE + n_exp
    n_row_steps = t // ts
    last_rows = n_row_steps - 1
    blocks = _nbytes((ts,) + slab, xt.dtype) + 3 * _nbytes((d, f), wg.dtype) + 3 * _nbytes((d, f), BF16)
    temps = _nbytes((1,) + slab, xt.dtype) + 2 * _nbytes((d, f), F32)
    expert_block = lambda shape: pl.BlockSpec(shape, lambda i, cnt, off: (jnp.minimum(i, n_exp - 1), 0, 0))
    return pl.pallas_call(
        functools.partial(_moe_dispatch_kernel, n_exp=n_exp, n_row_steps=n_row_steps, n_assigned=t * TOP_K),
        grid_spec=pltpu.PrefetchScalarGridSpec(
            num_scalar_prefetch=2,
            grid=(max(n_row_steps, n_exp),),
            in_specs=[
                pl.BlockSpec((TOP_K, ts), lambda i, cnt, off: (0, jnp.minimum(i, last_rows)), memory_space=pltpu.SMEM),
                pl.BlockSpec((ts,) + slab, lambda i, cnt, off: (jnp.minimum(i, last_rows), 0, 0)),
                expert_block((1, d, f)), expert_block((1, d, f)), expert_block((1, f, d)),
            ],
            out_specs=[pl.BlockSpec(memory_space=pl.ANY),
                       expert_block((1, d, f)), expert_block((1, d, f)), expert_block((1, f, d))],
            scratch_shapes=[
                pltpu.VMEM((1,) + slab, xt.dtype),
                pltpu.SemaphoreType.DMA(()),
                pltpu.SemaphoreType.DMA(()),
            ],
        ),
        out_shape=[jax.ShapeDtypeStruct((n_tiles * EXPERT_TILE,) + slab, xt.dtype),
                   jax.ShapeDtypeStruct(wg.shape, BF16), jax.ShapeDtypeStruct(wu.shape, BF16),
                   jax.ShapeDtypeStruct(wd.shape, BF16)],
        compiler_params=pltpu.CompilerParams(
            dimension_semantics=("arbitrary",),
            vmem_limit_bytes=_vmem_limit(blocks, temps),
        ),
        name="moe_dispatch",
    )(counts, off, pos, xt, wg, wu, wd)


def _moe_experts_kernel(texp_ref, trows_ref, nused_ref, xs_ref, wg_ref, wu_ref, wd_ref, ys_ref):
    occupied = trows_ref[pl.program_id(0)]

    def experts_on(rows):
        x = _unpack_rows(_load_slabs(xs_ref.at[pl.ds(0, rows)])).astype(BF16)
        hid = _swiglu_hidden(x, wg_ref[0], wu_ref[0])
        _store_slabs(ys_ref.at[pl.ds(0, rows)], _pack_rows(_dot(hid.astype(BF16), wd_ref[0])))
        if rows < EXPERT_TILE:
            ys_ref[pl.ds(rows, EXPERT_TILE - rows)] = jnp.zeros((EXPERT_TILE - rows,) + ys_ref.shape[1:], ys_ref.dtype)

    lower = 0
    for rows in (EXPERT_TILE // 4, EXPERT_TILE // 2, EXPERT_TILE):
        @pl.when((occupied > lower) & (occupied <= rows))
        def _(rows=rows):
            experts_on(rows)
        lower = rows

    @pl.when(occupied == 0)
    def _():
        ys_ref[...] = jnp.zeros_like(ys_ref)


def _moe_experts(texp, trows, nused, xs, wg, wu, wd):
    rows = xs.shape[0]
    slab = xs.shape[1:]
    d, f = wg.shape[1:]
    n_tiles = rows // EXPERT_TILE
    blocks = 2 * _nbytes((EXPERT_TILE,) + slab, U32) + 3 * _nbytes((d, f), wg.dtype)
    temps = 4 * _nbytes((EXPERT_TILE, f), F32) + 2 * _nbytes((EXPERT_TILE, d), F32) + _nbytes((EXPERT_TILE, d), BF16)
    expert_block = lambda shape: pl.BlockSpec(shape, lambda i, texp, trows, nused: (texp[i], 0, 0))
    return pl.pallas_call(
        _moe_experts_kernel,
        grid_spec=pltpu.PrefetchScalarGridSpec(
            num_scalar_prefetch=3,
            grid=(n_tiles,),
            in_specs=[
                pl.BlockSpec((EXPERT_TILE,) + slab,
                             lambda i, texp, trows, nused: (jnp.minimum(i, nused[0] - 1), 0, 0)),
                expert_block((1, d, f)), expert_block((1, d, f)), expert_block((1, f, d)),
            ],
            out_specs=pl.BlockSpec((EXPERT_TILE,) + slab, lambda i, texp, trows, nused: (i, 0, 0)),
        ),
        out_shape=jax.ShapeDtypeStruct((rows,) + slab, U32),
        compiler_params=pltpu.CompilerParams(
            dimension_semantics=("arbitrary",),
            vmem_limit_bytes=_vmem_limit(blocks, temps),
        ),
        name="moe_experts",
    )(texp, trows, nused, xs, wg, wu, wd)


GATHER_DEPTH = 3


def _moe_combine_kernel(pos_ref, pos_1_ref, pos_2_ref, gate_ref, h1_ref, p_ref, gp_ref, wg_ref, wp_ref, gf_ref,
                        ys_hbm, o_ref, buf_0, buf_1, buf_2, sem, *, final_norm):
    tc = h1_ref.shape[0]
    i = pl.program_id(0)
    bufs = (buf_0, buf_1, buf_2)

    def wait_rows(b):
        for k in range(TOP_K):
            pltpu.make_async_copy(ys_hbm.at[pl.ds(0, tc)], bufs[b].at[k], sem.at[b]).wait()

    @pl.when(i == 0)
    def _():
        def token_group(g, c):
            base = pl.multiple_of(g * V7X_SUBLANES, V7X_SUBLANES)
            for u in range(V7X_SUBLANES):
                for k in range(TOP_K):
                    t = base + u
                    pltpu.make_async_copy(ys_hbm.at[pos_ref[k, t]], buf_0.at[k, t], sem.at[0]).start()
                    pltpu.make_async_copy(ys_hbm.at[pos_1_ref[k, t]], buf_1.at[k, t], sem.at[1]).start()
            return c

        lax.fori_loop(0, tc // V7X_SUBLANES, token_group, 0)

    def step(cur):
        ahead = (cur + 2) % GATHER_DEPTH
        wait_rows(cur)
        for t in range(tc):
            for k in range(TOP_K):
                pltpu.make_async_copy(ys_hbm.at[pos_2_ref[k, t]], bufs[ahead].at[k, t],
                                      sem.at[ahead]).start(priority=k % DMA_THREADS)

        h2 = h1_ref[...]
        emb = _dot(p_ref[...].astype(BF16), wp_ref[...])
        gate = gate_ref[...]
        for k in range(TOP_K):
            h2 = h2 + gate[:, k:k + 1] * _unpack_rows(_load_slabs(bufs[cur].at[k]))

        gate_ple = _sigmoid(_dot(_rms(h2, gp_ref[...]).astype(BF16), wg_ref[...]))
        h3 = h2 + gate_ple * emb
        o_ref[...] = _rms(h3, gf_ref[...]) if final_norm else h3

        @pl.when(i + 1 == pl.num_programs(0))
        def _():
            wait_rows((cur + 1) % GATHER_DEPTH)
            wait_rows(ahead)

    for b in range(GATHER_DEPTH):
        @pl.when(i % GATHER_DEPTH == b)
        def _(b=b):
            step(b)


def _moe_combine(pos, gate, h1, p, gp, wg, wp, gf, ys, *, final_norm, tc=256):
    t, d = h1.shape
    pd = p.shape[1]
    slab = ys.shape[1:]
    row_block = pl.BlockSpec((tc, d), lambda i: (i, 0))
    blocks = 2 * _nbytes((tc, d), F32) + _nbytes((tc, pd), F32) + _nbytes((tc, V7X_LANES), F32)
    temps = (GATHER_DEPTH * _nbytes((TOP_K, tc) + slab, U32) + 8 * _nbytes((tc, d), F32) + _nbytes(wg.shape, BF16)
             + _nbytes(wp.shape, BF16))
    last = t // tc - 1
    slots_of_step = lambda j: pl.BlockSpec((TOP_K, tc), lambda i: (0, jnp.minimum(i + j, last)),
                                           memory_space=pltpu.SMEM)
    return pl.pallas_call(
        functools.partial(_moe_combine_kernel, final_norm=final_norm),
        grid=(t // tc,),
        in_specs=[
            slots_of_step(0), slots_of_step(1), slots_of_step(2),
            pl.BlockSpec((tc, TOP_K), lambda i: (i, 0)),
            row_block,
            pl.BlockSpec((tc, pd), lambda i: (i, 0)),
            _resident(gp.shape, 1), _resident(wg.shape, 1), _resident(wp.shape, 1), _resident(gf.shape, 1),
            pl.BlockSpec(memory_space=pl.ANY),
        ],
        out_specs=row_block,
        out_shape=jax.ShapeDtypeStruct((t, d), F32),
        scratch_shapes=[pltpu.VMEM((TOP_K, tc) + slab, U32)] * GATHER_DEPTH
                       + [pltpu.SemaphoreType.DMA((GATHER_DEPTH,))],
        compiler_params=pltpu.CompilerParams(
            dimension_semantics=("arbitrary",),
            vmem_limit_bytes=_vmem_limit(blocks, temps),
        ),
        name="moe_combine",
    )(pos, pos, pos, gate, h1, p, gp, wg, wp, gf, ys)


def kernel(x, p, norm_mix_g, w_in, conv_w, gnorm_conv_g, gnorm_sb_g, w_out, norm_ffn_g, w_router, router_bias,
           w_exp_gate, w_exp_up, w_exp_down, w_sh_gate, w_sh_up, w_sh_down, norm_ple_g, w_ple_gate, w_ple_proj,
           norm_final_g):
    bsz, seq, d = x.shape
    depth = p.shape[0]
    t = bsz * seq
    conv_width = conv_w.shape[-1]
    sb_width = gnorm_sb_g.shape[-1]
    n_qk = 3 * conv_width + 2 * sb_width
    assert w_in.shape[-1] == n_qk + sb_width
    assert w_router.shape[-1] % N_GROUPS == 0

    q_lo = 3 * conv_width
    col = jnp.arange(n_qk)
    q_scale = HEAD_DIM ** -0.5 * LOG2_E
    col_scale = jnp.where((col >= q_lo) & (col < q_lo + sb_width), q_scale, 1.0).astype(F32)[None, :]
    row = lambda v: v.astype(F32)[None, :]

    h = x.reshape(t, d)
    for i in range(depth):
        w_in_bf = w_in[i].astype(BF16)
        proj, vt = _norm_inproj(h, row(norm_mix_g[i]), w_in_bf[:, :n_qk], col_scale, w_in_bf[:, n_qk:].T,
                                key_block=ATTN_BLOCK)
        sb = _sb_attention(proj.reshape(bsz, seq, n_qk), vt, conv_width=conv_width, sb_width=sb_width,
                           blk=ATTN_BLOCK)
        h1, xt, eid, rank, gate, cnt = _mixer_out(
            proj, sb.reshape(t, sb_width), h, conv_w[i], row(gnorm_conv_g[i]), row(gnorm_sb_g[i]),
            w_out[i].astype(BF16), row(norm_ffn_g[i]), w_router[i], router_bias[i].astype(F32)[:, None],
            w_sh_gate[i].astype(BF16), w_sh_up[i].astype(BF16), w_sh_down[i].astype(BF16),
            seq=seq, conv_width=conv_width)
        counts = cnt[:, 0].astype(I32)
        pos, off, texp, trows, nused = _moe_layout(counts, eid, rank)
        xs, wg_bf, wu_bf, wd_bf = _moe_dispatch(counts, off, pos, xt, w_exp_gate[i], w_exp_up[i], w_exp_down[i])
        ys = _moe_experts(texp, trows, nused, xs, wg_bf, wu_bf, wd_bf)
        h = _moe_combine(pos, gate, h1, p[i].reshape(t, -1), row(norm_ple_g[i]), w_ple_gate[i].astype(BF16),
                         w_ple_proj[i].astype(BF16), row(norm_final_g), ys, final_norm=(i == depth - 1))
    return h.reshape(bsz, seq, d)
```

```python
import functools

import jax
import jax.numpy as jnp
from jax import lax
from jax.experimental import pallas as pl
from jax.experimental.pallas import tpu as pltpu

NORM_EPS = 1e-6
HEAD_DIM = 128
CONV_KERNEL = 3
N_GROUPS = 8
TOPK_GROUPS = 4
TOP_K = 8
ROUTED_SCALE = 2.5
LOG2_E = 1.4426950408889634
F32_EXP2_UNDERFLOW = -160.0

V7X_VMEM_BYTES = 64 * 1024 * 1024
V7X_SUBLANES = 8
V7X_LANES = 128
DMA_THREADS = 2

F32 = jnp.float32
BF16 = jnp.bfloat16
I32 = jnp.int32
U32 = jnp.uint32

ATTN_BLOCK = 256
EXPERT_TILE = 512
EXPERT_TILE_LOG2 = 9
assert 1 << EXPERT_TILE_LOG2 == EXPERT_TILE


def _vmem_limit(pipelined_bytes, resident_bytes):
    want = 2 * pipelined_bytes + resident_bytes
    return int(min(want, V7X_VMEM_BYTES - 4 * 1024 * 1024))


def _nbytes(shape, dtype):
    n = 1
    for s in shape:
        n *= s
    return n * jnp.dtype(dtype).itemsize


def _resident(shape, n_grid_axes, n_prefetch=0):
    zeros = (0,) * len(shape)
    return pl.BlockSpec(shape, lambda *_: zeros, pipeline_mode=pl.Buffered(1))


def _rms(y, g):
    return y * lax.rsqrt(jnp.mean(y * y, axis=-1, keepdims=True) + NORM_EPS) * g


def _sigmoid(a):
    return 1.0 / (1.0 + jnp.exp(-a))


def _softplus2(z):
    return jnp.maximum(z, 0.0) + jnp.log2(1.0 + jnp.exp2(-jnp.abs(z)))


def _dot(a, b):
    return jnp.dot(a, b, preferred_element_type=F32)


def _dot_nt(a, b):
    return lax.dot_general(a, b, (((1,), (1,)), ((), ())), preferred_element_type=F32)


def _swiglu_hidden(x, wg, wu):
    a = _dot(x, wg)
    return a * _sigmoid(a) * _dot(x, wu)


def _pack_rows(x):
    half = x.shape[1] // 2
    as_bits = lambda v: lax.bitcast_convert_type(v.astype(BF16).astype(F32), U32)
    return (as_bits(x[:, :half]) >> 16) | (as_bits(x[:, half:]) & jnp.uint32(0xFFFF0000))


def _unpack_rows(w):
    lo = lax.bitcast_convert_type(w << 16, F32)
    hi = lax.bitcast_convert_type(w & jnp.uint32(0xFFFF0000), F32)
    return jnp.concatenate([lo, hi], axis=1)


def _slab_shape(words):
    return (words // V7X_LANES, V7X_LANES)


def _store_slabs(ref, words):
    by_chunk = jnp.stack([words[:, j * V7X_LANES:(j + 1) * V7X_LANES] for j in range(ref.shape[1])], axis=0)
    ref[...] = pltpu.einshape("jrl->rjl", by_chunk)


def _load_slabs(ref):
    by_chunk = pltpu.einshape("rjl->jrl", ref[...])
    return jnp.concatenate([by_chunk[j] for j in range(ref.shape[1])], axis=1)


def _norm_inproj_kernel(x_ref, g_ref, w_ref, cs_ref, wvt_ref, o_ref, vt_ref, hn_ref, *, n_col_tiles):
    j = pl.program_id(1)

    @pl.when(j == 0)
    def _():
        hn_ref[...] = _rms(x_ref[...], g_ref[...]).astype(hn_ref.dtype)

    @pl.when(j < n_col_tiles)
    def _():
        o_ref[...] = (_dot(hn_ref[...], w_ref[...]) * cs_ref[...]).astype(o_ref.dtype)

    @pl.when(j == n_col_tiles)
    def _():
        vt = _dot_nt(wvt_ref[...], hn_ref[...]).astype(vt_ref.dtype)
        kb = vt_ref.shape[2]
        for c in range(vt_ref.shape[0]):
            vt_ref[c] = vt[:, c * kb:(c + 1) * kb]


def _norm_inproj(h, g, w, col_scale, wvt, *, key_block, tm=1024, tn=1024):
    t, d = h.shape
    n = w.shape[1]
    vw = wvt.shape[0]
    n_col_tiles = n // tn
    last = n_col_tiles - 1
    blocks = _nbytes((tm, d), F32) + _nbytes((d, tn), BF16) + _nbytes((tm, tn), BF16) + _nbytes((vw, tm), BF16)
    temps = _nbytes((tm, d), BF16) + 2 * _nbytes((tm, d), F32) + _nbytes((tm, tn), F32) + _nbytes((vw, d), BF16)
    return pl.pallas_call(
        functools.partial(_norm_inproj_kernel, n_col_tiles=n_col_tiles),
        grid=(t // tm, n_col_tiles + 1),
        in_specs=[
            pl.BlockSpec((tm, d), lambda i, j: (i, 0)),
            _resident((1, d), 2),
            pl.BlockSpec((d, tn), lambda i, j: (0, jnp.minimum(j, last))),
            pl.BlockSpec((1, tn), lambda i, j: (0, jnp.minimum(j, last))),
            _resident((vw, d), 2),
        ],
        out_specs=[
            pl.BlockSpec((tm, tn), lambda i, j: (i, jnp.minimum(j, last))),
            pl.BlockSpec((tm // key_block, vw, key_block), lambda i, j: (i, 0, 0)),
        ],
        out_shape=[jax.ShapeDtypeStruct((t, n), BF16), jax.ShapeDtypeStruct((t // key_block, vw, key_block), BF16)],
        scratch_shapes=[pltpu.VMEM((tm, d), BF16)],
        compiler_params=pltpu.CompilerParams(
            dimension_semantics=("arbitrary", "arbitrary"),
            vmem_limit_bytes=_vmem_limit(blocks, temps),
        ),
        name="norm_inproj",
    )(h, g, w, col_scale, wvt)


def _sb_attn_kernel(q_ref, k_ref, vt_ref, lo_ref, o_ref, acc_ref, run_ref, *, blk, heads):
    qi = pl.program_id(2)
    lo = lo_ref[...]
    acc_ref[...] = jnp.zeros_like(acc_ref)
    run_ref[...] = jnp.zeros_like(run_ref)

    def block(j, masked):
        start = pl.multiple_of(j * blk, blk)
        cols = [slice(h * HEAD_DIM, (h + 1) * HEAD_DIM) for h in range(heads)]
        if masked:
            causal = lax.broadcasted_iota(I32, (blk, blk), 0) < lax.broadcasted_iota(I32, (blk, blk), 1)
        zs = [_dot_nt(k_ref[0, pl.ds(start, blk), c], q_ref[0, :, c]) for c in cols]
        sps = [_softplus2(z) for z in zs]
        keeps = [jnp.where(causal, sp, 0.0) if masked else sp for sp in sps]
        css = [_dot(lo, keep.astype(BF16)) for keep in keeps]
        for h in range(heads):
            run = run_ref[h]
            run_all = jnp.concatenate([run] * (blk // V7X_SUBLANES), axis=0)
            a = jnp.exp2(zs[h] - sps[h] + css[h][:blk] + run_all)
            if masked:
                a = jnp.where(causal, a, 0.0)
            acc_ref[h] += _dot(vt_ref[j, cols[h], :], a.astype(BF16))
            run_ref[h] = run + css[h][blk:]

    block(qi, True)

    def still_visible():
        return jnp.max(run_ref[...]) > F32_EXP2_UNDERFLOW

    def body(carry):
        n, _ = carry
        block(qi - 1 - n, False)
        return n + 1, still_visible()

    lax.while_loop(lambda carry: (carry[0] < qi) & carry[1], body, (jnp.int32(0), still_visible()))
    for h in range(heads):
        o_ref[0, :, h * HEAD_DIM:(h + 1) * HEAD_DIM] = acc_ref[h].T.astype(o_ref.dtype)


def _sb_attention(proj3, vt, *, conv_width, sb_width, blk, heads=8):
    b, s, _ = proj3.shape
    gw = heads * HEAD_DIM
    groups = sb_width // gw
    q_off = 3 * conv_width // gw
    k_off = q_off + groups
    later = lax.broadcasted_iota(I32, (blk, blk), 1) > lax.broadcasted_iota(I32, (blk, blk), 0)
    lo = -jnp.concatenate([later.astype(BF16), jnp.ones((V7X_SUBLANES, blk), BF16)], axis=0)
    blocks = 2 * _nbytes((blk, gw), BF16) + 2 * _nbytes((s, gw), BF16)
    temps = heads * 8 * _nbytes((blk, blk), F32)
    return pl.pallas_call(
        functools.partial(_sb_attn_kernel, blk=blk, heads=heads),
        grid=(b, groups, s // blk),
        in_specs=[
            pl.BlockSpec((1, blk, gw), lambda bi, gi, qi: (bi, qi, q_off + gi)),
            pl.BlockSpec((1, s, gw), lambda bi, gi, qi: (bi, 0, k_off + gi)),
            pl.BlockSpec((s // blk, gw, blk), lambda bi, gi, qi: (bi, gi, 0)),
            _resident((blk + V7X_SUBLANES, blk), 3),
        ],
        out_specs=pl.BlockSpec((1, blk, gw), lambda bi, gi, qi: (bi, qi, gi)),
        out_shape=jax.ShapeDtypeStruct((b, s, sb_width), BF16),
        scratch_shapes=[pltpu.VMEM((heads, HEAD_DIM, blk), F32), pltpu.VMEM((heads, V7X_SUBLANES, blk), F32)],
        compiler_params=pltpu.CompilerParams(
            dimension_semantics=("arbitrary", "arbitrary", "arbitrary"),
            vmem_limit_bytes=_vmem_limit(blocks, temps),
        ),
        name="sb_attn",
    )(proj3, proj3, vt, lo)


def _rank_of(vals):
    n = vals.shape[0]
    idx = lax.broadcasted_iota(I32, vals.shape, 0)
    rank = jnp.zeros(vals.shape, F32)
    for other in range(n):
        o = vals[other:other + 1, :]
        rank = rank + jnp.where(o > vals, 1.0, 0.0) + jnp.where(o == vals, jnp.where(idx > other, 1.0, 0.0), 0.0)
    return rank


def _router(logits_t, bias):
    n_exp, tm = logits_t.shape
    per_group = n_exp // N_GROUPS
    scores = _sigmoid(logits_t)
    biased = scores + bias

    grouped = biased.reshape(N_GROUPS, per_group, tm)
    top1 = jnp.max(grouped, axis=1, keepdims=True)
    n_top = jnp.sum(jnp.where(grouped == top1, 1.0, 0.0), axis=1, keepdims=True)
    below = jnp.max(jnp.where(grouped < top1, grouped, -jnp.inf), axis=1, keepdims=True)
    group_score = (top1 + jnp.where(n_top >= 2.0, top1, below)).reshape(N_GROUPS, tm)

    group_ok = _rank_of(group_score) < float(TOPK_GROUPS)
    expert_ok = jnp.broadcast_to(group_ok.reshape(N_GROUPS, 1, tm), (N_GROUPS, per_group, tm)).reshape(n_exp, tm)
    chosen = jnp.where(_rank_of(jnp.where(expert_ok, biased, -jnp.inf)) < float(TOP_K), 1.0, 0.0)
    w = chosen * scores
    return chosen, w / jnp.sum(w, axis=0, keepdims=True) * ROUTED_SCALE


def _mixer_out_kernel(ch_ref, cb_ref, cc_ref, hh_ref, hc_ref, sb_ref, x_ref, cw_ref, gc_ref, gs_ref,
                      wo_ref, gf_ref, wr_ref, rb_ref, tri_ref, low_ref, sg_ref, su_ref, sd_ref,
                      h1_ref, xt_ref, eid_ref, rank_ref, gate_ref, cnt_ref, *, tiles_per_seq):
    i = pl.program_id(0)
    u = cc_ref[...].astype(F32) * ch_ref[...].astype(F32)
    halo = hc_ref[...].astype(F32) * hh_ref[...].astype(F32)
    halo = jnp.where(i % tiles_per_seq == 0, 0.0, halo)
    prev1 = halo[V7X_SUBLANES - 1:V7X_SUBLANES, :]
    prev2 = halo[V7X_SUBLANES - 2:V7X_SUBLANES - 1, :]
    row = lax.broadcasted_iota(I32, u.shape, 0)
    u1 = jnp.where(row == 0, prev1, pltpu.roll(u, 1, 0))
    u2 = jnp.where(row == 0, prev2, jnp.where(row == 1, prev1, pltpu.roll(u, 2, 0)))
    cw = cw_ref[...]
    conv = cb_ref[...].astype(F32) * (cw[0:1, :] * u2 + cw[1:2, :] * u1 + cw[2:3, :] * u)

    y = jnp.concatenate([_rms(conv, gc_ref[...]), _rms(sb_ref[...].astype(F32), gs_ref[...])], axis=-1)
    h1 = x_ref[...] + _dot(y.astype(BF16), wo_ref[...])
    xt = _rms(h1, gf_ref[...])
    _store_slabs(xt_ref, _pack_rows(xt))
    xt_hi = xt.astype(BF16)
    shared = _swiglu_hidden(xt_hi, sg_ref[...], su_ref[...])
    h1_ref[...] = h1 + _dot(shared.astype(BF16), sd_ref[...])

    n_exp = wr_ref.shape[1] // 2
    xt_lo = (xt - xt_hi.astype(F32)).astype(BF16)
    by_hi = _dot(xt_hi, wr_ref[...])
    logits = by_hi[:, :n_exp] + by_hi[:, n_exp:] + _dot(xt_lo, wr_ref[:, :n_exp])
    chosen, gates = _router(logits.T, rb_ref[...])
    tm = chosen.shape[1]

    @pl.when(i == 0)
    def _():
        cnt_ref[...] = jnp.zeros_like(cnt_ref)

    counts = _dot(chosen.astype(BF16), tri_ref[...])
    seen = cnt_ref[...]
    rank = seen + counts[:, :tm]
    cnt_ref[...] = seen + counts[:, tm:]

    slot = _dot(low_ref[...], chosen.astype(BF16))
    expert = lax.broadcasted_iota(I32, chosen.shape, 0).astype(F32)
    eids, ranks, gsel = [], [], []
    for k in range(TOP_K):
        pick = chosen * jnp.where(slot == float(k), 1.0, 0.0)
        eids.append(jnp.sum(pick * expert, axis=0, keepdims=True))
        ranks.append(jnp.sum(pick * rank, axis=0, keepdims=True))
        gsel.append(jnp.sum(pick * gates, axis=0, keepdims=True))
    eid_ref[...] = jnp.concatenate(eids, axis=0).astype(I32)
    rank_ref[...] = jnp.concatenate(ranks, axis=0).astype(I32)
    gate_ref[...] = jnp.concatenate(gsel, axis=0).T


def _mixer_out(proj, sb, h, conv_w, gc, gs, w_out, gf, w_router, router_bias, sg, su, sd, *, seq, conv_width,
               tm=256):
    t, d = h.shape
    sb_width = sb.shape[1]
    n_exp = w_router.shape[1]
    router_hi = w_router.astype(BF16)
    router_lo = (w_router - router_hi.astype(F32)).astype(BF16)
    w_router = jnp.concatenate([router_hi, router_lo], axis=1)
    halo_blocks = tm // V7X_SUBLANES
    conv_spec = lambda c: pl.BlockSpec((tm, conv_width), lambda i: (i, c))
    halo_spec = lambda c: pl.BlockSpec((V7X_SUBLANES, conv_width), lambda i: (jnp.maximum(i * halo_blocks - 1, 0), c))
    earlier = lax.broadcasted_iota(I32, (tm, tm), 0) < lax.broadcasted_iota(I32, (tm, tm), 1)
    tri = jnp.concatenate([earlier.astype(BF16), jnp.ones((tm, tm), BF16)], axis=1)
    lower = (lax.broadcasted_iota(I32, (n_exp, n_exp), 1) < lax.broadcasted_iota(I32, (n_exp, n_exp), 0)).astype(BF16)
    blocks = (3 * _nbytes((tm, conv_width), BF16) + _nbytes((tm, sb_width), BF16) + 3 * _nbytes((tm, d), F32))
    temps = (10 * _nbytes((tm, d), F32) + _nbytes(w_out.shape, BF16) + _nbytes(w_router.shape, BF16)
             + 3 * _nbytes(sg.shape, BF16))
    row_block = pl.BlockSpec((tm, d), lambda i: (i, 0))
    slot_block = pl.BlockSpec((TOP_K, tm), lambda i: (0, i))
    return pl.pallas_call(
        functools.partial(_mixer_out_kernel, tiles_per_seq=seq // tm),
        grid=(t // tm,),
        in_specs=[
            conv_spec(0), conv_spec(1), conv_spec(2), halo_spec(0), halo_spec(2),
            pl.BlockSpec((tm, sb_width), lambda i: (i, 0)),
            row_block,
            _resident(conv_w.shape, 1), _resident(gc.shape, 1), _resident(gs.shape, 1), _resident(w_out.shape, 1),
            _resident(gf.shape, 1), _resident(w_router.shape, 1), _resident(router_bias.shape, 1),
            _resident(tri.shape, 1), _resident(lower.shape, 1),
            _resident(sg.shape, 1), _resident(su.shape, 1), _resident(sd.shape, 1),
        ],
        out_specs=[
            row_block, pl.BlockSpec((tm,) + _slab_shape(d // 2), lambda i: (i, 0, 0)), slot_block, slot_block,
            pl.BlockSpec((tm, TOP_K), lambda i: (i, 0)),
            pl.BlockSpec((n_exp, tm), lambda i: (0, 0)),
        ],
        out_shape=[
            jax.ShapeDtypeStruct((t, d), F32),
            jax.ShapeDtypeStruct((t,) + _slab_shape(d // 2), U32),
            jax.ShapeDtypeStruct((TOP_K, t), I32),
            jax.ShapeDtypeStruct((TOP_K, t), I32),
            jax.ShapeDtypeStruct((t, TOP_K), F32),
            jax.ShapeDtypeStruct((n_exp, tm), F32),
        ],
        compiler_params=pltpu.CompilerParams(
            dimension_semantics=("arbitrary",),
            vmem_limit_bytes=_vmem_limit(blocks, temps),
        ),
        name="mixer_out",
    )(proj, proj, proj, proj, proj, sb, h, conv_w, gc, gs, w_out, gf, w_router, router_bias, tri, lower, sg, su, sd)


def _moe_layout_kernel(cnt_ref, eid_ref, rank_ref, pos_ref, off_ref, texp_ref, trows_ref, nused_ref, *, n_exp,
                       n_tiles):
    def layout(e, off):
        off_ref[e] = off
        tiles = (cnt_ref[e] + EXPERT_TILE - 1) >> EXPERT_TILE_LOG2
        first = off >> EXPERT_TILE_LOG2

        def mark(j, c):
            texp_ref[first + j] = e
            trows_ref[first + j] = jnp.minimum(cnt_ref[e] - (j << EXPERT_TILE_LOG2), EXPERT_TILE)
            return c

        lax.fori_loop(0, tiles, mark, 0)
        return off + (tiles << EXPERT_TILE_LOG2)

    total = lax.fori_loop(0, n_exp, layout, 0)
    off_ref[n_exp] = total
    used_tiles = total >> EXPERT_TILE_LOG2
    nused_ref[0] = used_tiles

    def tail(j, c):
        texp_ref[j] = n_exp - 1
        trows_ref[j] = 0
        return c

    lax.fori_loop(used_tiles, n_tiles, tail, 0)

    eid = eid_ref[...]
    pos = rank_ref[...]
    for e in range(n_exp):
        pos = pos + jnp.where(eid == e, off_ref[e], 0)
    pos_ref[...] = pos


def _moe_layout(counts, eid, rank):
    n_exp = counts.shape[0]
    t = eid.shape[1]
    n_tiles = t * TOP_K // EXPERT_TILE + n_exp
    whole = lambda shape: pl.BlockSpec(shape, lambda i, cnt: (0,) * len(shape))
    whole_smem = lambda n: pl.BlockSpec((n,), lambda i, cnt: (0,), memory_space=pltpu.SMEM)
    return pl.pallas_call(
        functools.partial(_moe_layout_kernel, n_exp=n_exp, n_tiles=n_tiles),
        grid_spec=pltpu.PrefetchScalarGridSpec(
            num_scalar_prefetch=1,
            grid=(1,),
            in_specs=[whole(eid.shape), whole(rank.shape)],
            out_specs=[whole(eid.shape), whole_smem(n_exp + 1), whole_smem(n_tiles), whole_smem(n_tiles),
                       whole_smem(1)],
        ),
        out_shape=[
            jax.ShapeDtypeStruct(eid.shape, I32),
            jax.ShapeDtypeStruct((n_exp + 1,), I32),
            jax.ShapeDtypeStruct((n_tiles,), I32),
            jax.ShapeDtypeStruct((n_tiles,), I32),
            jax.ShapeDtypeStruct((1,), I32),
        ],
        compiler_params=pltpu.CompilerParams(
            dimension_semantics=("arbitrary",),
            vmem_limit_bytes=_vmem_limit(3 * _nbytes(eid.shape, I32), 4 * _nbytes(eid.shape, I32)),
        ),
        name="moe_layout",
    )(counts, eid, rank)


def _moe_dispatch_kernel(cnt_ref, off_ref, pos_ref, xt_ref, xs_hbm, zero_ref, row_sem, pad_sem, *, n_exp, n_assigned):
    ts = xt_ref.shape[0]

    @pl.when(pl.program_id(0) == 0)
    def _():
        zero_ref[...] = jnp.zeros_like(zero_ref)

        def pad(e, c):
            def zero_row(r, c):
                pltpu.make_async_copy(zero_ref.at[0], xs_hbm.at[r], pad_sem).start()
                return c

            lax.fori_loop(off_ref[e] + cnt_ref[e], off_ref[e + 1], zero_row, 0)
            return c

        def drain_row(r, c):
            pltpu.make_async_copy(zero_ref.at[0], xs_hbm.at[0], pad_sem).wait()
            return c

        lax.fori_loop(0, n_exp, pad, 0)
        lax.fori_loop(n_assigned, off_ref[n_exp], drain_row, 0)

    def token_group(g, c):
        base = pl.multiple_of(g * V7X_SUBLANES, V7X_SUBLANES)
        for u in range(V7X_SUBLANES):
            t = base + u
            for k in range(TOP_K):
                pltpu.make_async_copy(xt_ref.at[t], xs_hbm.at[pos_ref[k, t]], row_sem).start(priority=k % DMA_THREADS)
        return c

    lax.fori_loop(0, ts // V7X_SUBLANES, token_group, 0)
    for k in range(TOP_K):
        pltpu.make_async_copy(xt_ref, xs_hbm.at[pl.ds(0, ts)], row_sem).wait()


def _moe_dispatch(counts, off, pos, xt, *, ts=512):
    t = xt.shape[0]
    slab = xt.shape[1:]
    n_exp = counts.shape[0]
    n_tiles = t * TOP_K // EXPERT_TILE + n_exp
    blocks = _nbytes((ts,) + slab, xt.dtype)
    temps = _nbytes((1,) + slab, xt.dtype)
    return pl.pallas_call(
        functools.partial(_moe_dispatch_kernel, n_exp=n_exp, n_assigned=t * TOP_K),
        grid_spec=pltpu.PrefetchScalarGridSpec(
            num_scalar_prefetch=2,
            grid=(t // ts,),
            in_specs=[
                pl.BlockSpec((TOP_K, ts), lambda i, cnt, off: (0, i), memory_space=pltpu.SMEM),
                pl.BlockSpec((ts,) + slab, lambda i, cnt, off: (i, 0, 0)),
            ],
            out_specs=pl.BlockSpec(memory_space=pl.ANY),
            scratch_shapes=[
                pltpu.VMEM((1,) + slab, xt.dtype),
                pltpu.SemaphoreType.DMA(()),
                pltpu.SemaphoreType.DMA(()),
            ],
        ),
        out_shape=jax.ShapeDtypeStruct((n_tiles * EXPERT_TILE,) + slab, xt.dtype),
        compiler_params=pltpu.CompilerParams(
            dimension_semantics=("arbitrary",),
            vmem_limit_bytes=_vmem_limit(blocks, temps),
        ),
        name="moe_dispatch",
    )(counts, off, pos, xt)


def _moe_experts_kernel(texp_ref, trows_ref, nused_ref, xs_ref, wg_hbm, wu_hbm, wd_hbm, ys_ref,
                        wg_f32, wu_f32, wd_f32, wg_bf, wu_bf, wd_bf, slot_ref, sems):
    i = pl.program_id(0)
    occupied = trows_ref[i]
    expert = texp_ref[i]
    f32_bufs = ((wg_hbm, wg_f32), (wu_hbm, wu_f32), (wd_hbm, wd_f32))

    def weight_copies(e, slot):
        return [pltpu.make_async_copy(hbm.at[e], buf.at[slot], sems.at[slot, n])
                for n, (hbm, buf) in enumerate(f32_bufs)]

    @pl.when(i == 0)
    def _():
        slot_ref[0] = 0
        for cp in weight_copies(expert, 0):
            cp.start()

    @pl.when((occupied > 0) & ((i == 0) | (expert != texp_ref[jnp.maximum(i - 1, 0)])))
    def _():
        slot = slot_ref[0]
        for cp in weight_copies(expert, slot):
            cp.wait()
        wg_bf[...] = wg_f32[slot].astype(BF16)
        wu_bf[...] = wu_f32[slot].astype(BF16)
        wd_bf[...] = wd_f32[slot].astype(BF16)
        last_tile = pl.num_programs(0) - 1
        following = lax.while_loop(
            lambda j: (j < nused_ref[0]) & (texp_ref[jnp.minimum(j, last_tile)] == expert), lambda j: j + 1, i + 1)

        @pl.when(following < nused_ref[0])
        def _():
            for cp in weight_copies(texp_ref[following], 1 - slot):
                cp.start()

        slot_ref[0] = 1 - slot

    def experts_on(rows):
        x = _unpack_rows(_load_slabs(xs_ref.at[pl.ds(0, rows)])).astype(BF16)
        hid = _swiglu_hidden(x, wg_bf[...], wu_bf[...])
        _store_slabs(ys_ref.at[pl.ds(0, rows)], _pack_rows(_dot(hid.astype(BF16), wd_bf[...])))
        if rows < EXPERT_TILE:
            ys_ref[pl.ds(rows, EXPERT_TILE - rows)] = jnp.zeros((EXPERT_TILE - rows,) + ys_ref.shape[1:], ys_ref.dtype)

    lower = 0
    for rows in (EXPERT_TILE // 4, EXPERT_TILE // 2, EXPERT_TILE):
        @pl.when((occupied > lower) & (occupied <= rows))
        def _(rows=rows):
            experts_on(rows)
        lower = rows

    @pl.when(occupied == 0)
    def _():
        ys_ref[...] = jnp.zeros_like(ys_ref)


def _moe_experts(texp, trows, nused, xs, wg, wu, wd):
    rows = xs.shape[0]
    slab = xs.shape[1:]
    d, f = wg.shape[1:]
    n_tiles = rows // EXPERT_TILE
    blocks = 2 * _nbytes((EXPERT_TILE,) + slab, U32)
    temps = (6 * _nbytes((d, f), wg.dtype) + 3 * _nbytes((d, f), BF16) + 4 * _nbytes((EXPERT_TILE, f), F32)
             + _nbytes((EXPERT_TILE, d), F32) + _nbytes((EXPERT_TILE, d), BF16))
    in_hbm = pl.BlockSpec(memory_space=pl.ANY)
    return pl.pallas_call(
        _moe_experts_kernel,
        grid_spec=pltpu.PrefetchScalarGridSpec(
            num_scalar_prefetch=3,
            grid=(n_tiles,),
            in_specs=[
                pl.BlockSpec((EXPERT_TILE,) + slab,
                             lambda i, texp, trows, nused: (jnp.minimum(i, nused[0] - 1), 0, 0)),
                in_hbm, in_hbm, in_hbm,
            ],
            out_specs=pl.BlockSpec((EXPERT_TILE,) + slab, lambda i, texp, trows, nused: (i, 0, 0)),
            scratch_shapes=[
                pltpu.VMEM((2, d, f), wg.dtype), pltpu.VMEM((2, d, f), wu.dtype), pltpu.VMEM((2, f, d), wd.dtype),
                pltpu.VMEM((d, f), BF16), pltpu.VMEM((d, f), BF16), pltpu.VMEM((f, d), BF16),
                pltpu.SMEM((1,), I32), pltpu.SemaphoreType.DMA((2, 3)),
            ],
        ),
        out_shape=jax.ShapeDtypeStruct((rows,) + slab, U32),
        compiler_params=pltpu.CompilerParams(
            dimension_semantics=("arbitrary",),
            vmem_limit_bytes=_vmem_limit(blocks, temps),
        ),
        name="moe_experts",
    )(texp, trows, nused, xs, wg, wu, wd)


GATHER_DEPTH = 3


def _moe_combine_kernel(pos_ref, pos_1_ref, pos_2_ref, gate_ref, h1_ref, p_ref, gp_ref, wg_ref, wp_ref, gf_ref,
                        ys_hbm, o_ref, buf_0, buf_1, buf_2, sem, *, final_norm):
    tc = h1_ref.shape[0]
    i = pl.program_id(0)
    bufs = (buf_0, buf_1, buf_2)

    def wait_rows(b):
        for k in range(TOP_K):
            pltpu.make_async_copy(ys_hbm.at[pl.ds(0, tc)], bufs[b].at[k], sem.at[b]).wait()

    @pl.when(i == 0)
    def _():
        def token_group(g, c):
            base = pl.multiple_of(g * V7X_SUBLANES, V7X_SUBLANES)
            for u in range(V7X_SUBLANES):
                for k in range(TOP_K):
                    t = base + u
                    pltpu.make_async_copy(ys_hbm.at[pos_ref[k, t]], buf_0.at[k, t], sem.at[0]).start()
                    pltpu.make_async_copy(ys_hbm.at[pos_1_ref[k, t]], buf_1.at[k, t], sem.at[1]).start()
            return c

        lax.fori_loop(0, tc // V7X_SUBLANES, token_group, 0)

    def step(cur):
        ahead = (cur + 2) % GATHER_DEPTH
        wait_rows(cur)
        for t in range(tc):
            for k in range(TOP_K):
                pltpu.make_async_copy(ys_hbm.at[pos_2_ref[k, t]], bufs[ahead].at[k, t],
                                      sem.at[ahead]).start(priority=k % DMA_THREADS)

        h2 = h1_ref[...]
        emb = _dot(p_ref[...].astype(BF16), wp_ref[...])
        gate = gate_ref[...]
        for k in range(TOP_K):
            h2 = h2 + gate[:, k:k + 1] * _unpack_rows(_load_slabs(bufs[cur].at[k]))

        gate_ple = _sigmoid(_dot(_rms(h2, gp_ref[...]).astype(BF16), wg_ref[...]))
        h3 = h2 + gate_ple * emb
        o_ref[...] = _rms(h3, gf_ref[...]) if final_norm else h3

        @pl.when(i + 1 == pl.num_programs(0))
        def _():
            wait_rows((cur + 1) % GATHER_DEPTH)
            wait_rows(ahead)

    for b in range(GATHER_DEPTH):
        @pl.when(i % GATHER_DEPTH == b)
        def _(b=b):
            step(b)


def _moe_combine(pos, gate, h1, p, gp, wg, wp, gf, ys, *, final_norm, tc=256):
    t, d = h1.shape
    pd = p.shape[1]
    slab = ys.shape[1:]
    row_block = pl.BlockSpec((tc, d), lambda i: (i, 0))
    blocks = 2 * _nbytes((tc, d), F32) + _nbytes((tc, pd), F32) + _nbytes((tc, V7X_LANES), F32)
    temps = (GATHER_DEPTH * _nbytes((TOP_K, tc) + slab, U32) + 8 * _nbytes((tc, d), F32) + _nbytes(wg.shape, BF16)
             + _nbytes(wp.shape, BF16))
    last = t // tc - 1
    slots_of_step = lambda j: pl.BlockSpec((TOP_K, tc), lambda i: (0, jnp.minimum(i + j, last)),
                                           memory_space=pltpu.SMEM)
    return pl.pallas_call(
        functools.partial(_moe_combine_kernel, final_norm=final_norm),
        grid=(t // tc,),
        in_specs=[
            slots_of_step(0), slots_of_step(1), slots_of_step(2),
            pl.BlockSpec((tc, TOP_K), lambda i: (i, 0)),
            row_block,
            pl.BlockSpec((tc, pd), lambda i: (i, 0)),
            _resident(gp.shape, 1), _resident(wg.shape, 1), _resident(wp.shape, 1), _resident(gf.shape, 1),
            pl.BlockSpec(memory_space=pl.ANY),
        ],
        out_specs=row_block,
        out_shape=jax.ShapeDtypeStruct((t, d), F32),
        scratch_shapes=[pltpu.VMEM((TOP_K, tc) + slab, U32)] * GATHER_DEPTH
                       + [pltpu.SemaphoreType.DMA((GATHER_DEPTH,))],
        compiler_params=pltpu.CompilerParams(
            dimension_semantics=("arbitrary",),
            vmem_limit_bytes=_vmem_limit(blocks, temps),
        ),
        name="moe_combine",
    )(pos, pos, pos, gate, h1, p, gp, wg, wp, gf, ys)


def kernel(x, p, norm_mix_g, w_in, conv_w, gnorm_conv_g, gnorm_sb_g, w_out, norm_ffn_g, w_router, router_bias,
           w_exp_gate, w_exp_up, w_exp_down, w_sh_gate, w_sh_up, w_sh_down, norm_ple_g, w_ple_gate, w_ple_proj,
           norm_final_g):
    bsz, seq, d = x.shape
    depth = p.shape[0]
    t = bsz * seq
    conv_width = conv_w.shape[-1]
    sb_width = gnorm_sb_g.shape[-1]
    n_qk = 3 * conv_width + 2 * sb_width
    assert w_in.shape[-1] == n_qk + sb_width
    assert w_router.shape[-1] % N_GROUPS == 0

    q_lo = 3 * conv_width
    col = jnp.arange(n_qk)
    q_scale = HEAD_DIM ** -0.5 * LOG2_E
    col_scale = jnp.where((col >= q_lo) & (col < q_lo + sb_width), q_scale, 1.0).astype(F32)[None, :]
    row = lambda v: v.astype(F32)[None, :]

    h = x.reshape(t, d)
    for i in range(depth):
        w_in_bf = w_in[i].astype(BF16)
        proj, vt = _norm_inproj(h, row(norm_mix_g[i]), w_in_bf[:, :n_qk], col_scale, w_in_bf[:, n_qk:].T,
                                key_block=ATTN_BLOCK)
        sb = _sb_attention(proj.reshape(bsz, seq, n_qk), vt, conv_width=conv_width, sb_width=sb_width,
                           blk=ATTN_BLOCK)
        h1, xt, eid, rank, gate, cnt = _mixer_out(
            proj, sb.reshape(t, sb_width), h, conv_w[i], row(gnorm_conv_g[i]), row(gnorm_sb_g[i]),
            w_out[i].astype(BF16), row(norm_ffn_g[i]), w_router[i], router_bias[i].astype(F32)[:, None],
            w_sh_gate[i].astype(BF16), w_sh_up[i].astype(BF16), w_sh_down[i].astype(BF16),
            seq=seq, conv_width=conv_width)
        counts = cnt[:, 0].astype(I32)
        pos, off, texp, trows, nused = _moe_layout(counts, eid, rank)
        xs = _moe_dispatch(counts, off, pos, xt)
        ys = _moe_experts(texp, trows, nused, xs, w_exp_gate[i], w_exp_up[i], w_exp_down[i])
        h = _moe_combine(pos, gate, h1, p[i].reshape(t, -1), row(norm_ple_g[i]), w_ple_gate[i].astype(BF16),
                         w_ple_proj[i].astype(BF16), row(norm_final_g), ys, final_norm=(i == depth - 1))
    return h.reshape(bsz, seq, d)
```

```python
import functools

import jax
import jax.numpy as jnp
from jax import lax
from jax.experimental import pallas as pl
from jax.experimental.pallas import tpu as pltpu

NORM_EPS = 1e-6
HEAD_DIM = 128
CONV_KERNEL = 3
N_GROUPS = 8
TOPK_GROUPS = 4
TOP_K = 8
ROUTED_SCALE = 2.5
LOG2_E = 1.4426950408889634
F32_EXP2_UNDERFLOW = -160.0

V7X_VMEM_BYTES = 64 * 1024 * 1024
V7X_SUBLANES = 8
V7X_LANES = 128
DMA_THREADS = 2

F32 = jnp.float32
BF16 = jnp.bfloat16
I32 = jnp.int32
U32 = jnp.uint32

ATTN_BLOCK = 256
EXPERT_TILE = 512
EXPERT_TILE_LOG2 = 9
assert 1 << EXPERT_TILE_LOG2 == EXPERT_TILE


def _vmem_limit(pipelined_bytes, resident_bytes):
    want = 2 * pipelined_bytes + resident_bytes
    return int(min(want, V7X_VMEM_BYTES - 4 * 1024 * 1024))


def _nbytes(shape, dtype):
    n = 1
    for s in shape:
        n *= s
    return n * jnp.dtype(dtype).itemsize


def _resident(shape, n_grid_axes, n_prefetch=0):
    zeros = (0,) * len(shape)
    return pl.BlockSpec(shape, lambda *_: zeros, pipeline_mode=pl.Buffered(1))


def _rms(y, g):
    return y * lax.rsqrt(jnp.mean(y * y, axis=-1, keepdims=True) + NORM_EPS) * g


def _sigmoid(a):
    return 1.0 / (1.0 + jnp.exp(-a))


def _softplus2(z):
    return jnp.maximum(z, 0.0) + jnp.log2(1.0 + jnp.exp2(-jnp.abs(z)))


def _dot(a, b):
    return jnp.dot(a, b, preferred_element_type=F32)


def _dot_nt(a, b):
    return lax.dot_general(a, b, (((1,), (1,)), ((), ())), preferred_element_type=F32)


def _swiglu_hidden(x, wg, wu):
    a = _dot(x, wg)
    return a * _sigmoid(a) * _dot(x, wu)


def _pack_rows(x):
    half = x.shape[1] // 2
    as_bits = lambda v: lax.bitcast_convert_type(v.astype(BF16).astype(F32), U32)
    return (as_bits(x[:, :half]) >> 16) | (as_bits(x[:, half:]) & jnp.uint32(0xFFFF0000))


def _unpack_rows(w):
    lo = lax.bitcast_convert_type(w << 16, F32)
    hi = lax.bitcast_convert_type(w & jnp.uint32(0xFFFF0000), F32)
    return jnp.concatenate([lo, hi], axis=1)


def _slab_shape(words):
    return (words // V7X_LANES, V7X_LANES)


def _store_slabs(ref, words):
    by_chunk = jnp.stack([words[:, j * V7X_LANES:(j + 1) * V7X_LANES] for j in range(ref.shape[1])], axis=0)
    ref[...] = pltpu.einshape("jrl->rjl", by_chunk)


def _load_slabs(ref):
    by_chunk = pltpu.einshape("rjl->jrl", ref[...])
    return jnp.concatenate([by_chunk[j] for j in range(ref.shape[1])], axis=1)


def _norm_inproj_kernel(x_ref, g_ref, w_ref, cs_ref, wvt_ref, o_ref, vt_ref, hn_ref, *, n_col_tiles):
    j = pl.program_id(1)

    @pl.when(j == 0)
    def _():
        hn_ref[...] = _rms(x_ref[...], g_ref[...]).astype(hn_ref.dtype)

    @pl.when(j < n_col_tiles)
    def _():
        o_ref[...] = (_dot(hn_ref[...], w_ref[...]) * cs_ref[...]).astype(o_ref.dtype)

    @pl.when(j == n_col_tiles)
    def _():
        vt = _dot_nt(wvt_ref[...], hn_ref[...]).astype(vt_ref.dtype)
        kb = vt_ref.shape[2]
        for c in range(vt_ref.shape[0]):
            vt_ref[c] = vt[:, c * kb:(c + 1) * kb]


def _norm_inproj(h, g, w, col_scale, wvt, *, key_block, tm=1024, tn=1024):
    t, d = h.shape
    n = w.shape[1]
    vw = wvt.shape[0]
    n_col_tiles = n // tn
    last = n_col_tiles - 1
    blocks = _nbytes((tm, d), F32) + _nbytes((d, tn), BF16) + _nbytes((tm, tn), BF16) + _nbytes((vw, tm), BF16)
    temps = _nbytes((tm, d), BF16) + 2 * _nbytes((tm, d), F32) + _nbytes((tm, tn), F32) + _nbytes((vw, d), BF16)
    return pl.pallas_call(
        functools.partial(_norm_inproj_kernel, n_col_tiles=n_col_tiles),
        grid=(t // tm, n_col_tiles + 1),
        in_specs=[
            pl.BlockSpec((tm, d), lambda i, j: (i, 0)),
            _resident((1, d), 2),
            pl.BlockSpec((d, tn), lambda i, j: (0, jnp.minimum(j, last))),
            pl.BlockSpec((1, tn), lambda i, j: (0, jnp.minimum(j, last))),
            _resident((vw, d), 2),
        ],
        out_specs=[
            pl.BlockSpec((tm, tn), lambda i, j: (i, jnp.minimum(j, last))),
            pl.BlockSpec((tm // key_block, vw, key_block), lambda i, j: (i, 0, 0)),
        ],
        out_shape=[jax.ShapeDtypeStruct((t, n), BF16), jax.ShapeDtypeStruct((t // key_block, vw, key_block), BF16)],
        scratch_shapes=[pltpu.VMEM((tm, d), BF16)],
        compiler_params=pltpu.CompilerParams(
            dimension_semantics=("arbitrary", "arbitrary"),
            vmem_limit_bytes=_vmem_limit(blocks, temps),
        ),
        name="norm_inproj",
    )(h, g, w, col_scale, wvt)


def _sb_attn_kernel(q_ref, k_ref, vt_ref, lo_ref, o_ref, acc_ref, run_ref, *, blk, heads):
    qi = pl.program_id(2)
    lo = lo_ref[...]
    acc_ref[...] = jnp.zeros_like(acc_ref)
    run_ref[...] = jnp.zeros_like(run_ref)

    def block(j, masked):
        start = pl.multiple_of(j * blk, blk)
        cols = [slice(h * HEAD_DIM, (h + 1) * HEAD_DIM) for h in range(heads)]
        if masked:
            causal = lax.broadcasted_iota(I32, (blk, blk), 0) < lax.broadcasted_iota(I32, (blk, blk), 1)
        zs = [_dot_nt(k_ref[0, pl.ds(start, blk), c], q_ref[0, :, c]) for c in cols]
        sps = [_softplus2(z) for z in zs]
        keeps = [jnp.where(causal, sp, 0.0) if masked else sp for sp in sps]
        css = [_dot(lo, keep.astype(BF16)) for keep in keeps]
        for h in range(heads):
            run = run_ref[h]
            run_all = jnp.concatenate([run] * (blk // V7X_SUBLANES), axis=0)
            a = jnp.exp2(zs[h] - sps[h] + css[h][:blk] + run_all)
            if masked:
                a = jnp.where(causal, a, 0.0)
            acc_ref[h] += _dot(vt_ref[j, cols[h], :], a.astype(BF16))
            run_ref[h] = run + css[h][blk:]

    block(qi, True)

    def still_visible():
        return jnp.max(run_ref[...]) > F32_EXP2_UNDERFLOW

    def body(carry):
        n, _ = carry
        block(qi - 1 - n, False)
        return n + 1, still_visible()

    lax.while_loop(lambda carry: (carry[0] < qi) & carry[1], body, (jnp.int32(0), still_visible()))
    for h in range(heads):
        o_ref[0, :, h * HEAD_DIM:(h + 1) * HEAD_DIM] = acc_ref[h].T.astype(o_ref.dtype)


def _sb_attention(proj3, vt, *, conv_width, sb_width, blk, heads=8):
    b, s, _ = proj3.shape
    gw = heads * HEAD_DIM
    groups = sb_width // gw
    q_off = 3 * conv_width // gw
    k_off = q_off + groups
    later = lax.broadcasted_iota(I32, (blk, blk), 1) > lax.broadcasted_iota(I32, (blk, blk), 0)
    lo = -jnp.concatenate([later.astype(BF16), jnp.ones((V7X_SUBLANES, blk), BF16)], axis=0)
    blocks = 2 * _nbytes((blk, gw), BF16) + 2 * _nbytes((s, gw), BF16)
    temps = heads * 8 * _nbytes((blk, blk), F32)
    return pl.pallas_call(
        functools.partial(_sb_attn_kernel, blk=blk, heads=heads),
        grid=(b, groups, s // blk),
        in_specs=[
            pl.BlockSpec((1, blk, gw), lambda bi, gi, qi: (bi, qi, q_off + gi)),
            pl.BlockSpec((1, s, gw), lambda bi, gi, qi: (bi, 0, k_off + gi)),
            pl.BlockSpec((s // blk, gw, blk), lambda bi, gi, qi: (bi, gi, 0)),
            _resident((blk + V7X_SUBLANES, blk), 3),
        ],
        out_specs=pl.BlockSpec((1, blk, gw), lambda bi, gi, qi: (bi, qi, gi)),
        out_shape=jax.ShapeDtypeStruct((b, s, sb_width), BF16),
        scratch_shapes=[pltpu.VMEM((heads, HEAD_DIM, blk), F32), pltpu.VMEM((heads, V7X_SUBLANES, blk), F32)],
        compiler_params=pltpu.CompilerParams(
            dimension_semantics=("arbitrary", "arbitrary", "arbitrary"),
            vmem_limit_bytes=_vmem_limit(blocks, temps),
        ),
        name="sb_attn",
    )(proj3, proj3, vt, lo)


def _rank_of(vals):
    n = vals.shape[0]
    idx = lax.broadcasted_iota(I32, vals.shape, 0)
    rank = jnp.zeros(vals.shape, F32)
    for other in range(n):
        o = vals[other:other + 1, :]
        rank = rank + jnp.where(o > vals, 1.0, 0.0) + jnp.where(o == vals, jnp.where(idx > other, 1.0, 0.0), 0.0)
    return rank


def _router(logits_t, bias):
    n_exp, tm = logits_t.shape
    per_group = n_exp // N_GROUPS
    scores = _sigmoid(logits_t)
    biased = scores + bias

    grouped = biased.reshape(N_GROUPS, per_group, tm)
    top1 = jnp.max(grouped, axis=1, keepdims=True)
    n_top = jnp.sum(jnp.where(grouped == top1, 1.0, 0.0), axis=1, keepdims=True)
    below = jnp.max(jnp.where(grouped < top1, grouped, -jnp.inf), axis=1, keepdims=True)
    group_score = (top1 + jnp.where(n_top >= 2.0, top1, below)).reshape(N_GROUPS, tm)

    group_ok = _rank_of(group_score) < float(TOPK_GROUPS)
    expert_ok = jnp.broadcast_to(group_ok.reshape(N_GROUPS, 1, tm), (N_GROUPS, per_group, tm)).reshape(n_exp, tm)
    chosen = jnp.where(_rank_of(jnp.where(expert_ok, biased, -jnp.inf)) < float(TOP_K), 1.0, 0.0)
    w = chosen * scores
    return chosen, w / jnp.sum(w, axis=0, keepdims=True) * ROUTED_SCALE


def _mixer_out_kernel(ch_ref, cb_ref, cc_ref, hh_ref, hc_ref, sb_ref, x_ref, cw_ref, gc_ref, gs_ref,
                      wo_ref, gf_ref, wr_ref, rb_ref, tri_ref, low_ref, sg_ref, su_ref, sd_ref,
                      h1_ref, xt_ref, eid_ref, rank_ref, gate_ref, cnt_ref, *, tiles_per_seq):
    i = pl.program_id(0)

    @pl.when(i == 0)
    def _():
        cnt_ref[...] = jnp.zeros_like(cnt_ref)

    u = cc_ref[...].astype(F32) * ch_ref[...].astype(F32)
    halo = hc_ref[...].astype(F32) * hh_ref[...].astype(F32)
    halo = jnp.where(i % tiles_per_seq == 0, 0.0, halo)
    prev1 = halo[V7X_SUBLANES - 1:V7X_SUBLANES, :]
    prev2 = halo[V7X_SUBLANES - 2:V7X_SUBLANES - 1, :]
    row = lax.broadcasted_iota(I32, u.shape, 0)
    u1 = jnp.where(row == 0, prev1, pltpu.roll(u, 1, 0))
    u2 = jnp.where(row == 0, prev2, jnp.where(row == 1, prev1, pltpu.roll(u, 2, 0)))
    cw = cw_ref[...]
    conv = cb_ref[...].astype(F32) * (cw[0:1, :] * u2 + cw[1:2, :] * u1 + cw[2:3, :] * u)

    y = jnp.concatenate([_rms(conv, gc_ref[...]), _rms(sb_ref[...].astype(F32), gs_ref[...])], axis=-1)
    h1 = x_ref[...] + _dot(y.astype(BF16), wo_ref[...])
    xt = _rms(h1, gf_ref[...])
    _store_slabs(xt_ref, _pack_rows(xt))
    xt_hi = xt.astype(BF16)

    n_exp = wr_ref.shape[1] // 2
    xt_lo = (xt - xt_hi.astype(F32)).astype(BF16)
    by_hi = _dot(xt_hi, wr_ref[...])
    logits = by_hi[:, :n_exp] + by_hi[:, n_exp:] + _dot(xt_lo, wr_ref[:, :n_exp])

    shared = _swiglu_hidden(xt_hi, sg_ref[...], su_ref[...])
    h1_ref[...] = h1 + _dot(shared.astype(BF16), sd_ref[...])

    chosen, gates = _router(logits.T, rb_ref[...])
    tm = chosen.shape[1]

    counts = _dot(chosen.astype(BF16), tri_ref[...])
    seen = cnt_ref[...]
    rank = seen + counts[:, :tm]
    cnt_ref[...] = seen + counts[:, tm:]

    slot = _dot(low_ref[...], chosen.astype(BF16))
    expert = lax.broadcasted_iota(I32, chosen.shape, 0).astype(F32)
    eids, ranks, gsel = [], [], []
    for k in range(TOP_K):
        pick = chosen * jnp.where(slot == float(k), 1.0, 0.0)
        eids.append(jnp.sum(pick * expert, axis=0, keepdims=True))
        ranks.append(jnp.sum(pick * rank, axis=0, keepdims=True))
        gsel.append(jnp.sum(pick * gates, axis=0, keepdims=True))
    eid_ref[...] = jnp.concatenate(eids, axis=0).astype(I32)
    rank_ref[...] = jnp.concatenate(ranks, axis=0).astype(I32)
    gate_ref[...] = jnp.concatenate(gsel, axis=0).T


def _mixer_out(proj, sb, h, conv_w, gc, gs, w_out, gf, w_router, router_bias, sg, su, sd, *, seq, conv_width,
               tm=256):
    t, d = h.shape
    sb_width = sb.shape[1]
    n_exp = w_router.shape[1]
    router_hi = w_router.astype(BF16)
    router_lo = (w_router - router_hi.astype(F32)).astype(BF16)
    w_router = jnp.concatenate([router_hi, router_lo], axis=1)
    halo_blocks = tm // V7X_SUBLANES
    conv_spec = lambda c: pl.BlockSpec((tm, conv_width), lambda i: (i, c))
    halo_spec = lambda c: pl.BlockSpec((V7X_SUBLANES, conv_width), lambda i: (jnp.maximum(i * halo_blocks - 1, 0), c))
    earlier = lax.broadcasted_iota(I32, (tm, tm), 0) < lax.broadcasted_iota(I32, (tm, tm), 1)
    tri = jnp.concatenate([earlier.astype(BF16), jnp.ones((tm, tm), BF16)], axis=1)
    lower = (lax.broadcasted_iota(I32, (n_exp, n_exp), 1) < lax.broadcasted_iota(I32, (n_exp, n_exp), 0)).astype(BF16)
    blocks = (3 * _nbytes((tm, conv_width), BF16) + _nbytes((tm, sb_width), BF16) + 3 * _nbytes((tm, d), F32))
    temps = (10 * _nbytes((tm, d), F32) + _nbytes(w_out.shape, BF16) + _nbytes(w_router.shape, BF16)
             + 3 * _nbytes(sg.shape, BF16))
    row_block = pl.BlockSpec((tm, d), lambda i: (i, 0))
    slot_block = pl.BlockSpec((TOP_K, tm), lambda i: (0, i))
    return pl.pallas_call(
        functools.partial(_mixer_out_kernel, tiles_per_seq=seq // tm),
        grid=(t // tm,),
        in_specs=[
            conv_spec(0), conv_spec(1), conv_spec(2), halo_spec(0), halo_spec(2),
            pl.BlockSpec((tm, sb_width), lambda i: (i, 0)),
            row_block,
            _resident(conv_w.shape, 1), _resident(gc.shape, 1), _resident(gs.shape, 1), _resident(w_out.shape, 1),
            _resident(gf.shape, 1), _resident(w_router.shape, 1), _resident(router_bias.shape, 1),
            _resident(tri.shape, 1), _resident(lower.shape, 1),
            _resident(sg.shape, 1), _resident(su.shape, 1), _resident(sd.shape, 1),
        ],
        out_specs=[
            row_block, pl.BlockSpec((tm,) + _slab_shape(d // 2), lambda i: (i, 0, 0)), slot_block, slot_block,
            pl.BlockSpec((tm, TOP_K), lambda i: (i, 0)),
            pl.BlockSpec((n_exp, tm), lambda i: (0, 0)),
        ],
        out_shape=[
            jax.ShapeDtypeStruct((t, d), F32),
            jax.ShapeDtypeStruct((t,) + _slab_shape(d // 2), U32),
            jax.ShapeDtypeStruct((TOP_K, t), I32),
            jax.ShapeDtypeStruct((TOP_K, t), I32),
            jax.ShapeDtypeStruct((t, TOP_K), F32),
            jax.ShapeDtypeStruct((n_exp, tm), F32),
        ],
        compiler_params=pltpu.CompilerParams(
            dimension_semantics=("arbitrary",),
            vmem_limit_bytes=_vmem_limit(blocks, temps),
        ),
        name="mixer_out",
    )(proj, proj, proj, proj, proj, sb, h, conv_w, gc, gs, w_out, gf, w_router, router_bias, tri, lower, sg, su, sd)


def _moe_layout_kernel(cnt_ref, eid_ref, rank_ref, pos_ref, off_ref, texp_ref, trows_ref, nused_ref, *, n_exp,
                       n_tiles):
    def layout(e, off):
        off_ref[e] = off
        tiles = (cnt_ref[e] + EXPERT_TILE - 1) >> EXPERT_TILE_LOG2
        first = off >> EXPERT_TILE_LOG2

        def mark(j, c):
            texp_ref[first + j] = e
            trows_ref[first + j] = jnp.minimum(cnt_ref[e] - (j << EXPERT_TILE_LOG2), EXPERT_TILE)
            return c

        lax.fori_loop(0, tiles, mark, 0)
        return off + (tiles << EXPERT_TILE_LOG2)

    total = lax.fori_loop(0, n_exp, layout, 0)
    off_ref[n_exp] = total
    used_tiles = total >> EXPERT_TILE_LOG2
    nused_ref[0] = used_tiles

    def tail(j, c):
        texp_ref[j] = n_exp - 1
        trows_ref[j] = 0
        return c

    lax.fori_loop(used_tiles, n_tiles, tail, 0)

    eid = eid_ref[...]
    pos = rank_ref[...]
    for e in range(n_exp):
        pos = pos + jnp.where(eid == e, off_ref[e], 0)
    pos_ref[...] = pos


def _moe_layout(counts, eid, rank):
    n_exp = counts.shape[0]
    t = eid.shape[1]
    n_tiles = t * TOP_K // EXPERT_TILE + n_exp
    whole = lambda shape: pl.BlockSpec(shape, lambda i, cnt: (0,) * len(shape))
    whole_smem = lambda n: pl.BlockSpec((n,), lambda i, cnt: (0,), memory_space=pltpu.SMEM)
    return pl.pallas_call(
        functools.partial(_moe_layout_kernel, n_exp=n_exp, n_tiles=n_tiles),
        grid_spec=pltpu.PrefetchScalarGridSpec(
            num_scalar_prefetch=1,
            grid=(1,),
            in_specs=[whole(eid.shape), whole(rank.shape)],
            out_specs=[whole(eid.shape), whole_smem(n_exp + 1), whole_smem(n_tiles), whole_smem(n_tiles),
                       whole_smem(1)],
        ),
        out_shape=[
            jax.ShapeDtypeStruct(eid.shape, I32),
            jax.ShapeDtypeStruct((n_exp + 1,), I32),
            jax.ShapeDtypeStruct((n_tiles,), I32),
            jax.ShapeDtypeStruct((n_tiles,), I32),
            jax.ShapeDtypeStruct((1,), I32),
        ],
        compiler_params=pltpu.CompilerParams(
            dimension_semantics=("arbitrary",),
            vmem_limit_bytes=_vmem_limit(3 * _nbytes(eid.shape, I32), 4 * _nbytes(eid.shape, I32)),
        ),
        name="moe_layout",
    )(counts, eid, rank)


def _moe_dispatch_kernel(cnt_ref, off_ref, pos_ref, xt_ref, xs_hbm, zero_ref, row_sem, pad_sem, *, n_exp, n_assigned):
    ts = xt_ref.shape[0]

    @pl.when(pl.program_id(0) == 0)
    def _():
        zero_ref[...] = jnp.zeros_like(zero_ref)

        def pad(e, c):
            def zero_row(r, c):
                pltpu.make_async_copy(zero_ref.at[0], xs_hbm.at[r], pad_sem).start()
                return c

            lax.fori_loop(off_ref[e] + cnt_ref[e], off_ref[e + 1], zero_row, 0)
            return c

        def drain_row(r, c):
            pltpu.make_async_copy(zero_ref.at[0], xs_hbm.at[0], pad_sem).wait()
            return c

        lax.fori_loop(0, n_exp, pad, 0)
        lax.fori_loop(n_assigned, off_ref[n_exp], drain_row, 0)

    def token_group(g, c):
        base = pl.multiple_of(g * V7X_SUBLANES, V7X_SUBLANES)
        for u in range(V7X_SUBLANES):
            t = base + u
            for k in range(TOP_K):
                pltpu.make_async_copy(xt_ref.at[t], xs_hbm.at[pos_ref[k, t]], row_sem).start(priority=k % DMA_THREADS)
        return c

    lax.fori_loop(0, ts // V7X_SUBLANES, token_group, 0)
    for k in range(TOP_K):
        pltpu.make_async_copy(xt_ref, xs_hbm.at[pl.ds(0, ts)], row_sem).wait()


def _moe_dispatch(counts, off, pos, xt, *, ts=512):
    t = xt.shape[0]
    slab = xt.shape[1:]
    n_exp = counts.shape[0]
    n_tiles = t * TOP_K // EXPERT_TILE + n_exp
    blocks = _nbytes((ts,) + slab, xt.dtype)
    temps = _nbytes((1,) + slab, xt.dtype)
    return pl.pallas_call(
        functools.partial(_moe_dispatch_kernel, n_exp=n_exp, n_assigned=t * TOP_K),
        grid_spec=pltpu.PrefetchScalarGridSpec(
            num_scalar_prefetch=2,
            grid=(t // ts,),
            in_specs=[
                pl.BlockSpec((TOP_K, ts), lambda i, cnt, off: (0, i), memory_space=pltpu.SMEM),
                pl.BlockSpec((ts,) + slab, lambda i, cnt, off: (i, 0, 0)),
            ],
            out_specs=pl.BlockSpec(memory_space=pl.ANY),
            scratch_shapes=[
                pltpu.VMEM((1,) + slab, xt.dtype),
                pltpu.SemaphoreType.DMA(()),
                pltpu.SemaphoreType.DMA(()),
            ],
        ),
        out_shape=jax.ShapeDtypeStruct((n_tiles * EXPERT_TILE,) + slab, xt.dtype),
        compiler_params=pltpu.CompilerParams(
            dimension_semantics=("arbitrary",),
            vmem_limit_bytes=_vmem_limit(blocks, temps),
        ),
        name="moe_dispatch",
    )(counts, off, pos, xt)


def _moe_experts_kernel(texp_ref, trows_ref, nused_ref, xs_ref, wg_hbm, wu_hbm, wd_hbm, ys_ref,
                        wg_f32, wu_f32, wd_f32, wg_bf, wu_bf, wd_bf, slot_ref, sems):
    i = pl.program_id(0)
    occupied = trows_ref[i]
    expert = texp_ref[i]
    f32_bufs = ((wg_hbm, wg_f32), (wu_hbm, wu_f32), (wd_hbm, wd_f32))

    def weight_copies(e, slot):
        return [pltpu.make_async_copy(hbm.at[e], buf.at[slot], sems.at[slot, n])
                for n, (hbm, buf) in enumerate(f32_bufs)]

    @pl.when(i == 0)
    def _():
        slot_ref[0] = 0
        for cp in weight_copies(expert, 0):
            cp.start()

    @pl.when((occupied > 0) & ((i == 0) | (expert != texp_ref[jnp.maximum(i - 1, 0)])))
    def _():
        slot = slot_ref[0]
        for cp in weight_copies(expert, slot):
            cp.wait()
        wg_bf[...] = wg_f32[slot].astype(BF16)
        wu_bf[...] = wu_f32[slot].astype(BF16)
        wd_bf[...] = wd_f32[slot].astype(BF16)
        last_tile = pl.num_programs(0) - 1
        following = lax.while_loop(
            lambda j: (j < nused_ref[0]) & (texp_ref[jnp.minimum(j, last_tile)] == expert), lambda j: j + 1, i + 1)

        @pl.when(following < nused_ref[0])
        def _():
            for cp in weight_copies(texp_ref[following], 1 - slot):
                cp.start()

        slot_ref[0] = 1 - slot

    def experts_on(rows):
        x = _unpack_rows(_load_slabs(xs_ref.at[pl.ds(0, rows)])).astype(BF16)
        hid = _swiglu_hidden(x, wg_bf[...], wu_bf[...])
        _store_slabs(ys_ref.at[pl.ds(0, rows)], _pack_rows(_dot(hid.astype(BF16), wd_bf[...])))
        if rows < EXPERT_TILE:
            ys_ref[pl.ds(rows, EXPERT_TILE - rows)] = jnp.zeros((EXPERT_TILE - rows,) + ys_ref.shape[1:], ys_ref.dtype)

    lower = 0
    for rows in (EXPERT_TILE // 4, EXPERT_TILE // 2, EXPERT_TILE):
        @pl.when((occupied > lower) & (occupied <= rows))
        def _(rows=rows):
            experts_on(rows)
        lower = rows

    @pl.when(occupied == 0)
    def _():
        ys_ref[...] = jnp.zeros_like(ys_ref)


def _moe_experts(texp, trows, nused, xs, wg, wu, wd):
    rows = xs.shape[0]
    slab = xs.shape[1:]
    d, f = wg.shape[1:]
    n_tiles = rows // EXPERT_TILE
    blocks = 2 * _nbytes((EXPERT_TILE,) + slab, U32)
    temps = (6 * _nbytes((d, f), wg.dtype) + 3 * _nbytes((d, f), BF16) + 4 * _nbytes((EXPERT_TILE, f), F32)
             + _nbytes((EXPERT_TILE, d), F32) + _nbytes((EXPERT_TILE, d), BF16))
    in_hbm = pl.BlockSpec(memory_space=pl.ANY)
    return pl.pallas_call(
        _moe_experts_kernel,
        grid_spec=pltpu.PrefetchScalarGridSpec(
            num_scalar_prefetch=3,
            grid=(n_tiles,),
            in_specs=[
                pl.BlockSpec((EXPERT_TILE,) + slab,
                             lambda i, texp, trows, nused: (jnp.minimum(i, nused[0] - 1), 0, 0)),
                in_hbm, in_hbm, in_hbm,
            ],
            out_specs=pl.BlockSpec((EXPERT_TILE,) + slab, lambda i, texp, trows, nused: (i, 0, 0)),
            scratch_shapes=[
                pltpu.VMEM((2, d, f), wg.dtype), pltpu.VMEM((2, d, f), wu.dtype), pltpu.VMEM((2, f, d), wd.dtype),
                pltpu.VMEM((d, f), BF16), pltpu.VMEM((d, f), BF16), pltpu.VMEM((f, d), BF16),
                pltpu.SMEM((1,), I32), pltpu.SemaphoreType.DMA((2, 3)),
            ],
        ),
        out_shape=jax.ShapeDtypeStruct((rows,) + slab, U32),
        compiler_params=pltpu.CompilerParams(
            dimension_semantics=("arbitrary",),
            vmem_limit_bytes=_vmem_limit(blocks, temps),
        ),
        name="moe_experts",
    )(texp, trows, nused, xs, wg, wu, wd)


GATHER_DEPTH = 3


def _moe_combine_kernel(pos_ref, pos_1_ref, pos_2_ref, gate_ref, h1_ref, p_ref, gp_ref, wg_ref, wp_ref, gf_ref,
                        ys_hbm, o_ref, buf_0, buf_1, buf_2, sem, *, final_norm):
    tc = h1_ref.shape[0]
    i = pl.program_id(0)
    bufs = (buf_0, buf_1, buf_2)

    def wait_rows(b):
        for k in range(TOP_K):
            pltpu.make_async_copy(ys_hbm.at[pl.ds(0, tc)], bufs[b].at[k], sem.at[b]).wait()

    @pl.when(i == 0)
    def _():
        def token_group(g, c):
            base = pl.multiple_of(g * V7X_SUBLANES, V7X_SUBLANES)
            for u in range(V7X_SUBLANES):
                for k in range(TOP_K):
                    t = base + u
                    pltpu.make_async_copy(ys_hbm.at[pos_ref[k, t]], buf_0.at[k, t], sem.at[0]).start()
                    pltpu.make_async_copy(ys_hbm.at[pos_1_ref[k, t]], buf_1.at[k, t], sem.at[1]).start()
            return c

        lax.fori_loop(0, tc // V7X_SUBLANES, token_group, 0)

    def step(cur):
        ahead = (cur + 2) % GATHER_DEPTH
        wait_rows(cur)
        for t in range(tc):
            for k in range(TOP_K):
                pltpu.make_async_copy(ys_hbm.at[pos_2_ref[k, t]], bufs[ahead].at[k, t],
                                      sem.at[ahead]).start(priority=k % DMA_THREADS)

        h2 = h1_ref[...]
        emb = _dot(p_ref[...].astype(BF16), wp_ref[...])
        gate = gate_ref[...]
        for k in range(TOP_K):
            h2 = h2 + gate[:, k:k + 1] * _unpack_rows(_load_slabs(bufs[cur].at[k]))

        gate_ple = _sigmoid(_dot(_rms(h2, gp_ref[...]).astype(BF16), wg_ref[...]))
        h3 = h2 + gate_ple * emb
        o_ref[...] = _rms(h3, gf_ref[...]) if final_norm else h3

        @pl.when(i + 1 == pl.num_programs(0))
        def _():
            wait_rows((cur + 1) % GATHER_DEPTH)
            wait_rows(ahead)

    for b in range(GATHER_DEPTH):
        @pl.when(i % GATHER_DEPTH == b)
        def _(b=b):
            step(b)


def _moe_combine(pos, gate, h1, p, gp, wg, wp, gf, ys, *, final_norm, tc=256):
    t, d = h1.shape
    pd = p.shape[1]
    slab = ys.shape[1:]
    row_block = pl.BlockSpec((tc, d), lambda i: (i, 0))
    blocks = 2 * _nbytes((tc, d), F32) + _nbytes((tc, pd), F32) + _nbytes((tc, V7X_LANES), F32)
    temps = (GATHER_DEPTH * _nbytes((TOP_K, tc) + slab, U32) + 8 * _nbytes((tc, d), F32) + _nbytes(wg.shape, BF16)
             + _nbytes(wp.shape, BF16))
    last = t // tc - 1
    slots_of_step = lambda j: pl.BlockSpec((TOP_K, tc), lambda i: (0, jnp.minimum(i + j, last)),
                                           memory_space=pltpu.SMEM)
    return pl.pallas_call(
        functools.partial(_moe_combine_kernel, final_norm=final_norm),
        grid=(t // tc,),
        in_specs=[
            slots_of_step(0), slots_of_step(1), slots_of_step(2),
            pl.BlockSpec((tc, TOP_K), lambda i: (i, 0)),
            row_block,
            pl.BlockSpec((tc, pd), lambda i: (i, 0)),
            _resident(gp.shape, 1), _resident(wg.shape, 1), _resident(wp.shape, 1), _resident(gf.shape, 1),
            pl.BlockSpec(memory_space=pl.ANY),
        ],
        out_specs=row_block,
        out_shape=jax.ShapeDtypeStruct((t, d), F32),
        scratch_shapes=[pltpu.VMEM((TOP_K, tc) + slab, U32)] * GATHER_DEPTH
                       + [pltpu.SemaphoreType.DMA((GATHER_DEPTH,))],
        compiler_params=pltpu.CompilerParams(
            dimension_semantics=("arbitrary",),
            vmem_limit_bytes=_vmem_limit(blocks, temps),
        ),
        name="moe_combine",
    )(pos, pos, pos, gate, h1, p, gp, wg, wp, gf, ys)


def kernel(x, p, norm_mix_g, w_in, conv_w, gnorm_conv_g, gnorm_sb_g, w_out, norm_ffn_g, w_router, router_bias,
           w_exp_gate, w_exp_up, w_exp_down, w_sh_gate, w_sh_up, w_sh_down, norm_ple_g, w_ple_gate, w_ple_proj,
           norm_final_g):
    bsz, seq, d = x.shape
    depth = p.shape[0]
    t = bsz * seq
    conv_width = conv_w.shape[-1]
    sb_width = gnorm_sb_g.shape[-1]
    n_qk = 3 * conv_width + 2 * sb_width
    assert w_in.shape[-1] == n_qk + sb_width
    assert w_router.shape[-1] % N_GROUPS == 0

    q_lo = 3 * conv_width
    col = jnp.arange(n_qk)
    q_scale = HEAD_DIM ** -0.5 * LOG2_E
    col_scale = jnp.where((col >= q_lo) & (col < q_lo + sb_width), q_scale, 1.0).astype(F32)[None, :]
    row = lambda v: v.astype(F32)[None, :]

    h = x.reshape(t, d)
    for i in range(depth):
        w_in_bf = w_in[i].astype(BF16)
        proj, vt = _norm_inproj(h, row(norm_mix_g[i]), w_in_bf[:, :n_qk], col_scale, w_in_bf[:, n_qk:].T,
                                key_block=ATTN_BLOCK)
        sb = _sb_attention(proj.reshape(bsz, seq, n_qk), vt, conv_width=conv_width, sb_width=sb_width,
                           blk=ATTN_BLOCK)
        h1, xt, eid, rank, gate, cnt = _mixer_out(
            proj, sb.reshape(t, sb_width), h, conv_w[i], row(gnorm_conv_g[i]), row(gnorm_sb_g[i]),
            w_out[i].astype(BF16), row(norm_ffn_g[i]), w_router[i], router_bias[i].astype(F32)[:, None],
            w_sh_gate[i].astype(BF16), w_sh_up[i].astype(BF16), w_sh_down[i].astype(BF16),
            seq=seq, conv_width=conv_width)
        counts = cnt[:, 0].astype(I32)
        pos, off, texp, trows, nused = _moe_layout(counts, eid, rank)
        xs = _moe_dispatch(counts, off, pos, xt)
        ys = _moe_experts(texp, trows, nused, xs, w_exp_gate[i], w_exp_up[i], w_exp_down[i])
        h = _moe_combine(pos, gate, h1, p[i].reshape(t, -1), row(norm_ple_g[i]), w_ple_gate[i].astype(BF16),
                         w_ple_proj[i].astype(BF16), row(norm_final_g), ys, final_norm=(i == depth - 1))
    return h.reshape(bsz, seq, d)
```

```python
import functools

import jax
import jax.numpy as jnp
from jax import lax
from jax.experimental import pallas as pl
from jax.experimental.pallas import tpu as pltpu

NORM_EPS = 1e-6
HEAD_DIM = 128
CONV_KERNEL = 3
N_GROUPS = 8
TOPK_GROUPS = 4
TOP_K = 8
ROUTED_SCALE = 2.5
LOG2_E = 1.4426950408889634
F32_EXP2_UNDERFLOW = -160.0

V7X_VMEM_BYTES = 64 * 1024 * 1024
V7X_SUBLANES = 8
V7X_LANES = 128
DMA_THREADS = 2

F32 = jnp.float32
BF16 = jnp.bfloat16
I32 = jnp.int32
U32 = jnp.uint32

ATTN_BLOCK = 256
EXPERT_TILE = 512
EXPERT_TILE_LOG2 = 9
assert 1 << EXPERT_TILE_LOG2 == EXPERT_TILE


def _vmem_limit(pipelined_bytes, resident_bytes):
    want = 2 * pipelined_bytes + resident_bytes
    return int(min(want, V7X_VMEM_BYTES - 4 * 1024 * 1024))


def _nbytes(shape, dtype):
    n = 1
    for s in shape:
        n *= s
    return n * jnp.dtype(dtype).itemsize


def _resident(shape, n_grid_axes, n_prefetch=0):
    zeros = (0,) * len(shape)
    return pl.BlockSpec(shape, lambda *_: zeros, pipeline_mode=pl.Buffered(1))


def _rms(y, g):
    return y * lax.rsqrt(jnp.mean(y * y, axis=-1, keepdims=True) + NORM_EPS) * g


def _sigmoid(a):
    return 1.0 / (1.0 + jnp.exp(-a))


def _softplus2(z):
    return jnp.maximum(z, 0.0) + jnp.log2(1.0 + jnp.exp2(-jnp.abs(z)))


def _dot(a, b):
    return jnp.dot(a, b, preferred_element_type=F32)


def _dot_nt(a, b):
    return lax.dot_general(a, b, (((1,), (1,)), ((), ())), preferred_element_type=F32)


def _swiglu_hidden(x, wg, wu):
    a = _dot(x, wg)
    return a * _sigmoid(a) * _dot(x, wu)


def _pack_rows(x):
    half = x.shape[1] // 2
    as_bits = lambda v: lax.bitcast_convert_type(v.astype(BF16).astype(F32), U32)
    return (as_bits(x[:, :half]) >> 16) | (as_bits(x[:, half:]) & jnp.uint32(0xFFFF0000))


def _unpack_rows(w):
    lo = lax.bitcast_convert_type(w << 16, F32)
    hi = lax.bitcast_convert_type(w & jnp.uint32(0xFFFF0000), F32)
    return jnp.concatenate([lo, hi], axis=1)


def _slab_shape(words):
    return (words // V7X_LANES, V7X_LANES)


def _store_slabs(ref, words):
    by_chunk = jnp.stack([words[:, j * V7X_LANES:(j + 1) * V7X_LANES] for j in range(ref.shape[1])], axis=0)
    ref[...] = pltpu.einshape("jrl->rjl", by_chunk)


def _load_slabs(ref):
    by_chunk = pltpu.einshape("rjl->jrl", ref[...])
    return jnp.concatenate([by_chunk[j] for j in range(ref.shape[1])], axis=1)


def _norm_inproj_kernel(x_ref, g_ref, w_ref, cs_ref, wvt_ref, o_ref, vt_ref, hn_ref, *, n_col_tiles):
    j = pl.program_id(1)

    @pl.when(j == 0)
    def _():
        hn_ref[...] = _rms(x_ref[...], g_ref[...]).astype(hn_ref.dtype)

    @pl.when(j < n_col_tiles)
    def _():
        o_ref[...] = (_dot(hn_ref[...], w_ref[...]) * cs_ref[...]).astype(o_ref.dtype)

    @pl.when(j == n_col_tiles)
    def _():
        vt = _dot_nt(wvt_ref[...], hn_ref[...]).astype(vt_ref.dtype)
        kb = vt_ref.shape[2]
        for c in range(vt_ref.shape[0]):
            vt_ref[c] = vt[:, c * kb:(c + 1) * kb]


def _norm_inproj(h, g, w, col_scale, wvt, *, key_block, tm=1024, tn=1024):
    t, d = h.shape
    n = w.shape[1]
    vw = wvt.shape[0]
    n_col_tiles = n // tn
    last = n_col_tiles - 1
    blocks = _nbytes((tm, d), F32) + _nbytes((d, tn), BF16) + _nbytes((tm, tn), BF16) + _nbytes((vw, tm), BF16)
    temps = _nbytes((tm, d), BF16) + 2 * _nbytes((tm, d), F32) + _nbytes((tm, tn), F32) + _nbytes((vw, d), BF16)
    return pl.pallas_call(
        functools.partial(_norm_inproj_kernel, n_col_tiles=n_col_tiles),
        grid=(t // tm, n_col_tiles + 1),
        in_specs=[
            pl.BlockSpec((tm, d), lambda i, j: (i, 0)),
            _resident((1, d), 2),
            pl.BlockSpec((d, tn), lambda i, j: (0, jnp.minimum(j, last))),
            pl.BlockSpec((1, tn), lambda i, j: (0, jnp.minimum(j, last))),
            _resident((vw, d), 2),
        ],
        out_specs=[
            pl.BlockSpec((tm, tn), lambda i, j: (i, jnp.minimum(j, last))),
            pl.BlockSpec((tm // key_block, vw, key_block), lambda i, j: (i, 0, 0)),
        ],
        out_shape=[jax.ShapeDtypeStruct((t, n), BF16), jax.ShapeDtypeStruct((t // key_block, vw, key_block), BF16)],
        scratch_shapes=[pltpu.VMEM((tm, d), BF16)],
        compiler_params=pltpu.CompilerParams(
            dimension_semantics=("arbitrary", "arbitrary"),
            vmem_limit_bytes=_vmem_limit(blocks, temps),
        ),
        name="norm_inproj",
    )(h, g, w, col_scale, wvt)


def _sb_attn_kernel(q_ref, k_ref, vt_ref, lo_ref, o_ref, acc_ref, run_ref, *, blk, heads):
    qi = pl.program_id(2)
    lo = lo_ref[...]
    acc_ref[...] = jnp.zeros_like(acc_ref)
    run_ref[...] = jnp.zeros_like(run_ref)

    def block(j, masked):
        start = pl.multiple_of(j * blk, blk)
        cols = [slice(h * HEAD_DIM, (h + 1) * HEAD_DIM) for h in range(heads)]
        if masked:
            causal = lax.broadcasted_iota(I32, (blk, blk), 0) < lax.broadcasted_iota(I32, (blk, blk), 1)
        zs = [_dot_nt(k_ref[0, pl.ds(start, blk), c], q_ref[0, :, c]) for c in cols]
        sps = [_softplus2(z) for z in zs]
        keeps = [jnp.where(causal, sp, 0.0) if masked else sp for sp in sps]
        css = [_dot(lo, keep.astype(BF16)) for keep in keeps]
        for h in range(heads):
            run = run_ref[h]
            run_all = jnp.concatenate([run] * (blk // V7X_SUBLANES), axis=0)
            a = jnp.exp2(zs[h] - sps[h] + css[h][:blk] + run_all)
            if masked:
                a = jnp.where(causal, a, 0.0)
            acc_ref[h] += _dot(vt_ref[j, cols[h], :], a.astype(BF16))
            run_ref[h] = run + css[h][blk:]

    block(qi, True)

    def still_visible():
        return jnp.max(run_ref[...]) > F32_EXP2_UNDERFLOW

    def body(carry):
        n, _ = carry
        block(qi - 1 - n, False)
        return n + 1, still_visible()

    lax.while_loop(lambda carry: (carry[0] < qi) & carry[1], body, (jnp.int32(0), still_visible()))
    for h in range(heads):
        o_ref[0, :, h * HEAD_DIM:(h + 1) * HEAD_DIM] = acc_ref[h].T.astype(o_ref.dtype)


def _sb_attention(proj3, vt, *, conv_width, sb_width, blk, heads=8):
    b, s, _ = proj3.shape
    gw = heads * HEAD_DIM
    groups = sb_width // gw
    q_off = 3 * conv_width // gw
    k_off = q_off + groups
    later = lax.broadcasted_iota(I32, (blk, blk), 1) > lax.broadcasted_iota(I32, (blk, blk), 0)
    lo = -jnp.concatenate([later.astype(BF16), jnp.ones((V7X_SUBLANES, blk), BF16)], axis=0)
    blocks = 2 * _nbytes((blk, gw), BF16) + 2 * _nbytes((s, gw), BF16)
    temps = heads * 8 * _nbytes((blk, blk), F32)
    return pl.pallas_call(
        functools.partial(_sb_attn_kernel, blk=blk, heads=heads),
        grid=(b, groups, s // blk),
        in_specs=[
            pl.BlockSpec((1, blk, gw), lambda bi, gi, qi: (bi, qi, q_off + gi)),
            pl.BlockSpec((1, s, gw), lambda bi, gi, qi: (bi, 0, k_off + gi)),
            pl.BlockSpec((s // blk, gw, blk), lambda bi, gi, qi: (bi, gi, 0)),
            _resident((blk + V7X_SUBLANES, blk), 3),
        ],
        out_specs=pl.BlockSpec((1, blk, gw), lambda bi, gi, qi: (bi, qi, gi)),
        out_shape=jax.ShapeDtypeStruct((b, s, sb_width), BF16),
        scratch_shapes=[pltpu.VMEM((heads, HEAD_DIM, blk), F32), pltpu.VMEM((heads, V7X_SUBLANES, blk), F32)],
        compiler_params=pltpu.CompilerParams(
            dimension_semantics=("arbitrary", "arbitrary", "arbitrary"),
            vmem_limit_bytes=_vmem_limit(blocks, temps),
        ),
        name="sb_attn",
    )(proj3, proj3, vt, lo)


def _rank_of(vals):
    n = vals.shape[0]
    idx = lax.broadcasted_iota(I32, vals.shape, 0)
    rank = jnp.zeros(vals.shape, F32)
    for other in range(n):
        o = vals[other:other + 1, :]
        rank = rank + jnp.where(o > vals, 1.0, 0.0) + jnp.where(o == vals, jnp.where(idx > other, 1.0, 0.0), 0.0)
    return rank


def _router(logits_t, bias):
    n_exp, tm = logits_t.shape
    per_group = n_exp // N_GROUPS
    scores = _sigmoid(logits_t)
    biased = scores + bias

    grouped = biased.reshape(N_GROUPS, per_group, tm)
    top1 = jnp.max(grouped, axis=1, keepdims=True)
    n_top = jnp.sum(jnp.where(grouped == top1, 1.0, 0.0), axis=1, keepdims=True)
    below = jnp.max(jnp.where(grouped < top1, grouped, -jnp.inf), axis=1, keepdims=True)
    group_score = (top1 + jnp.where(n_top >= 2.0, top1, below)).reshape(N_GROUPS, tm)

    group_ok = _rank_of(group_score) < float(TOPK_GROUPS)
    expert_ok = jnp.broadcast_to(group_ok.reshape(N_GROUPS, 1, tm), (N_GROUPS, per_group, tm)).reshape(n_exp, tm)
    left = jnp.where(expert_ok, biased, -jnp.inf)
    row = lax.broadcasted_iota(I32, left.shape, 0).astype(F32)
    chosen = jnp.zeros(left.shape, F32)
    for _ in range(TOP_K):
        top = jnp.max(left, axis=0, keepdims=True)
        first = jnp.min(jnp.where(left == top, row, float(n_exp)), axis=0, keepdims=True)
        pick = row == first
        chosen = jnp.where(pick, 1.0, chosen)
        left = jnp.where(pick, -jnp.inf, left)
    w = chosen * scores
    return chosen, w / jnp.sum(w, axis=0, keepdims=True) * ROUTED_SCALE


def _mixer_out_kernel(ch_ref, cb_ref, cc_ref, hh_ref, hc_ref, sb_ref, x_ref, cw_ref, gc_ref, gs_ref,
                      wo_ref, gf_ref, wr_ref, rb_ref, tri_ref, low_ref, sg_ref, su_ref, sd_ref,
                      h1_ref, xt_ref, eid_ref, rank_ref, gate_ref, cnt_ref, *, tiles_per_seq):
    i = pl.program_id(0)

    @pl.when(i == 0)
    def _():
        cnt_ref[...] = jnp.zeros_like(cnt_ref)

    u = cc_ref[...].astype(F32) * ch_ref[...].astype(F32)
    halo = hc_ref[...].astype(F32) * hh_ref[...].astype(F32)
    halo = jnp.where(i % tiles_per_seq == 0, 0.0, halo)
    prev1 = halo[V7X_SUBLANES - 1:V7X_SUBLANES, :]
    prev2 = halo[V7X_SUBLANES - 2:V7X_SUBLANES - 1, :]
    row = lax.broadcasted_iota(I32, u.shape, 0)
    u1 = jnp.where(row == 0, prev1, pltpu.roll(u, 1, 0))
    u2 = jnp.where(row == 0, prev2, jnp.where(row == 1, prev1, pltpu.roll(u, 2, 0)))
    cw = cw_ref[...]
    conv = cb_ref[...].astype(F32) * (cw[0:1, :] * u2 + cw[1:2, :] * u1 + cw[2:3, :] * u)

    y = jnp.concatenate([_rms(conv, gc_ref[...]), _rms(sb_ref[...].astype(F32), gs_ref[...])], axis=-1)
    h1 = x_ref[...] + _dot(y.astype(BF16), wo_ref[...])
    xt = _rms(h1, gf_ref[...])
    _store_slabs(xt_ref, _pack_rows(xt))
    xt_hi = xt.astype(BF16)

    n_exp = wr_ref.shape[1] // 2
    xt_lo = (xt - xt_hi.astype(F32)).astype(BF16)
    by_hi = _dot(xt_hi, wr_ref[...])
    logits = by_hi[:, :n_exp] + by_hi[:, n_exp:] + _dot(xt_lo, wr_ref[:, :n_exp])

    shared = _swiglu_hidden(xt_hi, sg_ref[...], su_ref[...])
    h1_ref[...] = h1 + _dot(shared.astype(BF16), sd_ref[...])

    chosen, gates = _router(logits.T, rb_ref[...])
    tm = chosen.shape[1]

    counts = _dot(chosen.astype(BF16), tri_ref[...])
    seen = cnt_ref[...]
    rank = seen + counts[:, :tm]
    cnt_ref[...] = seen + counts[:, tm:]

    slot = _dot(low_ref[...], chosen.astype(BF16))
    expert = lax.broadcasted_iota(I32, chosen.shape, 0).astype(F32)
    eids, ranks, gsel = [], [], []
    for k in range(TOP_K):
        pick = chosen * jnp.where(slot == float(k), 1.0, 0.0)
        eids.append(jnp.sum(pick * expert, axis=0, keepdims=True))
        ranks.append(jnp.sum(pick * rank, axis=0, keepdims=True))
        gsel.append(jnp.sum(pick * gates, axis=0, keepdims=True))
    eid_ref[...] = jnp.concatenate(eids, axis=0).astype(I32)
    rank_ref[...] = jnp.concatenate(ranks, axis=0).astype(I32)
    gate_ref[...] = jnp.concatenate(gsel, axis=0).T


def _mixer_out(proj, sb, h, conv_w, gc, gs, w_out, gf, w_router, router_bias, sg, su, sd, *, seq, conv_width,
               tm=256):
    t, d = h.shape
    sb_width = sb.shape[1]
    n_exp = w_router.shape[1]
    router_hi = w_router.astype(BF16)
    router_lo = (w_router - router_hi.astype(F32)).astype(BF16)
    w_router = jnp.concatenate([router_hi, router_lo], axis=1)
    halo_blocks = tm // V7X_SUBLANES
    conv_spec = lambda c: pl.BlockSpec((tm, conv_width), lambda i: (i, c))
    halo_spec = lambda c: pl.BlockSpec((V7X_SUBLANES, conv_width), lambda i: (jnp.maximum(i * halo_blocks - 1, 0), c))
    earlier = lax.broadcasted_iota(I32, (tm, tm), 0) < lax.broadcasted_iota(I32, (tm, tm), 1)
    tri = jnp.concatenate([earlier.astype(BF16), jnp.ones((tm, tm), BF16)], axis=1)
    lower = (lax.broadcasted_iota(I32, (n_exp, n_exp), 1) < lax.broadcasted_iota(I32, (n_exp, n_exp), 0)).astype(BF16)
    blocks = (3 * _nbytes((tm, conv_width), BF16) + _nbytes((tm, sb_width), BF16) + 3 * _nbytes((tm, d), F32))
    temps = (10 * _nbytes((tm, d), F32) + _nbytes(w_out.shape, BF16) + _nbytes(w_router.shape, BF16)
             + 3 * _nbytes(sg.shape, BF16))
    row_block = pl.BlockSpec((tm, d), lambda i: (i, 0))
    slot_block = pl.BlockSpec((TOP_K, tm), lambda i: (0, i))
    return pl.pallas_call(
        functools.partial(_mixer_out_kernel, tiles_per_seq=seq // tm),
        grid=(t // tm,),
        in_specs=[
            conv_spec(0), conv_spec(1), conv_spec(2), halo_spec(0), halo_spec(2),
            pl.BlockSpec((tm, sb_width), lambda i: (i, 0)),
            row_block,
            _resident(conv_w.shape, 1), _resident(gc.shape, 1), _resident(gs.shape, 1), _resident(w_out.shape, 1),
            _resident(gf.shape, 1), _resident(w_router.shape, 1), _resident(router_bias.shape, 1),
            _resident(tri.shape, 1), _resident(lower.shape, 1),
            _resident(sg.shape, 1), _resident(su.shape, 1), _resident(sd.shape, 1),
        ],
        out_specs=[
            row_block, pl.BlockSpec((tm,) + _slab_shape(d // 2), lambda i: (i, 0, 0)), slot_block, slot_block,
            pl.BlockSpec((tm, TOP_K), lambda i: (i, 0)),
            pl.BlockSpec((n_exp, tm), lambda i: (0, 0)),
        ],
        out_shape=[
            jax.ShapeDtypeStruct((t, d), F32),
            jax.ShapeDtypeStruct((t,) + _slab_shape(d // 2), U32),
            jax.ShapeDtypeStruct((TOP_K, t), I32),
            jax.ShapeDtypeStruct((TOP_K, t), I32),
            jax.ShapeDtypeStruct((t, TOP_K), F32),
            jax.ShapeDtypeStruct((n_exp, tm), F32),
        ],
        compiler_params=pltpu.CompilerParams(
            dimension_semantics=("arbitrary",),
            vmem_limit_bytes=_vmem_limit(blocks, temps),
        ),
        name="mixer_out",
    )(proj, proj, proj, proj, proj, sb, h, conv_w, gc, gs, w_out, gf, w_router, router_bias, tri, lower, sg, su, sd)


def _moe_layout_kernel(cnt_ref, eid_ref, rank_ref, pos_ref, off_ref, texp_ref, trows_ref, nused_ref, *, n_exp,
                       n_tiles):
    def layout(e, off):
        off_ref[e] = off
        tiles = (cnt_ref[e] + EXPERT_TILE - 1) >> EXPERT_TILE_LOG2
        first = off >> EXPERT_TILE_LOG2

        def mark(j, c):
            texp_ref[first + j] = e
            trows_ref[first + j] = jnp.minimum(cnt_ref[e] - (j << EXPERT_TILE_LOG2), EXPERT_TILE)
            return c

        lax.fori_loop(0, tiles, mark, 0)
        return off + (tiles << EXPERT_TILE_LOG2)

    total = lax.fori_loop(0, n_exp, layout, 0)
    off_ref[n_exp] = total
    used_tiles = total >> EXPERT_TILE_LOG2
    nused_ref[0] = used_tiles

    def tail(j, c):
        texp_ref[j] = n_exp - 1
        trows_ref[j] = 0
        return c

    lax.fori_loop(used_tiles, n_tiles, tail, 0)

    eid = eid_ref[...]
    pos = rank_ref[...]
    for e in range(n_exp):
        pos = pos + jnp.where(eid == e, off_ref[e], 0)
    pos_ref[...] = pos


def _moe_layout(counts, eid, rank):
    n_exp = counts.shape[0]
    t = eid.shape[1]
    n_tiles = t * TOP_K // EXPERT_TILE + n_exp
    whole = lambda shape: pl.BlockSpec(shape, lambda i, cnt: (0,) * len(shape))
    whole_smem = lambda n: pl.BlockSpec((n,), lambda i, cnt: (0,), memory_space=pltpu.SMEM)
    return pl.pallas_call(
        functools.partial(_moe_layout_kernel, n_exp=n_exp, n_tiles=n_tiles),
        grid_spec=pltpu.PrefetchScalarGridSpec(
            num_scalar_prefetch=1,
            grid=(1,),
            in_specs=[whole(eid.shape), whole(rank.shape)],
            out_specs=[whole(eid.shape), whole_smem(n_exp + 1), whole_smem(n_tiles), whole_smem(n_tiles),
                       whole_smem(1)],
        ),
        out_shape=[
            jax.ShapeDtypeStruct(eid.shape, I32),
            jax.ShapeDtypeStruct((n_exp + 1,), I32),
            jax.ShapeDtypeStruct((n_tiles,), I32),
            jax.ShapeDtypeStruct((n_tiles,), I32),
            jax.ShapeDtypeStruct((1,), I32),
        ],
        compiler_params=pltpu.CompilerParams(
            dimension_semantics=("arbitrary",),
            vmem_limit_bytes=_vmem_limit(3 * _nbytes(eid.shape, I32), 4 * _nbytes(eid.shape, I32)),
        ),
        name="moe_layout",
    )(counts, eid, rank)


def _moe_dispatch_kernel(cnt_ref, off_ref, pos_ref, xt_ref, xs_hbm, zero_ref, row_sem, pad_sem, *, n_exp, n_assigned):
    ts = xt_ref.shape[0]

    @pl.when(pl.program_id(0) == 0)
    def _():
        zero_ref[...] = jnp.zeros_like(zero_ref)

        def pad(e, c):
            def zero_row(r, c):
                pltpu.make_async_copy(zero_ref.at[0], xs_hbm.at[r], pad_sem).start()
                return c

            lax.fori_loop(off_ref[e] + cnt_ref[e], off_ref[e + 1], zero_row, 0)
            return c

        def drain_row(r, c):
            pltpu.make_async_copy(zero_ref.at[0], xs_hbm.at[0], pad_sem).wait()
            return c

        lax.fori_loop(0, n_exp, pad, 0)
        lax.fori_loop(n_assigned, off_ref[n_exp], drain_row, 0)

    def token_group(g, c):
        base = pl.multiple_of(g * V7X_SUBLANES, V7X_SUBLANES)
        for u in range(V7X_SUBLANES):
            t = base + u
            for k in range(TOP_K):
                pltpu.make_async_copy(xt_ref.at[t], xs_hbm.at[pos_ref[k, t]], row_sem).start(priority=k % DMA_THREADS)
        return c

    lax.fori_loop(0, ts // V7X_SUBLANES, token_group, 0)
    for k in range(TOP_K):
        pltpu.make_async_copy(xt_ref, xs_hbm.at[pl.ds(0, ts)], row_sem).wait()


def _moe_dispatch(counts, off, pos, xt, *, ts=512):
    t = xt.shape[0]
    slab = xt.shape[1:]
    n_exp = counts.shape[0]
    n_tiles = t * TOP_K // EXPERT_TILE + n_exp
    blocks = _nbytes((ts,) + slab, xt.dtype)
    temps = _nbytes((1,) + slab, xt.dtype)
    return pl.pallas_call(
        functools.partial(_moe_dispatch_kernel, n_exp=n_exp, n_assigned=t * TOP_K),
        grid_spec=pltpu.PrefetchScalarGridSpec(
            num_scalar_prefetch=2,
            grid=(t // ts,),
            in_specs=[
                pl.BlockSpec((TOP_K, ts), lambda i, cnt, off: (0, i), memory_space=pltpu.SMEM),
                pl.BlockSpec((ts,) + slab, lambda i, cnt, off: (i, 0, 0)),
            ],
            out_specs=pl.BlockSpec(memory_space=pl.ANY),
            scratch_shapes=[
                pltpu.VMEM((1,) + slab, xt.dtype),
                pltpu.SemaphoreType.DMA(()),
                pltpu.SemaphoreType.DMA(()),
            ],
        ),
        out_shape=jax.ShapeDtypeStruct((n_tiles * EXPERT_TILE,) + slab, xt.dtype),
        compiler_params=pltpu.CompilerParams(
            dimension_semantics=("arbitrary",),
            vmem_limit_bytes=_vmem_limit(blocks, temps),
        ),
        name="moe_dispatch",
    )(counts, off, pos, xt)


def _moe_experts_kernel(texp_ref, trows_ref, nused_ref, xs_ref, wg_hbm, wu_hbm, wd_hbm, ys_ref,
                        wg_f32, wu_f32, wd_f32, wg_bf, wu_bf, wd_bf, slot_ref, sems):
    i = pl.program_id(0)
    occupied = trows_ref[i]
    expert = texp_ref[i]
    f32_bufs = ((wg_hbm, wg_f32), (wu_hbm, wu_f32), (wd_hbm, wd_f32))

    def weight_copies(e, slot):
        return [pltpu.make_async_copy(hbm.at[e], buf.at[slot], sems.at[slot, n])
                for n, (hbm, buf) in enumerate(f32_bufs)]

    @pl.when(i == 0)
    def _():
        slot_ref[0] = 0
        for cp in weight_copies(expert, 0):
            cp.start()

    @pl.when((occupied > 0) & ((i == 0) | (expert != texp_ref[jnp.maximum(i - 1, 0)])))
    def _():
        slot = slot_ref[0]
        for cp in weight_copies(expert, slot):
            cp.wait()
        wg_bf[...] = wg_f32[slot].astype(BF16)
        wu_bf[...] = wu_f32[slot].astype(BF16)
        wd_bf[...] = wd_f32[slot].astype(BF16)
        last_tile = pl.num_programs(0) - 1
        following = lax.while_loop(
            lambda j: (j < nused_ref[0]) & (texp_ref[jnp.minimum(j, last_tile)] == expert), lambda j: j + 1, i + 1)

        @pl.when(following < nused_ref[0])
        def _():
            for cp in weight_copies(texp_ref[following], 1 - slot):
                cp.start()

        slot_ref[0] = 1 - slot

    def experts_on(rows):
        x = _unpack_rows(_load_slabs(xs_ref.at[pl.ds(0, rows)])).astype(BF16)
        hid = _swiglu_hidden(x, wg_bf[...], wu_bf[...])
        _store_slabs(ys_ref.at[pl.ds(0, rows)], _pack_rows(_dot(hid.astype(BF16), wd_bf[...])))
        if rows < EXPERT_TILE:
            ys_ref[pl.ds(rows, EXPERT_TILE - rows)] = jnp.zeros((EXPERT_TILE - rows,) + ys_ref.shape[1:], ys_ref.dtype)

    lower = 0
    for rows in (EXPERT_TILE // 4, EXPERT_TILE // 2, EXPERT_TILE):
        @pl.when((occupied > lower) & (occupied <= rows))
        def _(rows=rows):
            experts_on(rows)
        lower = rows

    @pl.when(occupied == 0)
    def _():
        ys_ref[...] = jnp.zeros_like(ys_ref)


def _moe_experts(texp, trows, nused, xs, wg, wu, wd):
    rows = xs.shape[0]
    slab = xs.shape[1:]
    d, f = wg.shape[1:]
    n_tiles = rows // EXPERT_TILE
    blocks = 2 * _nbytes((EXPERT_TILE,) + slab, U32)
    temps = (6 * _nbytes((d, f), wg.dtype) + 3 * _nbytes((d, f), BF16) + 4 * _nbytes((EXPERT_TILE, f), F32)
             + _nbytes((EXPERT_TILE, d), F32) + _nbytes((EXPERT_TILE, d), BF16))
    in_hbm = pl.BlockSpec(memory_space=pl.ANY)
    return pl.pallas_call(
        _moe_experts_kernel,
        grid_spec=pltpu.PrefetchScalarGridSpec(
            num_scalar_prefetch=3,
            grid=(n_tiles,),
            in_specs=[
                pl.BlockSpec((EXPERT_TILE,) + slab,
                             lambda i, texp, trows, nused: (jnp.minimum(i, nused[0] - 1), 0, 0)),
                in_hbm, in_hbm, in_hbm,
            ],
            out_specs=pl.BlockSpec((EXPERT_TILE,) + slab, lambda i, texp, trows, nused: (i, 0, 0)),
            scratch_shapes=[
                pltpu.VMEM((2, d, f), wg.dtype), pltpu.VMEM((2, d, f), wu.dtype), pltpu.VMEM((2, f, d), wd.dtype),
                pltpu.VMEM((d, f), BF16), pltpu.VMEM((d, f), BF16), pltpu.VMEM((f, d), BF16),
                pltpu.SMEM((1,), I32), pltpu.SemaphoreType.DMA((2, 3)),
            ],
        ),
        out_shape=jax.ShapeDtypeStruct((rows,) + slab, U32),
        compiler_params=pltpu.CompilerParams(
            dimension_semantics=("arbitrary",),
            vmem_limit_bytes=_vmem_limit(blocks, temps),
        ),
        name="moe_experts",
    )(texp, trows, nused, xs, wg, wu, wd)


GATHER_DEPTH = 3


def _moe_combine_kernel(pos_ref, pos_1_ref, pos_2_ref, gate_ref, h1_ref, p_ref, gp_ref, wg_ref, wp_ref, gf_ref,
                        ys_hbm, o_ref, buf_0, buf_1, buf_2, sem, *, final_norm):
    tc = h1_ref.shape[0]
    i = pl.program_id(0)
    bufs = (buf_0, buf_1, buf_2)

    def wait_rows(b):
        for k in range(TOP_K):
            pltpu.make_async_copy(ys_hbm.at[pl.ds(0, tc)], bufs[b].at[k], sem.at[b]).wait()

    @pl.when(i == 0)
    def _():
        def token_group(g, c):
            base = pl.multiple_of(g * V7X_SUBLANES, V7X_SUBLANES)
            for u in range(V7X_SUBLANES):
                for k in range(TOP_K):
                    t = base + u
                    pltpu.make_async_copy(ys_hbm.at[pos_ref[k, t]], buf_0.at[k, t], sem.at[0]).start()
                    pltpu.make_async_copy(ys_hbm.at[pos_1_ref[k, t]], buf_1.at[k, t], sem.at[1]).start()
            return c

        lax.fori_loop(0, tc // V7X_SUBLANES, token_group, 0)

    def step(cur):
        ahead = (cur + 2) % GATHER_DEPTH
        wait_rows(cur)
        for t in range(tc):
            for k in range(TOP_K):
                pltpu.make_async_copy(ys_hbm.at[pos_2_ref[k, t]], bufs[ahead].at[k, t],
                                      sem.at[ahead]).start(priority=k % DMA_THREADS)

        h2 = h1_ref[...]
        emb = _dot(p_ref[...].astype(BF16), wp_ref[...])
        gate = gate_ref[...]
        for k in range(TOP_K):
            h2 = h2 + gate[:, k:k + 1] * _unpack_rows(_load_slabs(bufs[cur].at[k]))

        gate_ple = _sigmoid(_dot(_rms(h2, gp_ref[...]).astype(BF16), wg_ref[...]))
        h3 = h2 + gate_ple * emb
        o_ref[...] = _rms(h3, gf_ref[...]) if final_norm else h3

        @pl.when(i + 1 == pl.num_programs(0))
        def _():
            wait_rows((cur + 1) % GATHER_DEPTH)
            wait_rows(ahead)

    for b in range(GATHER_DEPTH):
        @pl.when(i % GATHER_DEPTH == b)
        def _(b=b):
            step(b)


def _moe_combine(pos, gate, h1, p, gp, wg, wp, gf, ys, *, final_norm, tc=256):
    t, d = h1.shape
    pd = p.shape[1]
    slab = ys.shape[1:]
    row_block = pl.BlockSpec((tc, d), lambda i: (i, 0))
    blocks = 2 * _nbytes((tc, d), F32) + _nbytes((tc, pd), F32) + _nbytes((tc, V7X_LANES), F32)
    temps = (GATHER_DEPTH * _nbytes((TOP_K, tc) + slab, U32) + 8 * _nbytes((tc, d), F32) + _nbytes(wg.shape, BF16)
             + _nbytes(wp.shape, BF16))
    last = t // tc - 1
    slots_of_step = lambda j: pl.BlockSpec((TOP_K, tc), lambda i: (0, jnp.minimum(i + j, last)),
                                           memory_space=pltpu.SMEM)
    return pl.pallas_call(
        functools.partial(_moe_combine_kernel, final_norm=final_norm),
        grid=(t // tc,),
        in_specs=[
            slots_of_step(0), slots_of_step(1), slots_of_step(2),
            pl.BlockSpec((tc, TOP_K), lambda i: (i, 0)),
            row_block,
            pl.BlockSpec((tc, pd), lambda i: (i, 0)),
            _resident(gp.shape, 1), _resident(wg.shape, 1), _resident(wp.shape, 1), _resident(gf.shape, 1),
            pl.BlockSpec(memory_space=pl.ANY),
        ],
        out_specs=row_block,
        out_shape=jax.ShapeDtypeStruct((t, d), F32),
        scratch_shapes=[pltpu.VMEM((TOP_K, tc) + slab, U32)] * GATHER_DEPTH
                       + [pltpu.SemaphoreType.DMA((GATHER_DEPTH,))],
        compiler_params=pltpu.CompilerParams(
            dimension_semantics=("arbitrary",),
            vmem_limit_bytes=_vmem_limit(blocks, temps),
        ),
        name="moe_combine",
    )(pos, pos, pos, gate, h1, p, gp, wg, wp, gf, ys)


def kernel(x, p, norm_mix_g, w_in, conv_w, gnorm_conv_g, gnorm_sb_g, w_out, norm_ffn_g, w_router, router_bias,
           w_exp_gate, w_exp_up, w_exp_down, w_sh_gate, w_sh_up, w_sh_down, norm_ple_g, w_ple_gate, w_ple_proj,
           norm_final_g):
    bsz, seq, d = x.shape
    depth = p.shape[0]
    t = bsz * seq
    conv_width = conv_w.shape[-1]
    sb_width = gnorm_sb_g.shape[-1]
    n_qk = 3 * conv_width + 2 * sb_width
    assert w_in.shape[-1] == n_qk + sb_width
    assert w_router.shape[-1] % N_GROUPS == 0

    q_lo = 3 * conv_width
    col = jnp.arange(n_qk)
    q_scale = HEAD_DIM ** -0.5 * LOG2_E
    col_scale = jnp.where((col >= q_lo) & (col < q_lo + sb_width), q_scale, 1.0).astype(F32)[None, :]
    row = lambda v: v.astype(F32)[None, :]

    h = x.reshape(t, d)
    for i in range(depth):
        w_in_bf = w_in[i].astype(BF16)
        proj, vt = _norm_inproj(h, row(norm_mix_g[i]), w_in_bf[:, :n_qk], col_scale, w_in_bf[:, n_qk:].T,
                                key_block=ATTN_BLOCK)
        sb = _sb_attention(proj.reshape(bsz, seq, n_qk), vt, conv_width=conv_width, sb_width=sb_width,
                           blk=ATTN_BLOCK)
        h1, xt, eid, rank, gate, cnt = _mixer_out(
            proj, sb.reshape(t, sb_width), h, conv_w[i], row(gnorm_conv_g[i]), row(gnorm_sb_g[i]),
            w_out[i].astype(BF16), row(norm_ffn_g[i]), w_router[i], router_bias[i].astype(F32)[:, None],
            w_sh_gate[i].astype(BF16), w_sh_up[i].astype(BF16), w_sh_down[i].astype(BF16),
            seq=seq, conv_width=conv_width)
        counts = cnt[:, 0].astype(I32)
        pos, off, texp, trows, nused = _moe_layout(counts, eid, rank)
        xs = _moe_dispatch(counts, off, pos, xt)
        ys = _moe_experts(texp, trows, nused, xs, w_exp_gate[i], w_exp_up[i], w_exp_down[i])
        h = _moe_combine(pos, gate, h1, p[i].reshape(t, -1), row(norm_ple_g[i]), w_ple_gate[i].astype(BF16),
                         w_ple_proj[i].astype(BF16), row(norm_final_g), ys, final_norm=(i == depth - 1))
    return h.reshape(bsz, seq, d)
```
